```python
import math
import jax, jax.numpy as jnp
from jax import lax
import numpy as np


D_MODEL = 1024
BATCH = 8
SEQ = 4096
DEPTH = 4

N_MIXERS = 4
D_FF = 2816
MEM_LEN = 256
X_HEADS = 4
X_HEAD_DIM = 128
X_WIDTH = X_HEADS * X_HEAD_DIM
POOL_WINDOWS = (2, 4, 8, 16)
POOL_GROUPS = len(POOL_WINDOWS)
POOL_GROUP_DIM = D_MODEL // POOL_GROUPS
DIFF_HEAD_DIM = 64
DIFF_HEADS = D_MODEL // (2 * DIFF_HEAD_DIM)
DIFF_QK = DIFF_HEADS * 2 * DIFF_HEAD_DIM
DIFF_V = DIFF_HEADS * 2 * DIFF_HEAD_DIM
Q_BLOCK = 128
MLSTM_HEADS = 4
MLSTM_HEAD_DIM = D_MODEL // MLSTM_HEADS
MLSTM_CONV = 4
MLSTM_CHUNK = 64
GLA_HEADS = 4
GLA_KEY_DIM = D_MODEL // 2 // GLA_HEADS
GLA_VALUE_DIM = D_MODEL // GLA_HEADS
GLA_GATE_RANK = 16
GLA_TAU = 16.0
GLA_CHUNK = 64
NORM_EPS = 1e-6
SUBLN_EPS = 1e-5
POOL_IN = D_MODEL + X_WIDTH
DIFF_IN = 2 * DIFF_QK + DIFF_V + X_WIDTH
MLSTM_IN = 4 * D_MODEL + 2 * MLSTM_HEADS + X_WIDTH
GLA_IN = 2 * GLA_HEADS * GLA_KEY_DIM + 2 * D_MODEL + GLA_GATE_RANK + X_WIDTH
MIX_OUT = D_MODEL + X_WIDTH

kernel_name = 'hybrid_interleaved_pool_diff_mlstm_gla'


def rmsnorm(x, g, eps=NORM_EPS):
    x32 = x.astype(jnp.float32)
    y = x32 * lax.rsqrt(jnp.mean(x32 * x32, axis=-1, keepdims=True) + eps)
    return (y * g.astype(jnp.float32)).astype(x.dtype)


def swiglu(h, w_up, w_down):
    gate, up = jnp.split(h @ w_up, 2, axis=-1)
    return (jax.nn.silu(gate) * up) @ w_down


def causal_conv(x, w):
    K, C = w.shape
    return lax.conv_general_dilated(x, w[:, None, :].astype(x.dtype), (1,), [(K - 1, 0)],
                                    dimension_numbers=('NWC', 'WIO', 'NWC'), feature_group_count=C)


def alibi_slopes(n):
    return jnp.asarray([2.0 ** (-8.0 * (h + 1) / n) for h in range(n)], dtype=jnp.float32)


def memory_attention(xq, mem_k, mem_v):
    B, S, _ = xq.shape
    q = xq.reshape(B, S, X_HEADS, X_HEAD_DIM)
    s = jnp.einsum('bshd,bmhd->bhsm', q, mem_k).astype(jnp.float32) * (X_HEAD_DIM ** -0.5)
    p = jax.nn.softmax(s, axis=-1)
    o = jnp.einsum('bhsm,bmhd->bshd', p.astype(mem_v.dtype), mem_v)
    return o.reshape(B, S, X_WIDTH)


def _chunks(a, L):
    B, H, S = a.shape[:3]
    return jnp.moveaxis(a.reshape(B, H, S // L, L, *a.shape[3:]), 2, 0)


def _unchunk(a):
    NC, B, H, L = a.shape[:4]
    return jnp.moveaxis(a, 0, 2).reshape(B, H, NC * L, *a.shape[4:])


def multiscale_pool(u, w_group, scale):
    B, S, _ = u.shape
    u32 = u.astype(jnp.float32).reshape(B, S, POOL_GROUPS, POOL_GROUP_DIM)
    cs = jnp.cumsum(u32, axis=1)
    outs = []
    for g, w in enumerate(POOL_WINDOWS):
        c = cs[:, :, g]
        prev = jnp.pad(c[:, :S - w], ((0, 0), (w, 0), (0, 0)))
        count = jnp.minimum(jnp.arange(1, S + 1), w).astype(jnp.float32)
        outs.append((c - prev) / count[None, :, None] - u32[:, :, g])
    pooled = jnp.stack(outs, axis=2).astype(u.dtype)
    mixed = jnp.einsum('bsgc,gcd->bsgd', pooled, w_group).reshape(B, S, D_MODEL)
    return mixed * scale


def pool_layer(h, w_in, w_group, scale, w_out, mem_k, mem_v):
    proj = h @ w_in
    u, xq = proj[..., :D_MODEL], proj[..., D_MODEL:]
    mix = multiscale_pool(u, w_group, scale)
    xo = memory_attention(xq, mem_k, mem_v)
    return jnp.concatenate([mix, xo], axis=-1) @ w_out


def diff_attention(q, k, v, lam, slopes):
    S = q.shape[3]
    scale = DIFF_HEAD_DIM ** -0.5
    outs = []
    for start in range(0, S, Q_BLOCK):
        end = start + Q_BLOCK
        dist = (jnp.arange(start, end)[:, None] - jnp.arange(end)[None, :]).astype(jnp.float32)
        bias = jnp.where(dist >= 0, -slopes[:, None, None] * dist, -jnp.inf)
        s = jnp.einsum('bhcqd,bhckd->bhcqk', q[:, :, :, start:end], k[:, :, :, :end]).astype(jnp.float32) * scale
        p = jax.nn.softmax(s + bias[None, :, None], axis=-1)
        w = p[:, :, 0] - lam * p[:, :, 1]
        outs.append(jnp.einsum('bhqk,bhkd->bhqd', w.astype(v.dtype), v[:, :, :end]))
    return jnp.concatenate(outs, axis=2)


def diff_layer(h, w_in, lam_p, norm_g, w_out, mem_k, mem_v, slopes, layer_idx):
    B, S, _ = h.shape
    H, Dh = DIFF_HEADS, DIFF_HEAD_DIM
    proj = h @ w_in
    q = proj[..., :DIFF_QK].reshape(B, S, H, 2, Dh).transpose(0, 2, 3, 1, 4)
    k = proj[..., DIFF_QK:2 * DIFF_QK].reshape(B, S, H, 2, Dh).transpose(0, 2, 3, 1, 4)
    v = proj[..., 2 * DIFF_QK:2 * DIFF_QK + DIFF_V].reshape(B, S, H, 2 * Dh).transpose(0, 2, 1, 3)
    xq = proj[..., 2 * DIFF_QK + DIFF_V:]
    lam_init = 0.8 - 0.6 * math.exp(-0.3 * layer_idx)
    lp = lam_p.astype(jnp.float32)
    lam = jnp.exp(jnp.sum(lp[0] * lp[1])) - jnp.exp(jnp.sum(lp[2] * lp[3])) + lam_init
    o = diff_attention(q, k, v, lam, slopes)
    o = rmsnorm(o, norm_g, eps=SUBLN_EPS) * (1.0 - lam_init)
    o = o.transpose(0, 2, 1, 3).reshape(B, S, H * 2 * Dh)
    xo = memory_attention(xq, mem_k, mem_v)
    return jnp.concatenate([o, xo], axis=-1) @ w_out


def mlstm_chunkwise(q, k, v, i_pre, f_pre):
    B, H, S, Dk = q.shape
    Dv = v.shape[-1]
    L = MLSTM_CHUNK
    q = q * (Dk ** -0.5)
    lf = jax.nn.log_sigmoid(f_pre)
    causal = jnp.tril(jnp.ones((L, L), dtype=bool))

    def step(carry, inp):
        C, n, m = carry
        qc, kc, vc, ic, lfc = inp
        b = jnp.cumsum(lfc, axis=-1)
        dmat = jnp.where(causal, b[..., :, None] - b[..., None, :] + ic[..., None, :], -jnp.inf)
        inter = b + m[..., None]
        m_t = jnp.maximum(inter, jnp.max(dmat, axis=-1))
        dec = jnp.exp(inter - m_t)
        sqk = jnp.einsum('bhtd,bhsd->bhts', qc, kc) * jnp.exp(dmat - m_t[..., None])
        num = dec[..., None] * jnp.einsum('bhtd,bhde->bhte', qc, C) + jnp.einsum('bhts,bhse->bhte', sqk, vc)
        den = dec * jnp.einsum('bhtd,bhd->bht', qc, n) + jnp.sum(sqk, axis=-1)
        hc = num / jnp.maximum(jnp.abs(den), jnp.exp(-m_t))[..., None]
        bL = b[..., -1]
        gs = bL[..., None] - b + ic
        m_new = jnp.maximum(bL + m, jnp.max(gs, axis=-1))
        ws = jnp.exp(gs - m_new[..., None])
        carry_dec = jnp.exp(bL + m - m_new)
        C_new = carry_dec[..., None, None] * C + jnp.einsum('bhs,bhsd,bhse->bhde', ws, kc, vc)
        n_new = carry_dec[..., None] * n + jnp.einsum('bhs,bhsd->bhd', ws, kc)
        return (C_new, n_new, m_new), hc

    init = (jnp.zeros((B, H, Dk, Dv), jnp.float32), jnp.zeros((B, H, Dk), jnp.float32),
            jnp.zeros((B, H), jnp.float32))
    xs = (_chunks(q, L), _chunks(k, L), _chunks(v, L), _chunks(i_pre, L), _chunks(lf, L))
    _, hs = lax.scan(step, init, xs)
    return _unchunk(hs)


def mlstm_layer(h, w_in, conv_w, gate_b, norm_g, w_out, mem_k, mem_v):
    B, S, _ = h.shape
    H, Dh, D = MLSTM_HEADS, MLSTM_HEAD_DIM, D_MODEL
    proj = h @ w_in
    qk = jax.nn.silu(causal_conv(proj[..., :2 * D], conv_w))
    q, k = qk[..., :D], qk[..., D:]
    v = proj[..., 2 * D:3 * D]
    o = proj[..., 3 * D:4 * D]
    gates = proj[..., 4 * D:4 * D + 2 * H].astype(jnp.float32).reshape(B, S, 2, H) + gate_b.astype(jnp.float32)
    xq = proj[..., 4 * D + 2 * H:]

    def heads(a):
        return a.reshape(B, S, H, Dh).transpose(0, 2, 1, 3).astype(jnp.float32)

    hh = mlstm_chunkwise(heads(q), heads(k), heads(v),
                         gates[:, :, 0].transpose(0, 2, 1), gates[:, :, 1].transpose(0, 2, 1))
    hh = rmsnorm(hh.transpose(0, 2, 1, 3).astype(h.dtype), norm_g.reshape(H, Dh)).reshape(B, S, D)
    hh = hh * jax.nn.sigmoid(o)
    xo = memory_attention(xq, mem_k, mem_v)
    return jnp.concatenate([hh, xo], axis=-1) @ w_out


def gla_chunkwise(q, k, v, log_a):
    B, H, S, Dk = q.shape
    Dv = v.shape[-1]
    L = GLA_CHUNK
    q = q * (Dk ** -0.5)
    causal = jnp.tril(jnp.ones((L, L), dtype=bool))

    def step(state, inp):
        qc, kc, vc, ac = inp
        b = jnp.cumsum(ac, axis=2)
        inter = jnp.einsum('bhtd,bhde->bhte', qc * jnp.exp(b), state)
        diff = b[:, :, :, None, :] - b[:, :, None, :, :]
        decay = jnp.exp(jnp.where(causal[:, :, None], diff, -jnp.inf))
        amat = jnp.einsum('bhtd,bhsd,bhtsd->bhts', qc, kc, decay)
        intra = jnp.einsum('bhts,bhse->bhte', amat, vc)
        bL = b[:, :, -1]
        k_dec = kc * jnp.exp(bL[:, :, None, :] - b)
        new_state = jnp.exp(bL)[..., None] * state + jnp.einsum('bhsd,bhse->bhde', k_dec, vc)
        return new_state, inter + intra

    init = jnp.zeros((B, H, Dk, Dv), jnp.float32)
    xs = (_chunks(q, L), _chunks(k, L), _chunks(v, L), _chunks(log_a, L))
    _, os_ = lax.scan(step, init, xs)
    return _unchunk(os_)


def gla_layer(h, w_in, gate_w2, gate_b, norm_g, w_out, mem_k, mem_v):
    B, S, _ = h.shape
    H, Dk, Dv = GLA_HEADS, GLA_KEY_DIM, GLA_VALUE_DIM
    kw = H * Dk
    proj = h @ w_in
    q = proj[..., :kw]
    k = proj[..., kw:2 * kw]
    v = proj[..., 2 * kw:2 * kw + D_MODEL]
    g = proj[..., 2 * kw + D_MODEL:2 * kw + 2 * D_MODEL]
    z = proj[..., 2 * kw + 2 * D_MODEL:2 * kw + 2 * D_MODEL + GLA_GATE_RANK]
    xq = proj[..., 2 * kw + 2 * D_MODEL + GLA_GATE_RANK:]
    log_a = jax.nn.log_sigmoid((z @ gate_w2 + gate_b).astype(jnp.float32)) / GLA_TAU

    def heads(a, d):
        return a.reshape(B, S, H, d).transpose(0, 2, 1, 3).astype(jnp.float32)

    o = gla_chunkwise(heads(q, Dk), heads(k, Dk), heads(v, Dv), heads(log_a, Dk))
    o = rmsnorm(o.transpose(0, 2, 1, 3).astype(h.dtype), norm_g.reshape(H, Dv)).reshape(B, S, D_MODEL)
    o = o * jax.nn.silu(g)
    xo = memory_attention(xq, mem_k, mem_v)
    return jnp.concatenate([o, xo], axis=-1) @ w_out


def _layers_of(kind):
    return len(range(kind, DEPTH, N_MIXERS))


def setup_inputs(seed: int = 0) -> dict:
    key = jax.random.key(seed)
    keys = iter(jax.random.split(key, 40))

    def nrm(shape, scale):
        return jax.random.normal(next(keys), shape, jnp.float32) * scale

    def gain(shape):
        return 1.0 + nrm(shape, 0.05)

    nA, nB, nC, nD = (_layers_of(kd) for kd in range(N_MIXERS))
    D = D_MODEL
    x = nrm((BATCH, SEQ, D), 1.0)
    mem = nrm((BATCH, MEM_LEN, D), 1.0)
    norm_g = gain((DEPTH, 3, D))
    ffn_w_up = nrm((DEPTH, 2, D, 2 * D_FF), D ** -0.5)
    ffn_w_down = nrm((DEPTH, 2, D_FF, D), D_FF ** -0.5)
    mem_norm_g = gain((D,))
    mem_w_kv = nrm((DEPTH, D, 2 * X_WIDTH), D ** -0.5)
    pool_w_in = nrm((nA, D, POOL_IN), D ** -0.5)
    pool_w_group = nrm((nA, POOL_GROUPS, POOL_GROUP_DIM, POOL_GROUP_DIM), POOL_GROUP_DIM ** -0.5)
    pool_scale = gain((nA, D))
    pool_w_out = nrm((nA, MIX_OUT, D), MIX_OUT ** -0.5)
    diff_w_in = nrm((nB, D, DIFF_IN), D ** -0.5)
    diff_lambda = nrm((nB, 4, DIFF_HEAD_DIM), 0.1)
    diff_norm_g = gain((nB, 2 * DIFF_HEAD_DIM))
    diff_w_out = nrm((nB, MIX_OUT, D), MIX_OUT ** -0.5)
    mlstm_w_in = nrm((nC, D, MLSTM_IN), D ** -0.5)
    mlstm_conv_w = nrm((nC, MLSTM_CONV, 2 * D), MLSTM_CONV ** -0.5)
    i_bias = nrm((nC, MLSTM_HEADS), 0.1)
    f_bias = jnp.linspace(3.0, 6.0, MLSTM_HEADS, dtype=jnp.float32)[None] + nrm((nC, MLSTM_HEADS), 0.01)
    mlstm_gate_b = jnp.stack([i_bias, f_bias], axis=1)
    mlstm_norm_g = gain((nC, D))
    mlstm_w_out = nrm((nC, MIX_OUT, D), MIX_OUT ** -0.5)
    gla_w_in = nrm((nD, D, GLA_IN), D ** -0.5)
    gla_gate_w2 = nrm((nD, GLA_GATE_RANK, GLA_HEADS * GLA_KEY_DIM), GLA_GATE_RANK ** -0.5)
    gla_gate_b = nrm((nD, GLA_HEADS * GLA_KEY_DIM), 0.1)
    gla_norm_g = gain((nD, D))
    gla_w_out = nrm((nD, MIX_OUT, D), MIX_OUT ** -0.5)
    final_norm_g = gain((D,))
    return {'x': x, 'mem': mem, 'norm_g': norm_g, 'ffn_w_up': ffn_w_up, 'ffn_w_down': ffn_w_down,
            'mem_norm_g': mem_norm_g, 'mem_w_kv': mem_w_kv,
            'pool_w_in': pool_w_in, 'pool_w_group': pool_w_group, 'pool_scale': pool_scale, 'pool_w_out': pool_w_out,
            'diff_w_in': diff_w_in, 'diff_lambda': diff_lambda, 'diff_norm_g': diff_norm_g, 'diff_w_out': diff_w_out,
            'mlstm_w_in': mlstm_w_in, 'mlstm_conv_w': mlstm_conv_w, 'mlstm_gate_b': mlstm_gate_b,
            'mlstm_norm_g': mlstm_norm_g, 'mlstm_w_out': mlstm_w_out,
            'gla_w_in': gla_w_in, 'gla_gate_w2': gla_gate_w2, 'gla_gate_b': gla_gate_b,
            'gla_norm_g': gla_norm_g, 'gla_w_out': gla_w_out, 'final_norm_g': final_norm_g}


def reference(x, mem, norm_g, ffn_w_up, ffn_w_down, mem_norm_g, mem_w_kv,
              pool_w_in, pool_w_group, pool_scale, pool_w_out,
              diff_w_in, diff_lambda, diff_norm_g, diff_w_out,
              mlstm_w_in, mlstm_conv_w, mlstm_gate_b, mlstm_norm_g, mlstm_w_out,
              gla_w_in, gla_gate_w2, gla_gate_b, gla_norm_g, gla_w_out, final_norm_g):
    B, M, _ = mem.shape
    mem_n = rmsnorm(mem, mem_norm_g)
    slopes = alibi_slopes(DIFF_HEADS)
    for i in range(DEPTH):
        kind, j = i % N_MIXERS, i // N_MIXERS
        kv = mem_n @ mem_w_kv[i]
        mem_k = kv[..., :X_WIDTH].reshape(B, M, X_HEADS, X_HEAD_DIM)
        mem_v = kv[..., X_WIDTH:].reshape(B, M, X_HEADS, X_HEAD_DIM)
        x = x + 0.5 * swiglu(rmsnorm(x, norm_g[i, 0]), ffn_w_up[i, 0], ffn_w_down[i, 0])
        h = rmsnorm(x, norm_g[i, 1])
        if kind == 0:
            mix = pool_layer(h, pool_w_in[j], pool_w_group[j], pool_scale[j], pool_w_out[j], mem_k, mem_v)
        elif kind == 1:
            mix = diff_layer(h, diff_w_in[j], diff_lambda[j], diff_norm_g[j], diff_w_out[j], mem_k, mem_v, slopes, i)
        elif kind == 2:
            mix = mlstm_layer(h, mlstm_w_in[j], mlstm_conv_w[j], mlstm_gate_b[j], mlstm_norm_g[j], mlstm_w_out[j],
                              mem_k, mem_v)
        else:
            mix = gla_layer(h, gla_w_in[j], gla_gate_w2[j], gla_gate_b[j], gla_norm_g[j], gla_w_out[j], mem_k, mem_v)
        x = x + mix
        x = x + 0.5 * swiglu(rmsnorm(x, norm_g[i, 2]), ffn_w_up[i, 1], ffn_w_down[i, 1])
    return rmsnorm(x, final_norm_g)
```

```python
import functools
import math

import jax
import jax.numpy as jnp
from jax import lax
from jax.experimental import pallas as pl
from jax.experimental.pallas import tpu as pltpu

F32 = jnp.float32
BF16 = jnp.bfloat16

D_MODEL = 1024
D_FF = 2816
X_HEADS = 4
X_HEAD_DIM = 128
X_WIDTH = X_HEADS * X_HEAD_DIM
POOL_WINDOWS = (2, 4, 8, 16)
POOL_GROUP_DIM = D_MODEL // len(POOL_WINDOWS)
POOL_HALO = 16
DIFF_HEAD_DIM = 64
DIFF_HEADS = 8
MLSTM_HEADS = 4
MLSTM_HEAD_DIM = 256
MLSTM_CONV = 4
CONV_HALO = 8
GLA_HEADS = 4
GLA_KEY_DIM = 128
GLA_VALUE_DIM = 256
GLA_GATE_RANK = 16
GLA_TAU = 16.0
NORM_EPS = 1e-6
SUBLN_EPS = 1e-5
LANES = 128

VMEM_LIMIT = 56 * 1024 * 1024
TOKEN_TILE = 512
FF_CHUNK = 1408
ATTN_TILE = 256
SCAN_CHUNK = 128


def _params(*sem):
    return pltpu.CompilerParams(dimension_semantics=sem, vmem_limit_bytes=VMEM_LIMIT)


def _resident(shape):
    nd = len(shape)
    return pl.BlockSpec(shape, lambda *_: (0,) * nd, pipeline_mode=pl.Buffered(1))


def _rms(x, g, eps):
    return x * lax.rsqrt(jnp.mean(x * x, axis=-1, keepdims=True) + eps) * g


def _sigmoid(x):
    return 1.0 / (1.0 + jnp.exp(-x))


def _log_sigmoid(x):
    return jnp.minimum(x, 0.0) - jnp.log(1.0 + jnp.exp(-jnp.abs(x)))


def _dot(a, b):
    return jnp.dot(a, b, preferred_element_type=F32)


def _dot_nt(a, b):
    return lax.dot_general(a, b, (((1,), (1,)), ((), ())), preferred_element_type=F32)


def _split3(x):
    hi = x.astype(BF16)
    r1 = x - hi.astype(F32)
    mid = r1.astype(BF16)
    lo = (r1 - mid.astype(F32)).astype(BF16)
    return hi, mid, lo


def _tri(n, upper):
    r = lax.broadcasted_iota(jnp.int32, (n, n), 0)
    c = lax.broadcasted_iota(jnp.int32, (n, n), 1)
    keep = (r <= c) if upper else (c <= r)
    return jnp.where(keep, 1.0, 0.0).astype(BF16)


def _cumsum_rows(x):
    tri = _tri(x.shape[0], upper=False)
    hi, mid, lo = _split3(x)
    return _dot(tri, hi) + _dot(tri, mid) + _dot(tri, lo)


def _cumsum_lanes(x):
    tri = _tri(x.shape[1], upper=True)
    hi, mid, lo = _split3(x)
    return _dot(hi, tri) + _dot(mid, tri) + _dot(lo, tri)


def _ffn_kernel(x_ref, g_ref, wg_ref, wu_ref, wd_ref, *rest, final):
    o_ref = rest[-1]
    x = x_ref[...]
    h = _rms(x, g_ref[...], NORM_EPS).astype(BF16)
    acc = jnp.zeros_like(x)
    for c in range(D_FF // FF_CHUNK):
        sl = slice(c * FF_CHUNK, (c + 1) * FF_CHUNK)
        gate = _dot(h, wg_ref[:, sl])
        up = _dot(h, wu_ref[:, sl])
        act = (gate * _sigmoid(gate) * up).astype(BF16)
        acc = acc + _dot(act, wd_ref[sl, :])
    y = x + 0.5 * acc
    if final:
        y = _rms(y, rest[0][...], NORM_EPS)
    o_ref[...] = y


def _ffn(x, g, w_up, w_down, final_g=None):
    T = x.shape[0]
    wg = w_up[:, :D_FF].astype(BF16)
    wu = w_up[:, D_FF:].astype(BF16)
    wd = w_down.astype(BF16)
    tok = pl.BlockSpec((TOKEN_TILE, D_MODEL), lambda i: (i, 0))
    in_specs = [tok, _resident((1, D_MODEL)), _resident((D_MODEL, D_FF)), _resident((D_MODEL, D_FF)),
                _resident((D_FF, D_MODEL))]
    args = [x, g.reshape(1, D_MODEL), wg, wu, wd]
    if final_g is not None:
        in_specs.append(_resident((1, D_MODEL)))
        args.append(final_g.reshape(1, D_MODEL))
    return pl.pallas_call(
        functools.partial(_ffn_kernel, final=final_g is not None),
        grid=(T // TOKEN_TILE,),
        in_specs=in_specs,
        out_specs=tok,
        out_shape=jax.ShapeDtypeStruct((T, D_MODEL), F32),
        compiler_params=_params("parallel"),
        name="ffn",
    )(*args)


def _norm_proj_kernel(x_ref, g_ref, *refs, n_out):
    h = _rms(x_ref[...], g_ref[...], NORM_EPS).astype(BF16)
    for w_ref, o_ref in zip(refs[:n_out], refs[n_out:]):
        o_ref[...] = _dot(h, w_ref[...]).astype(o_ref.dtype)


def _norm_proj(x, g, pieces):
    T = x.shape[0]
    ws, out_specs, out_shapes = [], [], []
    for w, dt in pieces:
        n = w.shape[1]
        n_pad = -(-n // LANES) * LANES
        if n_pad != n:
            w = jnp.pad(w, ((0, 0), (0, n_pad - n)))
        ws.append(w.astype(BF16))
        out_specs.append(pl.BlockSpec((TOKEN_TILE, n_pad), lambda i: (i, 0)))
        out_shapes.append(jax.ShapeDtypeStruct((T, n_pad), dt))
    in_specs = [pl.BlockSpec((TOKEN_TILE, D_MODEL), lambda i: (i, 0)), _resident((1, D_MODEL))]
    in_specs += [_resident(w.shape) for w in ws]
    return pl.pallas_call(
        functools.partial(_norm_proj_kernel, n_out=len(ws)),
        grid=(T // TOKEN_TILE,),
        in_specs=in_specs,
        out_specs=out_specs,
        out_shape=out_shapes,
        compiler_params=_params("parallel"),
        name="norm_proj",
    )(x, g.reshape(1, D_MODEL), *ws)


def _attn_out_kernel(x_ref, mix_ref, xq_ref, mk_ref, mv_ref, w1_ref, w2_ref, o_ref):
    xq = xq_ref[...]
    mk = mk_ref[...]
    mv = mv_ref[...]
    outs = []
    for h in range(X_HEADS):
        sl = slice(h * X_HEAD_DIM, (h + 1) * X_HEAD_DIM)
        s = _dot_nt(xq[:, sl], mk[:, sl]) * (X_HEAD_DIM ** -0.5)
        e = jnp.exp(s - jnp.max(s, axis=-1, keepdims=True))
        p = e * (1.0 / jnp.sum(e, axis=-1, keepdims=True))
        outs.append(_dot(p.astype(BF16), mv[:, sl]).astype(BF16))
    xo = jnp.concatenate(outs, axis=-1)
    o_ref[...] = x_ref[...] + _dot(mix_ref[...], w1_ref[...]) + _dot(xo, w2_ref[...])


def _attn_out(x, mix, xq, mem_k, mem_v, w_out):
    B, S, _ = x.shape
    M = mem_k.shape[1]
    w1 = w_out[:D_MODEL].astype(BF16)
    w2 = w_out[D_MODEL:].astype(BF16)
    tok = lambda n: pl.BlockSpec((None, TOKEN_TILE, n), lambda b, i: (b, i, 0))
    mem = pl.BlockSpec((None, M, X_WIDTH), lambda b, i: (b, 0, 0))
    return pl.pallas_call(
        _attn_out_kernel,
        grid=(B, S // TOKEN_TILE),
        in_specs=[tok(D_MODEL), tok(D_MODEL), tok(X_WIDTH), mem, mem, _resident(w1.shape), _resident(w2.shape)],
        out_specs=tok(D_MODEL),
        out_shape=jax.ShapeDtypeStruct((B, S, D_MODEL), F32),
        compiler_params=_params("parallel", "parallel"),
        name="attn_out",
    )(x, mix, xq, mem_k, mem_v, w1, w2)


def _pool_kernel(u_ref, wg_ref, sc_ref, o_ref, ext_ref, *, ts):
    j = pl.program_id(1)

    @pl.when(j == 0)
    def _():
        ext_ref[0:POOL_HALO, :] = jnp.zeros((POOL_HALO, D_MODEL), F32)

    ext_ref[POOL_HALO:POOL_HALO + ts, :] = u_ref[...]
    pos = j * ts + lax.broadcasted_iota(jnp.int32, (ts, 1), 0)
    for g, w in enumerate(POOL_WINDOWS):
        sl = slice(g * POOL_GROUP_DIM, (g + 1) * POOL_GROUP_DIM)
        u = u_ref[:, sl]
        acc = u
        for k in range(1, w):
            acc = acc + ext_ref[POOL_HALO - k:POOL_HALO - k + ts, sl]
        inv = 1.0 / jnp.minimum(pos + 1, w).astype(F32)
        pooled = (acc * inv - u).astype(BF16)
        o_ref[:, sl] = (_dot(pooled, wg_ref[g]) * sc_ref[:, sl]).astype(o_ref.dtype)
    ext_ref[0:POOL_HALO, :] = ext_ref[ts:ts + POOL_HALO, :]


def _pool_mix(u, w_group, scale):
    B, S, _ = u.shape
    ts = TOKEN_TILE
    tok = pl.BlockSpec((None, ts, D_MODEL), lambda b, j: (b, j, 0))
    return pl.pallas_call(
        functools.partial(_pool_kernel, ts=ts),
        grid=(B, S // ts),
        in_specs=[tok, _resident(w_group.shape), _resident((1, D_MODEL))],
        out_specs=tok,
        out_shape=jax.ShapeDtypeStruct((B, S, D_MODEL), BF16),
        scratch_shapes=[pltpu.VMEM((ts + POOL_HALO, D_MODEL), F32)],
        compiler_params=_params("parallel", "arbitrary"),
        name="pool_mix",
    )(u, w_group.astype(BF16), scale.reshape(1, D_MODEL))


def _diff_kernel(slopes_ref, lam_ref, q_ref, k_ref, v_ref, ng_ref, o_ref,
                 acc_ref, m_ref, l_ref, *, t, lam_init):
    h = pl.program_id(1)
    qi = pl.program_id(2)
    slope = slopes_ref[h]
    q = q_ref[...]
    lane = lax.broadcasted_iota(jnp.int32, q.shape, 1)
    zero = jnp.zeros_like(q)
    qs = (jnp.where(lane < DIFF_HEAD_DIM, q, zero), jnp.where(lane >= DIFF_HEAD_DIM, q, zero))
    acc_ref[...] = jnp.zeros_like(acc_ref)
    l_ref[...] = jnp.zeros_like(l_ref)
    m_ref[...] = jnp.full(m_ref.shape, -jnp.inf, F32)
    row = lax.broadcasted_iota(jnp.int32, (t, t), 0)
    col = lax.broadcasted_iota(jnp.int32, (t, t), 1)

    def body(kj, carry):
        start = pl.multiple_of(kj * t, t)
        kt = k_ref[pl.ds(start, t), :]
        vt = v_ref[pl.ds(start, t), :]
        dist = (row - col) + (qi - kj) * t
        bias = jnp.where(dist >= 0, -slope * dist.astype(F32), -jnp.inf)
        for c in range(2):
            s = _dot_nt(qs[c], kt) * (DIFF_HEAD_DIM ** -0.5) + bias
            m_prev = m_ref[c]
            m_new = jnp.maximum(m_prev, jnp.max(s, axis=-1, keepdims=True))
            alpha = jnp.exp(m_prev - m_new)
            p = jnp.exp(s - m_new)
            l_ref[c] = alpha * l_ref[c] + jnp.sum(p, axis=-1, keepdims=True)
            acc_ref[c] = alpha * acc_ref[c] + _dot(p.astype(BF16), vt)
            m_ref[c] = m_new
        return carry

    lax.fori_loop(0, qi + 1, body, 0)
    lp = lam_ref[...]
    lam = (jnp.exp(jnp.sum(lp[0:1] * lp[1:2], axis=-1, keepdims=True))
           - jnp.exp(jnp.sum(lp[2:3] * lp[3:4], axis=-1, keepdims=True)) + lam_init)
    o = acc_ref[0] * (1.0 / l_ref[0]) - lam * (acc_ref[1] * (1.0 / l_ref[1]))
    o_ref[...] = (_rms(o, ng_ref[...], SUBLN_EPS) * (1.0 - lam_init)).astype(o_ref.dtype)


def _diff_mix(q, k, v, lam_p, norm_g, layer_idx):
    B, S, _ = q.shape
    t = ATTN_TILE
    hd = 2 * DIFF_HEAD_DIM
    lam_init = 0.8 - 0.6 * math.exp(-0.3 * layer_idx)
    slopes = jnp.asarray([2.0 ** (-8.0 * (h + 1) / DIFF_HEADS) for h in range(DIFF_HEADS)], dtype=F32)
    qspec = pl.BlockSpec((None, t, hd), lambda b, h, i: (b, i, h))
    kvspec = pl.BlockSpec((None, S, hd), lambda b, h, i: (b, 0, h))
    return pl.pallas_call(
        functools.partial(_diff_kernel, t=t, lam_init=lam_init),
        grid=(B, DIFF_HEADS, S // t),
        in_specs=[pl.BlockSpec(memory_space=pltpu.SMEM), _resident(lam_p.shape), qspec, kvspec, kvspec,
                  _resident((1, hd))],
        out_specs=qspec,
        out_shape=jax.ShapeDtypeStruct((B, S, D_MODEL), BF16),
        scratch_shapes=[pltpu.VMEM((2, t, hd), F32), pltpu.VMEM((2, t, 1), F32), pltpu.VMEM((2, t, 1), F32)],
        compiler_params=_params("parallel", "parallel", "arbitrary"),
        name="diff_attn",
    )(slopes, lam_p, q, k, v, norm_g.reshape(1, hd))


def _mlstm_kernel(qk_ref, v_ref, og_ref, gc_ref, gr_ref, cw_ref, gbc_ref, gbr_ref, ng_ref, o_ref,
                  ext_ref, c_ref, n_ref, m_ref, *, L):
    c = pl.program_id(1)
    W = 2 * D_MODEL

    @pl.when(c == 0)
    def _():
        ext_ref[0:CONV_HALO, :] = jnp.zeros((CONV_HALO, W), F32)
        c_ref[...] = jnp.zeros_like(c_ref)
        n_ref[...] = jnp.zeros_like(n_ref)
        m_ref[...] = jnp.zeros_like(m_ref)

    ext_ref[CONV_HALO:CONV_HALO + L, :] = qk_ref[...]
    conv = cw_ref[MLSTM_CONV - 1:MLSTM_CONV, :] * qk_ref[...]
    for j in range(1, MLSTM_CONV):
        conv = conv + (cw_ref[MLSTM_CONV - 1 - j:MLSTM_CONV - j, :]
                       * ext_ref[CONV_HALO - j:CONV_HALO - j + L, :])
    ext_ref[0:CONV_HALO, :] = ext_ref[L:L + CONV_HALO, :]
    qk = conv * _sigmoid(conv)

    gc = gc_ref[...] + gbc_ref[...]
    gr = gr_ref[...] + gbr_ref[...]
    b_c = _cumsum_rows(_log_sigmoid(gc))
    b_r = _cumsum_lanes(_log_sigmoid(gr))
    row = lax.broadcasted_iota(jnp.int32, (L, L), 0)
    col = lax.broadcasted_iota(jnp.int32, (L, L), 1)
    causal = col <= row

    for hd in range(MLSTM_HEADS):
        sl = slice(hd * MLSTM_HEAD_DIM, (hd + 1) * MLSTM_HEAD_DIM)
        q32 = qk[:, sl] * (MLSTM_HEAD_DIM ** -0.5)
        k32 = qk[:, D_MODEL + hd * MLSTM_HEAD_DIM:D_MODEL + (hd + 1) * MLSTM_HEAD_DIM]
        vh = v_ref[:, sl]
        fi = MLSTM_HEADS + hd
        bcol, icol = b_c[:, fi:fi + 1], gc[:, hd:hd + 1]
        brow, irow = b_r[fi:fi + 1, :], gr[hd:hd + 1, :]
        b_last = brow[:, L - 1:L]
        m = m_ref[hd:hd + 1, 0:1]
        cmat = c_ref[hd]
        nrow = n_ref[hd]

        dmat = jnp.where(causal, bcol - brow + irow, -jnp.inf)
        inter = bcol + m
        m_t = jnp.maximum(inter, jnp.max(dmat, axis=-1, keepdims=True))
        dec = jnp.exp(inter - m_t)
        qb = q32.astype(BF16)
        k_t = k32.T
        sqk = _dot(qb, k_t.astype(BF16)) * jnp.exp(dmat - m_t)
        num = dec * _dot(qb, cmat.astype(BF16)) + _dot(sqk.astype(BF16), vh)
        den = dec * jnp.sum(q32 * nrow, axis=-1, keepdims=True) + jnp.sum(sqk, axis=-1, keepdims=True)
        hc = num * (1.0 / jnp.maximum(jnp.abs(den), jnp.exp(-m_t)))

        gs_r = b_last - brow + irow
        gs_c = b_last - bcol + icol
        m_new = jnp.maximum(b_last + m, jnp.max(gs_r, axis=-1, keepdims=True))
        carry_dec = jnp.exp(b_last + m - m_new)
        c_ref[hd] = carry_dec * cmat + _dot((k_t * jnp.exp(gs_r - m_new)).astype(BF16), vh)
        n_ref[hd] = carry_dec * nrow + jnp.sum(k32 * jnp.exp(gs_c - m_new), axis=0, keepdims=True)
        m_ref[hd:hd + 1, :] = jnp.broadcast_to(m_new, (1, LANES))

        o_ref[:, sl] = (_rms(hc, ng_ref[:, sl], NORM_EPS) * _sigmoid(og_ref[:, sl])).astype(o_ref.dtype)


def _mlstm_mix(qk, v, og, gates, conv_w, gate_b, norm_g):
    B, S, _ = qk.shape
    L = SCAN_CHUNK
    H = MLSTM_HEADS
    gates_r = jnp.transpose(gates[..., :2 * H], (0, 2, 1))
    gb = gate_b.reshape(2 * H)
    gb_c = jnp.pad(gb, (0, LANES - 2 * H)).reshape(1, LANES)
    gb_r = gb.reshape(2 * H, 1)
    tok = lambda n: pl.BlockSpec((None, L, n), lambda b, c: (b, c, 0))
    return pl.pallas_call(
        functools.partial(_mlstm_kernel, L=L),
        grid=(B, S // L),
        in_specs=[tok(2 * D_MODEL), tok(D_MODEL), tok(D_MODEL), tok(LANES),
                  pl.BlockSpec((None, 2 * H, L), lambda b, c: (b, 0, c)),
                  _resident(conv_w.shape), _resident((1, LANES)), _resident((2 * H, 1)), _resident((1, D_MODEL))],
        out_specs=tok(D_MODEL),
        out_shape=jax.ShapeDtypeStruct((B, S, D_MODEL), BF16),
        scratch_shapes=[pltpu.VMEM((L + CONV_HALO, 2 * D_MODEL), F32),
                        pltpu.VMEM((H, MLSTM_HEAD_DIM, MLSTM_HEAD_DIM), F32),
                        pltpu.VMEM((H, 1, MLSTM_HEAD_DIM), F32),
                        pltpu.VMEM((8, LANES), F32)],
        compiler_params=_params("parallel", "arbitrary"),
        name="mlstm",
    )(qk, v, og, gates, gates_r, conv_w, gb_c, gb_r, norm_g.reshape(1, D_MODEL))


def _gla_kernel(q_ref, k_ref, v_ref, g_ref, z_ref, w2_ref, gb_ref, ng_ref, o_ref, st_ref, *, L):
    c = pl.program_id(1)

    @pl.when(c == 0)
    def _():
        st_ref[...] = jnp.zeros_like(st_ref)

    log_a = _log_sigmoid(_dot(z_ref[...], w2_ref[...]) + gb_ref[...]) * (1.0 / GLA_TAU)
    b_all = _cumsum_rows(log_a)
    row = lax.broadcasted_iota(jnp.int32, (L, L), 0)
    col = lax.broadcasted_iota(jnp.int32, (L, L), 1)
    causal = col <= row

    for hd in range(GLA_HEADS):
        ks = slice(hd * GLA_KEY_DIM, (hd + 1) * GLA_KEY_DIM)
        vs = slice(hd * GLA_VALUE_DIM, (hd + 1) * GLA_VALUE_DIM)
        b = b_all[:, ks]
        vh = v_ref[:, vs]
        st = st_ref[hd]
        qe = (q_ref[:, ks] * (GLA_KEY_DIM ** -0.5) * jnp.exp(b)).astype(BF16)
        b_t = b.T
        k_t = k_ref[:, ks].T
        b_last = b_t[:, L - 1:L]
        amat = jnp.where(causal, _dot(qe, (k_t * jnp.exp(-b_t)).astype(BF16)), 0.0)
        o = _dot(qe, st.astype(BF16)) + _dot(amat.astype(BF16), vh)
        st_ref[hd] = jnp.exp(b_last) * st + _dot((k_t * jnp.exp(b_last - b_t)).astype(BF16), vh)
        gate = g_ref[:, vs]
        o_ref[:, vs] = (_rms(o, ng_ref[:, vs], NORM_EPS) * (gate * _sigmoid(gate))).astype(o_ref.dtype)


def _gla_mix(q, k, v, g, z, gate_w2, gate_b, norm_g):
    B, S, _ = q.shape
    L = SCAN_CHUNK
    kw = GLA_HEADS * GLA_KEY_DIM
    w2 = jnp.pad(gate_w2, ((0, LANES - GLA_GATE_RANK), (0, 0))).astype(BF16)
    tok = lambda n: pl.BlockSpec((None, L, n), lambda b, c: (b, c, 0))
    return pl.pallas_call(
        functools.partial(_gla_kernel, L=L),
        grid=(B, S // L),
        in_specs=[tok(kw), tok(kw), tok(D_MODEL), tok(D_MODEL), tok(LANES),
                  _resident(w2.shape), _resident((1, kw)), _resident((1, D_MODEL))],
        out_specs=tok(D_MODEL),
        out_shape=jax.ShapeDtypeStruct((B, S, D_MODEL), BF16),
        scratch_shapes=[pltpu.VMEM((GLA_HEADS, GLA_KEY_DIM, GLA_VALUE_DIM), F32)],
        compiler_params=_params("parallel", "arbitrary"),
        name="gla",
    )(q, k, v, g, z, w2, gate_b.reshape(1, kw), norm_g.reshape(1, D_MODEL))


def kernel(x, mem, norm_g, ffn_w_up, ffn_w_down, mem_norm_g, mem_w_kv, pool_w_in, pool_w_group, pool_scale, pool_w_out, diff_w_in, diff_lambda, diff_norm_g, diff_w_out, mlstm_w_in, mlstm_conv_w, mlstm_gate_b, mlstm_norm_g, mlstm_w_out, gla_w_in, gla_gate_w2, gla_gate_b, gla_norm_g, gla_w_out, final_norm_g):
    B, S, D = x.shape
    M = mem.shape[1]
    T = B * S
    depth = norm_g.shape[0]
    n_mixers = 4
    mem2 = mem.reshape(B * M, D)
    x2 = x.reshape(T, D)

    def seq(a):
        return a.reshape(B, S, a.shape[-1])

    for i in range(depth):
        kind, j = i % n_mixers, i // n_mixers
        mem_k, mem_v = _norm_proj(mem2, mem_norm_g, [(mem_w_kv[i][:, :X_WIDTH], BF16), (mem_w_kv[i][:, X_WIDTH:], BF16)])
        mem_k = mem_k.reshape(B, M, X_WIDTH)
        mem_v = mem_v.reshape(B, M, X_WIDTH)
        x2 = _ffn(x2, norm_g[i, 0], ffn_w_up[i, 0], ffn_w_down[i, 0])
        if kind == 0:
            w = pool_w_in[j]
            u, xq = _norm_proj(x2, norm_g[i, 1], [(w[:, :D], F32), (w[:, D:], BF16)])
            mix = _pool_mix(seq(u), pool_w_group[j], pool_scale[j])
            w_out = pool_w_out[j]
        elif kind == 1:
            w = diff_w_in[j]
            q, k, v, xq = _norm_proj(x2, norm_g[i, 1], [(w[:, :D], BF16), (w[:, D:2 * D], BF16),
                                                        (w[:, 2 * D:3 * D], BF16), (w[:, 3 * D:], BF16)])
            mix = _diff_mix(seq(q), seq(k), seq(v), diff_lambda[j], diff_norm_g[j], i)
            w_out = diff_w_out[j]
        elif kind == 2:
            w = mlstm_w_in[j]
            ng = 2 * MLSTM_HEADS
            qk, v, og, gates, xq = _norm_proj(x2, norm_g[i, 1], [
                (w[:, :2 * D], F32), (w[:, 2 * D:3 * D], BF16), (w[:, 3 * D:4 * D], F32),
                (w[:, 4 * D:4 * D + ng], F32), (w[:, 4 * D + ng:], BF16)])
            mix = _mlstm_mix(seq(qk), seq(v), seq(og), seq(gates), mlstm_conv_w[j], mlstm_gate_b[j], mlstm_norm_g[j])
            w_out = mlstm_w_out[j]
        else:
            w = gla_w_in[j]
            kw = GLA_HEADS * GLA_KEY_DIM
            o0 = 2 * kw + 2 * D
            q, k, v, g, z, xq = _norm_proj(x2, norm_g[i, 1], [
                (w[:, :kw], F32), (w[:, kw:2 * kw], F32), (w[:, 2 * kw:2 * kw + D], BF16),
                (w[:, 2 * kw + D:o0], F32), (w[:, o0:o0 + GLA_GATE_RANK], BF16), (w[:, o0 + GLA_GATE_RANK:], BF16)])
            mix = _gla_mix(seq(q), seq(k), seq(v), seq(g), seq(z), gla_gate_w2[j], gla_gate_b[j], gla_norm_g[j])
            w_out = gla_w_out[j]
        x3 = _attn_out(seq(x2), mix, seq(xq), mem_k, mem_v, w_out)
        x2 = _ffn(x3.reshape(T, D), norm_g[i, 2], ffn_w_up[i, 1], ffn_w_down[i, 1],
                  final_g=final_norm_g if i == depth - 1 else None)
    return x2.reshape(B, S, D)
```

```python
import functools
import math

import jax
import jax.numpy as jnp
from jax import lax
from jax.experimental import pallas as pl
from jax.experimental.pallas import tpu as pltpu

F32 = jnp.float32
BF16 = jnp.bfloat16

D_MODEL = 1024
D_FF = 2816
X_HEADS = 4
X_HEAD_DIM = 128
X_WIDTH = X_HEADS * X_HEAD_DIM
POOL_WINDOWS = (2, 4, 8, 16)
POOL_GROUP_DIM = D_MODEL // len(POOL_WINDOWS)
POOL_HALO = 16
DIFF_HEAD_DIM = 64
DIFF_HEADS = 8
DIFF_HEAD_BLOCK = 4
MLSTM_HEADS = 4
MLSTM_HEAD_DIM = 256
MLSTM_CONV = 4
CONV_HALO = 8
GLA_HEADS = 4
GLA_KEY_DIM = 128
GLA_VALUE_DIM = 256
GLA_GATE_RANK = 16
GLA_TAU = 16.0
NORM_EPS = 1e-6
SUBLN_EPS = 1e-5
LANES = 128

VMEM_LIMIT = 56 * 1024 * 1024
TOKEN_TILE = 512
FF_CHUNK = 1408
ATTN_TILE = 256
SCAN_CHUNK = 128


def _params(*sem):
    return pltpu.CompilerParams(dimension_semantics=sem, vmem_limit_bytes=VMEM_LIMIT)


def _resident(shape):
    nd = len(shape)
    return pl.BlockSpec(shape, lambda *_: (0,) * nd, pipeline_mode=pl.Buffered(1))


def _rms(x, g, eps):
    return x * lax.rsqrt(jnp.mean(x * x, axis=-1, keepdims=True) + eps) * g


def _sigmoid(x):
    return 1.0 / (1.0 + jnp.exp(-x))


def _log_sigmoid(x):
    return jnp.minimum(x, 0.0) - jnp.log(1.0 + jnp.exp(-jnp.abs(x)))


def _dot(a, b):
    return jnp.dot(a, b, preferred_element_type=F32)


def _dot_nt(a, b):
    return lax.dot_general(a, b, (((1,), (1,)), ((), ())), preferred_element_type=F32)


def _split3(x):
    hi = x.astype(BF16)
    r1 = x - hi.astype(F32)
    mid = r1.astype(BF16)
    lo = (r1 - mid.astype(F32)).astype(BF16)
    return hi, mid, lo


def _tri(n, upper):
    r = lax.broadcasted_iota(jnp.int32, (n, n), 0)
    c = lax.broadcasted_iota(jnp.int32, (n, n), 1)
    keep = (r <= c) if upper else (c <= r)
    return jnp.where(keep, 1.0, 0.0).astype(BF16)


def _cumsum_rows(x):
    tri = _tri(x.shape[0], upper=False)
    hi, mid, lo = _split3(x)
    return _dot(tri, hi) + _dot(tri, mid) + _dot(tri, lo)


def _cumsum_lanes(x):
    tri = _tri(x.shape[1], upper=True)
    hi, mid, lo = _split3(x)
    return _dot(hi, tri) + _dot(mid, tri) + _dot(lo, tri)


def _ffn_half_step(x, g_ref, wg_ref, wu_ref, wd_ref):
    h = _rms(x, g_ref[...], NORM_EPS).astype(BF16)
    acc = jnp.zeros_like(x)
    for c in range(D_FF // FF_CHUNK):
        sl = slice(c * FF_CHUNK, (c + 1) * FF_CHUNK)
        gate = _dot(h, wg_ref[:, sl])
        up = _dot(h, wu_ref[:, sl])
        act = (gate * _sigmoid(gate) * up).astype(BF16)
        acc = acc + _dot(act, wd_ref[sl, :])
    return x + 0.5 * acc


def _ffn_weights(g, w_up, w_down):
    args = [g.reshape(1, D_MODEL), w_up[:, :D_FF].astype(BF16), w_up[:, D_FF:].astype(BF16), w_down.astype(BF16)]
    specs = [_resident((1, D_MODEL)), _resident((D_MODEL, D_FF)), _resident((D_MODEL, D_FF)),
             _resident((D_FF, D_MODEL))]
    return args, specs


def _ffn_kernel(x_ref, g_ref, wg_ref, wu_ref, wd_ref, o_ref):
    o_ref[...] = _ffn_half_step(x_ref[...], g_ref, wg_ref, wu_ref, wd_ref)


def _ffn(x, g, w_up, w_down):
    T = x.shape[0]
    w_args, w_specs = _ffn_weights(g, w_up, w_down)
    tok = pl.BlockSpec((TOKEN_TILE, D_MODEL), lambda i: (i, 0))
    return pl.pallas_call(
        _ffn_kernel,
        grid=(T // TOKEN_TILE,),
        in_specs=[tok] + w_specs,
        out_specs=tok,
        out_shape=jax.ShapeDtypeStruct((T, D_MODEL), F32),
        compiler_params=_params("parallel"),
        name="ffn",
    )(x, *w_args)


def _norm_proj_kernel(x_ref, g_ref, *refs, n_out):
    h = _rms(x_ref[...], g_ref[...], NORM_EPS).astype(BF16)
    for w_ref, o_ref in zip(refs[:n_out], refs[n_out:]):
        o_ref[...] = _dot(h, w_ref[...]).astype(o_ref.dtype)


def _norm_proj(x, g, pieces):
    T = x.shape[0]
    ws, out_specs, out_shapes = [], [], []
    for w, dt in pieces:
        n = w.shape[1]
        n_pad = -(-n // LANES) * LANES
        if n_pad != n:
            w = jnp.pad(w, ((0, 0), (0, n_pad - n)))
        ws.append(w.astype(BF16))
        out_specs.append(pl.BlockSpec((TOKEN_TILE, n_pad), lambda i: (i, 0)))
        out_shapes.append(jax.ShapeDtypeStruct((T, n_pad), dt))
    in_specs = [pl.BlockSpec((TOKEN_TILE, D_MODEL), lambda i: (i, 0)), _resident((1, D_MODEL))]
    in_specs += [_resident(w.shape) for w in ws]
    return pl.pallas_call(
        functools.partial(_norm_proj_kernel, n_out=len(ws)),
        grid=(T // TOKEN_TILE,),
        in_specs=in_specs,
        out_specs=out_specs,
        out_shape=out_shapes,
        compiler_params=_params("parallel"),
        name="norm_proj",
    )(x, g.reshape(1, D_MODEL), *ws)


def _attn_out_kernel(x_ref, mix_ref, xq_ref, mk_ref, mv_ref, w1_ref, w2_ref, g_ref, wg_ref, wu_ref, wd_ref,
                     *rest, final):
    o_ref = rest[-1]
    xq = xq_ref[...]
    mk = mk_ref[...]
    mv = mv_ref[...]
    outs = []
    for h in range(X_HEADS):
        sl = slice(h * X_HEAD_DIM, (h + 1) * X_HEAD_DIM)
        s = _dot_nt(xq[:, sl], mk[:, sl]) * (X_HEAD_DIM ** -0.5)
        e = jnp.exp(s - jnp.max(s, axis=-1, keepdims=True))
        p = e * (1.0 / jnp.sum(e, axis=-1, keepdims=True))
        outs.append(_dot(p.astype(BF16), mv[:, sl]).astype(BF16))
    xo = jnp.concatenate(outs, axis=-1)
    x = x_ref[...] + _dot(mix_ref[...], w1_ref[...]) + _dot(xo, w2_ref[...])
    y = _ffn_half_step(x, g_ref, wg_ref, wu_ref, wd_ref)
    if final:
        y = _rms(y, rest[0][...], NORM_EPS)
    o_ref[...] = y


def _attn_out_ffn(x, mix, xq, mem_k, mem_v, w_out, g, w_up, w_down, final_g=None):
    B, S, _ = x.shape
    M = mem_k.shape[1]
    w1 = w_out[:D_MODEL].astype(BF16)
    w2 = w_out[D_MODEL:].astype(BF16)
    w_args, w_specs = _ffn_weights(g, w_up, w_down)
    if final_g is not None:
        w_args.append(final_g.reshape(1, D_MODEL))
        w_specs.append(_resident((1, D_MODEL)))
    tok = lambda n: pl.BlockSpec((None, TOKEN_TILE, n), lambda b, i: (b, i, 0))
    mem = pl.BlockSpec((None, M, X_WIDTH), lambda b, i: (b, 0, 0))
    return pl.pallas_call(
        functools.partial(_attn_out_kernel, final=final_g is not None),
        grid=(B, S // TOKEN_TILE),
        in_specs=[tok(D_MODEL), tok(D_MODEL), tok(X_WIDTH), mem, mem, _resident(w1.shape), _resident(w2.shape)]
        + w_specs,
        out_specs=tok(D_MODEL),
        out_shape=jax.ShapeDtypeStruct((B, S, D_MODEL), F32),
        compiler_params=_params("parallel", "parallel"),
        name="attn_out_ffn",
    )(x, mix, xq, mem_k, mem_v, w1, w2, *w_args)


def _pool_kernel(u_ref, wg_ref, sc_ref, o_ref, ext_ref, *, ts):
    j = pl.program_id(1)

    @pl.when(j == 0)
    def _():
        ext_ref[0:POOL_HALO, :] = jnp.zeros((POOL_HALO, D_MODEL), F32)

    ext_ref[POOL_HALO:POOL_HALO + ts, :] = u_ref[...]
    pos = j * ts + lax.broadcasted_iota(jnp.int32, (ts, 1), 0)
    for g, w in enumerate(POOL_WINDOWS):
        sl = slice(g * POOL_GROUP_DIM, (g + 1) * POOL_GROUP_DIM)
        u = u_ref[:, sl]
        acc = u
        for k in range(1, w):
            acc = acc + ext_ref[POOL_HALO - k:POOL_HALO - k + ts, sl]
        inv = 1.0 / jnp.minimum(pos + 1, w).astype(F32)
        pooled = (acc * inv - u).astype(BF16)
        o_ref[:, sl] = (_dot(pooled, wg_ref[g]) * sc_ref[:, sl]).astype(o_ref.dtype)
    ext_ref[0:POOL_HALO, :] = ext_ref[ts:ts + POOL_HALO, :]


def _pool_mix(u, w_group, scale):
    B, S, _ = u.shape
    ts = TOKEN_TILE
    tok = pl.BlockSpec((None, ts, D_MODEL), lambda b, j: (b, j, 0))
    return pl.pallas_call(
        functools.partial(_pool_kernel, ts=ts),
        grid=(B, S // ts),
        in_specs=[tok, _resident(w_group.shape), _resident((1, D_MODEL))],
        out_specs=tok,
        out_shape=jax.ShapeDtypeStruct((B, S, D_MODEL), BF16),
        scratch_shapes=[pltpu.VMEM((ts + POOL_HALO, D_MODEL), F32)],
        compiler_params=_params("parallel", "arbitrary"),
        name="pool_mix",
    )(u, w_group.astype(BF16), scale.reshape(1, D_MODEL))


def _diff_features(lane, c, f_a, f_b, f_c, f_d):
    f0 = DIFF_HEAD_DIM * (1 - c)
    return jnp.where(lane == f0, f_a, jnp.where(lane == f0 + 1, f_b, jnp.where(lane == f0 + 2, f_c, jnp.where(
        lane == f0 + 3, f_d, 0.0))))


def _diff_kernel(slopes_ref, lam_ref, q_ref, k_ref, vt_ref, ng_ref, o_ref,
                 ka_ref, acc_ref, m_ref, l_ref, *, t, n_tiles, lam_init):
    hg = pl.program_id(1)
    qi = pl.program_id(2)
    hd = 2 * DIFF_HEAD_DIM
    lane = lax.broadcasted_iota(jnp.int32, (t, hd), 1)
    rowf = lax.broadcasted_iota(jnp.int32, (t, hd), 0).astype(F32)
    own = (lane < DIFF_HEAD_DIM, lane >= DIFF_HEAD_DIM)
    slopes = [slopes_ref[hg * DIFF_HEAD_BLOCK + hb] for hb in range(DIFF_HEAD_BLOCK)]
    chains = [(hb, c) for hb in range(DIFF_HEAD_BLOCK) for c in range(2)]

    @pl.when(qi == 0)
    def _():
        def build(j, carry):
            start = pl.multiple_of(j * t, t)
            for hb, c in chains:
                k = k_ref[pl.ds(start, t), hb * hd:(hb + 1) * hd]
                feat = _diff_features(lane, c, rowf * slopes[hb], (j * t).astype(F32) * slopes[hb], 1.0, 1.0)
                ka_ref[hb, c, pl.ds(start, t), :] = jnp.where(own[c], k, feat.astype(BF16))
            return carry

        lax.fori_loop(0, n_tiles, build, 0)

    qa = {}
    for hb, c in chains:
        q = q_ref[:, hb * hd:(hb + 1) * hd] * (DIFF_HEAD_DIM ** -0.5)
        feat = _diff_features(lane, c, 1.0, 1.0, -(rowf * slopes[hb]), -((qi * t).astype(F32) * slopes[hb]))
        qa[hb, c] = jnp.where(own[c], q, feat.astype(BF16))
    acc_ref[...] = jnp.zeros_like(acc_ref)
    l_ref[...] = jnp.zeros_like(l_ref)
    m_ref[...] = jnp.full(m_ref.shape, -jnp.inf, F32)

    def tile(kj, masked):
        start = pl.multiple_of(kj * t, t)
        scores = {}
        for hb, c in chains:
            s = _dot_nt(ka_ref[hb, c, pl.ds(start, t), :], qa[hb, c])
            if masked:
                key = lax.broadcasted_iota(jnp.int32, (t, t), 0)
                qry = lax.broadcasted_iota(jnp.int32, (t, t), 1)
                s = jnp.where(key <= qry, s, -jnp.inf)
            scores[hb, c] = s
        probs = {}
        for hb, c in chains:
            s = scores[hb, c]
            m_prev = m_ref[hb, c]
            m_new = jnp.maximum(m_prev, jnp.max(s, axis=0, keepdims=True))
            alpha = jnp.exp(m_prev - m_new)
            p = jnp.exp(s - m_new)
            l_ref[hb, c] = alpha * l_ref[hb, c] + jnp.sum(p, axis=0, keepdims=True)
            m_ref[hb, c] = m_new
            probs[hb, c] = (alpha, p.astype(BF16))
        for hb, c in chains:
            alpha, p = probs[hb, c]
            acc_ref[hb, c] = alpha * acc_ref[hb, c] + _dot(vt_ref[hb, kj], p)

    def body(kj, carry):
        tile(kj, masked=False)
        return carry

    lax.fori_loop(0, qi, body, 0)
    tile(qi, masked=True)

    lp = lam_ref[...]
    lam = (jnp.exp(jnp.sum(lp[0:1] * lp[1:2], axis=-1, keepdims=True))
           - jnp.exp(jnp.sum(lp[2:3] * lp[3:4], axis=-1, keepdims=True)) + lam_init)
    for hb in range(DIFF_HEAD_BLOCK):
        o_t = (acc_ref[hb, 0] * (1.0 / l_ref[hb, 0])
               - lam * (acc_ref[hb, 1] * (1.0 / l_ref[hb, 1])))
        o_ref[:, hb * hd:(hb + 1) * hd] = (_rms(o_t.T, ng_ref[...], SUBLN_EPS) * (1.0 - lam_init)).astype(o_ref.dtype)


def _diff_mix(q, k, v, lam_p, norm_g, layer_idx):
    B, S, _ = q.shape
    t = ATTN_TILE
    n_tiles = S // t
    hd = 2 * DIFF_HEAD_DIM
    hb = DIFF_HEAD_BLOCK
    lam_init = 0.8 - 0.6 * math.exp(-0.3 * layer_idx)
    slopes = jnp.asarray([2.0 ** (-8.0 * (h + 1) / DIFF_HEADS) for h in range(DIFF_HEADS)], dtype=F32)
    v_t = v.reshape(B, n_tiles, t, DIFF_HEADS, hd).transpose(0, 3, 1, 4, 2)
    qspec = pl.BlockSpec((None, t, hb * hd), lambda b, h, i: (b, i, h))
    return pl.pallas_call(
        functools.partial(_diff_kernel, t=t, n_tiles=n_tiles, lam_init=lam_init),
        grid=(B, DIFF_HEADS // hb, n_tiles),
        in_specs=[pl.BlockSpec(memory_space=pltpu.SMEM), _resident(lam_p.shape), qspec,
                  pl.BlockSpec((None, S, hb * hd), lambda b, h, i: (b, 0, h)),
                  pl.BlockSpec((None, hb, n_tiles, hd, t), lambda b, h, i: (b, h, 0, 0, 0)),
                  _resident((1, hd))],
        out_specs=qspec,
        out_shape=jax.ShapeDtypeStruct((B, S, D_MODEL), BF16),
        scratch_shapes=[pltpu.VMEM((hb, 2, S, hd), BF16), pltpu.VMEM((hb, 2, hd, t), F32),
                        pltpu.VMEM((hb, 2, 1, t), F32), pltpu.VMEM((hb, 2, 1, t), F32)],
        compiler_params=_params("parallel", "parallel", "arbitrary"),
        name="diff_attn",
    )(slopes, lam_p, q, k, v_t, norm_g.reshape(1, hd))


def _mlstm_kernel(qk_ref, v_ref, og_ref, gc_ref, gr_ref, cw_ref, gbc_ref, gbr_ref, ng_ref, o_ref,
                  ext_ref, c_ref, n_ref, m_ref, *, L):
    c = pl.program_id(1)
    W = 2 * D_MODEL

    @pl.when(c == 0)
    def _():
        ext_ref[0:CONV_HALO, :] = jnp.zeros((CONV_HALO, W), F32)
        c_ref[...] = jnp.zeros_like(c_ref)
        n_ref[...] = jnp.zeros_like(n_ref)
        m_ref[...] = jnp.zeros_like(m_ref)

    ext_ref[CONV_HALO:CONV_HALO + L, :] = qk_ref[...]
    conv = cw_ref[MLSTM_CONV - 1:MLSTM_CONV, :] * qk_ref[...]
    for j in range(1, MLSTM_CONV):
        conv = conv + (cw_ref[MLSTM_CONV - 1 - j:MLSTM_CONV - j, :]
                       * ext_ref[CONV_HALO - j:CONV_HALO - j + L, :])
    ext_ref[0:CONV_HALO, :] = ext_ref[L:L + CONV_HALO, :]
    qk = conv * _sigmoid(conv)

    gc = gc_ref[...] + gbc_ref[...]
    gr = gr_ref[...] + gbr_ref[...]
    b_c = _cumsum_rows(_log_sigmoid(gc))
    b_r = _cumsum_lanes(_log_sigmoid(gr))
    row = lax.broadcasted_iota(jnp.int32, (L, L), 0)
    col = lax.broadcasted_iota(jnp.int32, (L, L), 1)
    causal = col <= row

    for hd in range(MLSTM_HEADS):
        sl = slice(hd * MLSTM_HEAD_DIM, (hd + 1) * MLSTM_HEAD_DIM)
        q32 = qk[:, sl] * (MLSTM_HEAD_DIM ** -0.5)
        k32 = qk[:, D_MODEL + hd * MLSTM_HEAD_DIM:D_MODEL + (hd + 1) * MLSTM_HEAD_DIM]
        vh = v_ref[:, sl]
        fi = MLSTM_HEADS + hd
        bcol, icol = b_c[:, fi:fi + 1], gc[:, hd:hd + 1]
        brow, irow = b_r[fi:fi + 1, :], gr[hd:hd + 1, :]
        b_last = brow[:, L - 1:L]
        m = m_ref[hd:hd + 1, 0:1]
        cmat = c_ref[hd]
        nrow = n_ref[hd]

        dmat = jnp.where(causal, bcol - brow + irow, -jnp.inf)
        inter = bcol + m
        m_t = jnp.maximum(inter, jnp.max(dmat, axis=-1, keepdims=True))
        dec = jnp.exp(inter - m_t)
        qb = q32.astype(BF16)
        k_t = k32.T
        sqk = _dot(qb, k_t.astype(BF16)) * jnp.exp(dmat - m_t)
        num = dec * _dot(qb, cmat.astype(BF16)) + _dot(sqk.astype(BF16), vh)
        den = dec * jnp.sum(q32 * nrow, axis=-1, keepdims=True) + jnp.sum(sqk, axis=-1, keepdims=True)
        hc = num * (1.0 / jnp.maximum(jnp.abs(den), jnp.exp(-m_t)))

        gs_r = b_last - brow + irow
        gs_c = b_last - bcol + icol
        m_new = jnp.maximum(b_last + m, jnp.max(gs_r, axis=-1, keepdims=True))
        carry_dec = jnp.exp(b_last + m - m_new)
        c_ref[hd] = carry_dec * cmat + _dot((k_t * jnp.exp(gs_r - m_new)).astype(BF16), vh)
        n_ref[hd] = carry_dec * nrow + jnp.sum(k32 * jnp.exp(gs_c - m_new), axis=0, keepdims=True)
        m_ref[hd:hd + 1, :] = jnp.broadcast_to(m_new, (1, LANES))

        o_ref[:, sl] = (_rms(hc, ng_ref[:, sl], NORM_EPS) * _sigmoid(og_ref[:, sl])).astype(o_ref.dtype)


def _mlstm_mix(qk, v, og, gates, conv_w, gate_b, norm_g):
    B, S, _ = qk.shape
    L = SCAN_CHUNK
    H = MLSTM_HEADS
    gates_r = jnp.transpose(gates[..., :2 * H], (0, 2, 1))
    gb = gate_b.reshape(2 * H)
    gb_c = jnp.pad(gb, (0, LANES - 2 * H)).reshape(1, LANES)
    gb_r = gb.reshape(2 * H, 1)
    tok = lambda n: pl.BlockSpec((None, L, n), lambda b, c: (b, c, 0))
    return pl.pallas_call(
        functools.partial(_mlstm_kernel, L=L),
        grid=(B, S // L),
        in_specs=[tok(2 * D_MODEL), tok(D_MODEL), tok(D_MODEL), tok(LANES),
                  pl.BlockSpec((None, 2 * H, L), lambda b, c: (b, 0, c)),
                  _resident(conv_w.shape), _resident((1, LANES)), _resident((2 * H, 1)), _resident((1, D_MODEL))],
        out_specs=tok(D_MODEL),
        out_shape=jax.ShapeDtypeStruct((B, S, D_MODEL), BF16),
        scratch_shapes=[pltpu.VMEM((L + CONV_HALO, 2 * D_MODEL), F32),
                        pltpu.VMEM((H, MLSTM_HEAD_DIM, MLSTM_HEAD_DIM), F32),
                        pltpu.VMEM((H, 1, MLSTM_HEAD_DIM), F32),
                        pltpu.VMEM((8, LANES), F32)],
        compiler_params=_params("parallel", "arbitrary"),
        name="mlstm",
    )(qk, v, og, gates, gates_r, conv_w, gb_c, gb_r, norm_g.reshape(1, D_MODEL))


def _gla_kernel(q_ref, k_ref, v_ref, g_ref, z_ref, w2_ref, gb_ref, ng_ref, o_ref, st_ref, *, L):
    c = pl.program_id(1)

    @pl.when(c == 0)
    def _():
        st_ref[...] = jnp.zeros_like(st_ref)

    log_a = _log_sigmoid(_dot(z_ref[...], w2_ref[...]) + gb_ref[...]) * (1.0 / GLA_TAU)
    b_all = _cumsum_rows(log_a)
    row = lax.broadcasted_iota(jnp.int32, (L, L), 0)
    col = lax.broadcasted_iota(jnp.int32, (L, L), 1)
    causal = col <= row

    for hd in range(GLA_HEADS):
        ks = slice(hd * GLA_KEY_DIM, (hd + 1) * GLA_KEY_DIM)
        vs = slice(hd * GLA_VALUE_DIM, (hd + 1) * GLA_VALUE_DIM)
        b = b_all[:, ks]
        vh = v_ref[:, vs]
        st = st_ref[hd]
        qe = (q_ref[:, ks] * (GLA_KEY_DIM ** -0.5) * jnp.exp(b)).astype(BF16)
        b_t = b.T
        k_t = k_ref[:, ks].T
        b_last = b_t[:, L - 1:L]
        amat = jnp.where(causal, _dot(qe, (k_t * jnp.exp(-b_t)).astype(BF16)), 0.0)
        o = _dot(qe, st.astype(BF16)) + _dot(amat.astype(BF16), vh)
        st_ref[hd] = jnp.exp(b_last) * st + _dot((k_t * jnp.exp(b_last - b_t)).astype(BF16), vh)
        gate = g_ref[:, vs]
        o_ref[:, vs] = (_rms(o, ng_ref[:, vs], NORM_EPS) * (gate * _sigmoid(gate))).astype(o_ref.dtype)


def _gla_mix(q, k, v, g, z, gate_w2, gate_b, norm_g):
    B, S, _ = q.shape
    L = SCAN_CHUNK
    kw = GLA_HEADS * GLA_KEY_DIM
    w2 = jnp.pad(gate_w2, ((0, LANES - GLA_GATE_RANK), (0, 0))).astype(BF16)
    tok = lambda n: pl.BlockSpec((None, L, n), lambda b, c: (b, c, 0))
    return pl.pallas_call(
        functools.partial(_gla_kernel, L=L),
        grid=(B, S // L),
        in_specs=[tok(kw), tok(kw), tok(D_MODEL), tok(D_MODEL), tok(LANES),
                  _resident(w2.shape), _resident((1, kw)), _resident((1, D_MODEL))],
        out_specs=tok(D_MODEL),
        out_shape=jax.ShapeDtypeStruct((B, S, D_MODEL), BF16),
        scratch_shapes=[pltpu.VMEM((GLA_HEADS, GLA_KEY_DIM, GLA_VALUE_DIM), F32)],
        compiler_params=_params("parallel", "arbitrary"),
        name="gla",
    )(q, k, v, g, z, w2, gate_b.reshape(1, kw), norm_g.reshape(1, D_MODEL))


def kernel(x, mem, norm_g, ffn_w_up, ffn_w_down, mem_norm_g, mem_w_kv, pool_w_in, pool_w_group, pool_scale, pool_w_out, diff_w_in, diff_lambda, diff_norm_g, diff_w_out, mlstm_w_in, mlstm_conv_w, mlstm_gate_b, mlstm_norm_g, mlstm_w_out, gla_w_in, gla_gate_w2, gla_gate_b, gla_norm_g, gla_w_out, final_norm_g):
    B, S, D = x.shape
    M = mem.shape[1]
    T = B * S
    depth = norm_g.shape[0]
    n_mixers = 4
    mem2 = mem.reshape(B * M, D)
    x2 = x.reshape(T, D)

    def seq(a):
        return a.reshape(B, S, a.shape[-1])

    for i in range(depth):
        kind, j = i % n_mixers, i // n_mixers
        mem_k, mem_v = _norm_proj(mem2, mem_norm_g, [(mem_w_kv[i][:, :X_WIDTH], BF16), (mem_w_kv[i][:, X_WIDTH:], BF16)])
        mem_k = mem_k.reshape(B, M, X_WIDTH)
        mem_v = mem_v.reshape(B, M, X_WIDTH)
        x2 = _ffn(x2, norm_g[i, 0], ffn_w_up[i, 0], ffn_w_down[i, 0])
        if kind == 0:
            w = pool_w_in[j]
            u, xq = _norm_proj(x2, norm_g[i, 1], [(w[:, :D], F32), (w[:, D:], BF16)])
            mix = _pool_mix(seq(u), pool_w_group[j], pool_scale[j])
            w_out = pool_w_out[j]
        elif kind == 1:
            w = diff_w_in[j]
            q, k, v, xq = _norm_proj(x2, norm_g[i, 1], [(w[:, :D], BF16), (w[:, D:2 * D], BF16),
                                                        (w[:, 2 * D:3 * D], BF16), (w[:, 3 * D:], BF16)])
            mix = _diff_mix(seq(q), seq(k), seq(v), diff_lambda[j], diff_norm_g[j], i)
            w_out = diff_w_out[j]
        elif kind == 2:
            w = mlstm_w_in[j]
            ng = 2 * MLSTM_HEADS
            qk, v, og, gates, xq = _norm_proj(x2, norm_g[i, 1], [
                (w[:, :2 * D], F32), (w[:, 2 * D:3 * D], BF16), (w[:, 3 * D:4 * D], F32),
                (w[:, 4 * D:4 * D + ng], F32), (w[:, 4 * D + ng:], BF16)])
            mix = _mlstm_mix(seq(qk), seq(v), seq(og), seq(gates), mlstm_conv_w[j], mlstm_gate_b[j], mlstm_norm_g[j])
            w_out = mlstm_w_out[j]
        else:
            w = gla_w_in[j]
            kw = GLA_HEADS * GLA_KEY_DIM
            o0 = 2 * kw + 2 * D
            q, k, v, g, z, xq = _norm_proj(x2, norm_g[i, 1], [
                (w[:, :kw], F32), (w[:, kw:2 * kw], F32), (w[:, 2 * kw:2 * kw + D], BF16),
                (w[:, 2 * kw + D:o0], F32), (w[:, o0:o0 + GLA_GATE_RANK], BF16), (w[:, o0 + GLA_GATE_RANK:], BF16)])
            mix = _gla_mix(seq(q), seq(k), seq(v), seq(g), seq(z), gla_gate_w2[j], gla_gate_b[j], gla_norm_g[j])
            w_out = gla_w_out[j]
        x3 = _attn_out_ffn(seq(x2), mix, seq(xq), mem_k, mem_v, w_out, norm_g[i, 2], ffn_w_up[i, 1],
                           ffn_w_down[i, 1], final_g=final_norm_g if i == depth - 1 else None)
        x2 = x3.reshape(T, D)
    return x2.reshape(B, S, D)
```

```python
import functools
import math

import jax
import jax.numpy as jnp
from jax import lax
from jax.experimental import pallas as pl
from jax.experimental.pallas import tpu as pltpu

F32 = jnp.float32
BF16 = jnp.bfloat16

D_MODEL = 1024
D_FF = 2816
X_HEADS = 4
X_HEAD_DIM = 128
X_WIDTH = X_HEADS * X_HEAD_DIM
POOL_WINDOWS = (2, 4, 8, 16)
POOL_GROUP_DIM = D_MODEL // len(POOL_WINDOWS)
POOL_HALO = 16
DIFF_HEAD_DIM = 64
DIFF_HEADS = 8
DIFF_HEAD_BLOCK = 8
MLSTM_HEADS = 4
MLSTM_HEAD_DIM = 256
MLSTM_CONV = 4
CONV_HALO = 8
GLA_HEADS = 4
GLA_KEY_DIM = 128
GLA_VALUE_DIM = 256
GLA_GATE_RANK = 16
GLA_TAU = 16.0
GLA_SUBCHUNK = 16
LOG2E = math.log2(math.e)
NORM_EPS = 1e-6
SUBLN_EPS = 1e-5
LANES = 128

VMEM_LIMIT = 56 * 1024 * 1024
TOKEN_TILE = 512
FF_CHUNKS = (0, 1536, D_FF)
ATTN_TILE = 256
SCAN_CHUNK = 128


def _params(*sem):
    return pltpu.CompilerParams(dimension_semantics=sem, vmem_limit_bytes=VMEM_LIMIT)


def _resident(shape):
    nd = len(shape)
    return pl.BlockSpec(shape, lambda *_: (0,) * nd, pipeline_mode=pl.Buffered(1))


def _rms(x, g, eps):
    return x * lax.rsqrt(jnp.mean(x * x, axis=-1, keepdims=True) + eps) * g


def _sigmoid(x):
    return 1.0 / (1.0 + jnp.exp(-x))


def _log_sigmoid(x):
    return jnp.minimum(x, 0.0) - jnp.log(1.0 + jnp.exp(-jnp.abs(x)))


def _dot(a, b):
    return jnp.dot(a, b, preferred_element_type=F32)


def _dot_nt(a, b):
    return lax.dot_general(a, b, (((1,), (1,)), ((), ())), preferred_element_type=F32)


def _split3(x):
    hi = x.astype(BF16)
    r1 = x - hi.astype(F32)
    mid = r1.astype(BF16)
    lo = (r1 - mid.astype(F32)).astype(BF16)
    return hi, mid, lo


def _tri(n, upper):
    r = lax.broadcasted_iota(jnp.int32, (n, n), 0)
    c = lax.broadcasted_iota(jnp.int32, (n, n), 1)
    keep = (r <= c) if upper else (c <= r)
    return jnp.where(keep, 1.0, 0.0).astype(BF16)


def _cumsum_rows(x):
    tri = _tri(x.shape[0], upper=False)
    hi, mid, lo = _split3(x)
    return _dot(tri, hi) + _dot(tri, mid) + _dot(tri, lo)


def _cumsum_lanes(x):
    tri = _tri(x.shape[1], upper=True)
    hi, mid, lo = _split3(x)
    return _dot(hi, tri) + _dot(mid, tri) + _dot(lo, tri)


def _ffn_half_step(x, g_ref, wg_ref, wu_ref, wd_ref):
    h = _rms(x, g_ref[...], NORM_EPS).astype(BF16)
    acc = jnp.zeros_like(x)
    for lo, hi in zip(FF_CHUNKS[:-1], FF_CHUNKS[1:]):
        sl = slice(lo, hi)
        gate = _dot(h, wg_ref[:, sl])
        up = _dot(h, wu_ref[:, sl])
        act = (gate * _sigmoid(gate) * up).astype(BF16)
        acc = acc + _dot(act, wd_ref[sl, :])
    return x + 0.5 * acc


def _ffn_weights(g, w_up, w_down):
    args = [g.reshape(1, D_MODEL), w_up[:, :D_FF].astype(BF16), w_up[:, D_FF:].astype(BF16), w_down.astype(BF16)]
    specs = [_resident((1, D_MODEL)), _resident((D_MODEL, D_FF)), _resident((D_MODEL, D_FF)),
             _resident((D_FF, D_MODEL))]
    return args, specs


def _ffn_kernel(x_ref, g_ref, wg_ref, wu_ref, wd_ref, o_ref):
    o_ref[...] = _ffn_half_step(x_ref[...], g_ref, wg_ref, wu_ref, wd_ref)


def _ffn(x, g, w_up, w_down):
    T = x.shape[0]
    w_args, w_specs = _ffn_weights(g, w_up, w_down)
    tok = pl.BlockSpec((TOKEN_TILE, D_MODEL), lambda i: (i, 0))
    return pl.pallas_call(
        _ffn_kernel,
        grid=(T // TOKEN_TILE,),
        in_specs=[tok] + w_specs,
        out_specs=tok,
        out_shape=jax.ShapeDtypeStruct((T, D_MODEL), F32),
        compiler_params=_params("parallel"),
        name="ffn",
    )(x, *w_args)


def _norm_proj_kernel(x_ref, g_ref, *refs, n_out):
    h = _rms(x_ref[...], g_ref[...], NORM_EPS).astype(BF16)
    for w_ref, o_ref in zip(refs[:n_out], refs[n_out:]):
        o_ref[...] = _dot(h, w_ref[...]).astype(o_ref.dtype)


def _norm_proj(x, g, pieces):
    T = x.shape[0]
    ws, out_specs, out_shapes = [], [], []
    for w, dt in pieces:
        n = w.shape[1]
        n_pad = -(-n // LANES) * LANES
        if n_pad != n:
            w = jnp.pad(w, ((0, 0), (0, n_pad - n)))
        ws.append(w.astype(BF16))
        out_specs.append(pl.BlockSpec((TOKEN_TILE, n_pad), lambda i: (i, 0)))
        out_shapes.append(jax.ShapeDtypeStruct((T, n_pad), dt))
    in_specs = [pl.BlockSpec((TOKEN_TILE, D_MODEL), lambda i: (i, 0)), _resident((1, D_MODEL))]
    in_specs += [_resident(w.shape) for w in ws]
    return pl.pallas_call(
        functools.partial(_norm_proj_kernel, n_out=len(ws)),
        grid=(T // TOKEN_TILE,),
        in_specs=in_specs,
        out_specs=out_specs,
        out_shape=out_shapes,
        compiler_params=_params("parallel"),
        name="norm_proj",
    )(x, g.reshape(1, D_MODEL), *ws)


def _attn_out_kernel(x_ref, mix_ref, xq_ref, mk_ref, mv_ref, w1_ref, w2_ref, g_ref, wg_ref, wu_ref, wd_ref,
                     *rest, final):
    o_ref = rest[-1]
    xq = xq_ref[...]
    mk = mk_ref[...]
    mv = mv_ref[...]
    outs = []
    for h in range(X_HEADS):
        sl = slice(h * X_HEAD_DIM, (h + 1) * X_HEAD_DIM)
        s = _dot_nt(xq[:, sl], mk[:, sl]) * (X_HEAD_DIM ** -0.5)
        e = jnp.exp(s - jnp.max(s, axis=-1, keepdims=True))
        p = e * (1.0 / jnp.sum(e, axis=-1, keepdims=True))
        outs.append(_dot(p.astype(BF16), mv[:, sl]).astype(BF16))
    xo = jnp.concatenate(outs, axis=-1)
    x = x_ref[...] + _dot(mix_ref[...], w1_ref[...]) + _dot(xo, w2_ref[...])
    y = _ffn_half_step(x, g_ref, wg_ref, wu_ref, wd_ref)
    if final:
        y = _rms(y, rest[0][...], NORM_EPS)
    o_ref[...] = y


def _attn_out_ffn(x, mix, xq, mem_k, mem_v, w_out, g, w_up, w_down, final_g=None):
    B, S, _ = x.shape
    M = mem_k.shape[1]
    w1 = w_out[:D_MODEL].astype(BF16)
    w2 = w_out[D_MODEL:].astype(BF16)
    w_args, w_specs = _ffn_weights(g, w_up, w_down)
    if final_g is not None:
        w_args.append(final_g.reshape(1, D_MODEL))
        w_specs.append(_resident((1, D_MODEL)))
    tok = lambda n: pl.BlockSpec((None, TOKEN_TILE, n), lambda b, i: (b, i, 0))
    mem = pl.BlockSpec((None, M, X_WIDTH), lambda b, i: (b, 0, 0))
    return pl.pallas_call(
        functools.partial(_attn_out_kernel, final=final_g is not None),
        grid=(B, S // TOKEN_TILE),
        in_specs=[tok(D_MODEL), tok(D_MODEL), tok(X_WIDTH), mem, mem, _resident(w1.shape), _resident(w2.shape)]
        + w_specs,
        out_specs=tok(D_MODEL),
        out_shape=jax.ShapeDtypeStruct((B, S, D_MODEL), F32),
        compiler_params=_params("parallel", "parallel"),
        name="attn_out_ffn",
    )(x, mix, xq, mem_k, mem_v, w1, w2, *w_args)


def _pool_kernel(u_ref, wg_ref, sc_ref, o_ref, ext_ref, *, ts):
    j = pl.program_id(1)

    @pl.when(j == 0)
    def _():
        ext_ref[0:POOL_HALO, :] = jnp.zeros((POOL_HALO, D_MODEL), F32)

    ext_ref[POOL_HALO:POOL_HALO + ts, :] = u_ref[...]
    pos = j * ts + lax.broadcasted_iota(jnp.int32, (ts, 1), 0)
    for g, w in enumerate(POOL_WINDOWS):
        sl = slice(g * POOL_GROUP_DIM, (g + 1) * POOL_GROUP_DIM)
        u = u_ref[:, sl]
        acc = u
        for k in range(1, w):
            acc = acc + ext_ref[POOL_HALO - k:POOL_HALO - k + ts, sl]
        inv = 1.0 / jnp.minimum(pos + 1, w).astype(F32)
        pooled = (acc * inv - u).astype(BF16)
        o_ref[:, sl] = (_dot(pooled, wg_ref[g]) * sc_ref[:, sl]).astype(o_ref.dtype)
    ext_ref[0:POOL_HALO, :] = ext_ref[ts:ts + POOL_HALO, :]


def _pool_mix(u, w_group, scale):
    B, S, _ = u.shape
    ts = TOKEN_TILE
    tok = pl.BlockSpec((None, ts, D_MODEL), lambda b, j: (b, j, 0))
    return pl.pallas_call(
        functools.partial(_pool_kernel, ts=ts),
        grid=(B, S // ts),
        in_specs=[tok, _resident(w_group.shape), _resident((1, D_MODEL))],
        out_specs=tok,
        out_shape=jax.ShapeDtypeStruct((B, S, D_MODEL), BF16),
        scratch_shapes=[pltpu.VMEM((ts + POOL_HALO, D_MODEL), F32)],
        compiler_params=_params("parallel", "arbitrary"),
        name="pool_mix",
    )(u, w_group.astype(BF16), scale.reshape(1, D_MODEL))


DIFF_SLOPE_TERMS = 3


def _lane_features(lane, groups):
    out = jnp.zeros(lane.shape, F32)
    for g, v in enumerate(groups):
        for r in range(DIFF_SLOPE_TERMS):
            out = jnp.where(lane == DIFF_SLOPE_TERMS * g + r, v[r] if isinstance(v, list) else v, out)
    return out


def _diff_kernel(slopes_ref, lam_ref, q_ref, k_ref, vt_ref, ng_ref, o_ref,
                 ka_ref, acc_ref, m_ref, l_ref, *, t, n_tiles, lam_init):
    hg = pl.program_id(1)
    qi = pl.program_id(2)
    hd = 2 * DIFF_HEAD_DIM
    lane = lax.broadcasted_iota(jnp.int32, (t, hd), 1)
    rowf = lax.broadcasted_iota(jnp.int32, (t, hd), 0).astype(F32)
    own = (lane < DIFF_HEAD_DIM, lane >= DIFF_HEAD_DIM)
    heads = range(DIFF_HEAD_BLOCK)
    c_terms = [[slopes_ref[hg * DIFF_HEAD_BLOCK + hb, r] for r in range(DIFF_SLOPE_TERMS)] for hb in heads]
    chains = [(hb, c) for hb in heads for c in range(2)]

    @pl.when(qi == 0)
    def _():
        def build(j, carry):
            start = pl.multiple_of(j * t, t)
            tile_idx = jnp.asarray(j, F32)
            for hb in heads:
                neg = [-cr for cr in c_terms[hb]]
                feat = _lane_features(lane, [rowf, tile_idx, neg, [cr * t for cr in neg]])
                ka_ref[hb, pl.ds(start, t), 0:hd] = k_ref[pl.ds(start, t), hb * hd:(hb + 1) * hd]
                ka_ref[hb, pl.ds(start, t), hd:2 * hd] = feat.astype(BF16)
            return carry

        lax.fori_loop(0, n_tiles, build, 0)

    qa = {}
    for hb in heads:
        q = q_ref[:, hb * hd:(hb + 1) * hd]
        feat = _lane_features(lane, [c_terms[hb], [cr * t for cr in c_terms[hb]], rowf, jnp.asarray(qi, F32)])
        for c in range(2):
            qa[hb, c] = jnp.concatenate([jnp.where(own[c], q, jnp.zeros_like(q)), feat.astype(BF16)], axis=1)
    acc_ref[...] = jnp.zeros_like(acc_ref)
    l_ref[...] = jnp.zeros_like(l_ref)
    m_ref[...] = jnp.full(m_ref.shape, -jnp.inf, F32)

    def tile(kj, masked):
        start = pl.multiple_of(kj * t, t)
        scores = {}
        for hb, c in chains:
            s = _dot_nt(ka_ref[hb, pl.ds(start, t), :], qa[hb, c])
            if masked:
                key = lax.broadcasted_iota(jnp.int32, (t, t), 0)
                qry = lax.broadcasted_iota(jnp.int32, (t, t), 1)
                s = jnp.where(key <= qry, s, -jnp.inf)
            scores[hb, c] = s
        probs = {}
        for hb, c in chains:
            s = scores[hb, c]
            m_prev = m_ref[hb, c]
            m_new = jnp.maximum(m_prev, jnp.max(s, axis=0, keepdims=True))
            alpha = jnp.exp2(m_prev - m_new)
            p = jnp.exp2(s - m_new)
            l_ref[hb, c] = alpha * l_ref[hb, c] + jnp.sum(p, axis=0, keepdims=True)
            m_ref[hb, c] = m_new
            probs[hb, c] = (alpha, p.astype(BF16))
        for hb, c in chains:
            alpha, p = probs[hb, c]
            acc_ref[hb, c] = alpha * acc_ref[hb, c] + _dot(vt_ref[hb, kj], p)

    def body(kj, carry):
        tile(kj, masked=False)
        return carry

    lax.fori_loop(0, qi, body, 0)
    tile(qi, masked=True)

    lp = lam_ref[...]
    lam = (jnp.exp(jnp.sum(lp[0:1] * lp[1:2], axis=-1, keepdims=True))
           - jnp.exp(jnp.sum(lp[2:3] * lp[3:4], axis=-1, keepdims=True)) + lam_init)
    for hb in range(DIFF_HEAD_BLOCK):
        o_t = (acc_ref[hb, 0] * (1.0 / l_ref[hb, 0])
               - lam * (acc_ref[hb, 1] * (1.0 / l_ref[hb, 1])))
        o_ref[:, hb * hd:(hb + 1) * hd] = (_rms(o_t.T, ng_ref[...], SUBLN_EPS) * (1.0 - lam_init)).astype(o_ref.dtype)


def _diff_mix(q, k, v, lam_p, norm_g, layer_idx):
    B, S, _ = q.shape
    t = ATTN_TILE
    n_tiles = S // t
    hd = 2 * DIFF_HEAD_DIM
    hb = DIFF_HEAD_BLOCK
    lam_init = 0.8 - 0.6 * math.exp(-0.3 * layer_idx)
    slopes = jnp.asarray([2.0 ** (-8.0 * (h + 1) / DIFF_HEADS) for h in range(DIFF_HEADS)], dtype=F32)
    rest = slopes * math.log2(math.e)
    terms = []
    for _ in range(DIFF_SLOPE_TERMS):
        terms.append(rest.astype(BF16).astype(F32))
        rest = rest - terms[-1]
    slopes = jnp.stack(terms, axis=1)
    v_t =v.reshape(B, n_tiles, t, DIFF_HEADS, hd).transpose(0, 3, 1, 4, 2)
    qspec = pl.BlockSpec((None, t, hb * hd), lambda b, h, i: (b, i, h))
    return pl.pallas_call(
        functools.partial(_diff_kernel, t=t, n_tiles=n_tiles, lam_init=lam_init),
        grid=(B, DIFF_HEADS // hb, n_tiles),
        in_specs=[pl.BlockSpec(memory_space=pltpu.SMEM), _resident(lam_p.shape), qspec,
                  pl.BlockSpec((None, S, hb * hd), lambda b, h, i: (b, 0, h), pipeline_mode=pl.Buffered(1)),
                  pl.BlockSpec((None, hb, n_tiles, hd, t), lambda b, h, i: (b, h, 0, 0, 0),
                               pipeline_mode=pl.Buffered(1)),
                  _resident((1, hd))],
        out_specs=qspec,
        out_shape=jax.ShapeDtypeStruct((B, S, D_MODEL), BF16),
        scratch_shapes=[pltpu.VMEM((hb, S, 2 * hd), BF16), pltpu.VMEM((hb, 2, hd, t), F32),
                        pltpu.VMEM((hb, 2, 1, t), F32), pltpu.VMEM((hb, 2, 1, t), F32)],
        compiler_params=_params("parallel", "parallel", "arbitrary"),
        name="diff_attn",
    )(slopes, lam_p, q, k, v_t, norm_g.reshape(1, hd))


def _mlstm_kernel(qk_ref, v_ref, og_ref, gc_ref, gr_ref, cw_ref, gbc_ref, gbr_ref, ng_ref, o_ref,
                  ext_ref, c_ref, n_ref, m_ref, *, L):
    c = pl.program_id(1)
    W = 2 * D_MODEL

    @pl.when(c == 0)
    def _():
        ext_ref[0:CONV_HALO, :] = jnp.zeros((CONV_HALO, W), F32)
        c_ref[...] = jnp.zeros_like(c_ref)
        n_ref[...] = jnp.zeros_like(n_ref)
        m_ref[...] = jnp.zeros_like(m_ref)

    ext_ref[CONV_HALO:CONV_HALO + L, :] = qk_ref[...]
    conv = cw_ref[MLSTM_CONV - 1:MLSTM_CONV, :] * qk_ref[...]
    for j in range(1, MLSTM_CONV):
        conv = conv + (cw_ref[MLSTM_CONV - 1 - j:MLSTM_CONV - j, :]
                       * ext_ref[CONV_HALO - j:CONV_HALO - j + L, :])
    ext_ref[0:CONV_HALO, :] = ext_ref[L:L + CONV_HALO, :]
    qk = conv * _sigmoid(conv)

    gc = gc_ref[...] + gbc_ref[...]
    gr = gr_ref[...] + gbr_ref[...]
    b_c = _cumsum_rows(_log_sigmoid(gc))
    b_r = _cumsum_lanes(_log_sigmoid(gr))
    row = lax.broadcasted_iota(jnp.int32, (L, L), 0)
    col = lax.broadcasted_iota(jnp.int32, (L, L), 1)
    causal = col <= row

    for hd in range(MLSTM_HEADS):
        sl = slice(hd * MLSTM_HEAD_DIM, (hd + 1) * MLSTM_HEAD_DIM)
        q32 = qk[:, sl] * (MLSTM_HEAD_DIM ** -0.5)
        k32 = qk[:, D_MODEL + hd * MLSTM_HEAD_DIM:D_MODEL + (hd + 1) * MLSTM_HEAD_DIM]
        vh = v_ref[:, sl]
        fi = MLSTM_HEADS + hd
        bcol, icol = b_c[:, fi:fi + 1], gc[:, hd:hd + 1]
        brow, irow = b_r[fi:fi + 1, :], gr[hd:hd + 1, :]
        b_last = brow[:, L - 1:L]
        m = m_ref[hd:hd + 1, 0:1]
        cmat = c_ref[hd]
        nrow = n_ref[hd]

        dmat = jnp.where(causal, bcol - brow + irow, -jnp.inf)
        inter = bcol + m
        m_t = jnp.maximum(inter, jnp.max(dmat, axis=-1, keepdims=True))
        dec = jnp.exp(inter - m_t)
        qb = q32.astype(BF16)
        k_t = k32.T
        sqk = _dot(qb, k_t.astype(BF16)) * jnp.exp(dmat - m_t)
        num = dec * _dot(qb, cmat.astype(BF16)) + _dot(sqk.astype(BF16), vh)
        den = dec * jnp.sum(q32 * nrow, axis=-1, keepdims=True) + jnp.sum(sqk, axis=-1, keepdims=True)
        hc = num * (1.0 / jnp.maximum(jnp.abs(den), jnp.exp(-m_t)))

        gs_r = b_last - brow + irow
        gs_c = b_last - bcol + icol
        m_new = jnp.maximum(b_last + m, jnp.max(gs_r, axis=-1, keepdims=True))
        carry_dec = jnp.exp(b_last + m - m_new)
        c_ref[hd] = carry_dec * cmat + _dot((k_t * jnp.exp(gs_r - m_new)).astype(BF16), vh)
        n_ref[hd] = carry_dec * nrow + jnp.sum(k32 * jnp.exp(gs_c - m_new), axis=0, keepdims=True)
        m_ref[hd:hd + 1, :] = jnp.broadcast_to(m_new, (1, LANES))

        o_ref[:, sl] = (_rms(hc, ng_ref[:, sl], NORM_EPS) * _sigmoid(og_ref[:, sl])).astype(o_ref.dtype)


def _mlstm_mix(qk, v, og, gates, conv_w, gate_b, norm_g):
    B, S, _ = qk.shape
    L = SCAN_CHUNK
    H = MLSTM_HEADS
    gates_r = jnp.transpose(gates[..., :2 * H], (0, 2, 1))
    gb = gate_b.reshape(2 * H)
    gb_c = jnp.pad(gb, (0, LANES - 2 * H)).reshape(1, LANES)
    gb_r = gb.reshape(2 * H, 1)
    tok = lambda n: pl.BlockSpec((None, L, n), lambda b, c: (b, c, 0))
    return pl.pallas_call(
        functools.partial(_mlstm_kernel, L=L),
        grid=(B, S // L),
        in_specs=[tok(2 * D_MODEL), tok(D_MODEL), tok(D_MODEL), tok(LANES),
                  pl.BlockSpec((None, 2 * H, L), lambda b, c: (b, 0, c)),
                  _resident(conv_w.shape), _resident((1, LANES)), _resident((2 * H, 1)), _resident((1, D_MODEL))],
        out_specs=tok(D_MODEL),
        out_shape=jax.ShapeDtypeStruct((B, S, D_MODEL), BF16),
        scratch_shapes=[pltpu.VMEM((L + CONV_HALO, 2 * D_MODEL), F32),
                        pltpu.VMEM((H, MLSTM_HEAD_DIM, MLSTM_HEAD_DIM), F32),
                        pltpu.VMEM((H, 1, MLSTM_HEAD_DIM), F32),
                        pltpu.VMEM((8, LANES), F32)],
        compiler_params=_params("parallel", "arbitrary"),
        name="mlstm",
    )(qk, v, og, gates, gates_r, conv_w, gb_c, gb_r, norm_g.reshape(1, D_MODEL))


def _gla_kernel(q_ref, k_ref, v_ref, g_ref, z_ref, w2_ref, gb_ref, ng_ref, o_ref, st_ref, *, L):
    c = pl.program_id(1)

    @pl.when(c == 0)
    def _():
        st_ref[...] = jnp.zeros_like(st_ref)

    log_a = _log_sigmoid(_dot(z_ref[...], w2_ref[...]) + gb_ref[...]) * (1.0 / GLA_TAU)
    b_all = _cumsum_rows(log_a) * LOG2E
    sub = GLA_SUBCHUNK
    row = lax.broadcasted_iota(jnp.int32, (sub, L), 0)
    col = lax.broadcasted_iota(jnp.int32, (sub, L), 1)
    ones = jnp.ones((GLA_KEY_DIM, LANES), BF16)

    for hd in range(GLA_HEADS):
        ks = slice(hd * GLA_KEY_DIM, (hd + 1) * GLA_KEY_DIM)
        vs = slice(hd * GLA_VALUE_DIM, (hd + 1) * GLA_VALUE_DIM)
        b = b_all[:, ks]
        vh = v_ref[:, vs]
        st = st_ref[hd]
        q = q_ref[:, ks] * (GLA_KEY_DIM ** -0.5)
        k = k_ref[:, ks]
        b_t = b.T
        k_t = k.T
        b_last = b_t[:, L - 1:L]
        blocks = []
        for i in range(L // sub):
            r0 = i * sub
            q_i, k_i, b_i = q[r0:r0 + sub], k[r0:r0 + sub], b[r0:r0 + sub]
            terms = []
            for s in range(sub):
                decay = jnp.exp2(jnp.minimum(b_i - b_i[s:s + 1], 0.0))
                terms.append(q_i * k_i[s:s + 1] * decay)
            sums = _dot(jnp.concatenate(terms, axis=0).astype(BF16), ones)
            a_i = jnp.zeros((sub, L), F32)
            for s in range(sub):
                a_i = jnp.where(col == r0 + s, sums[s * sub:(s + 1) * sub], a_i)
            a_i = jnp.where(col <= r0 + row, a_i, 0.0)
            if i > 0:
                ref_row, ref_col = b[r0 - 1:r0], b_t[:, r0 - 1:r0]
                q_s = (q_i * jnp.exp2(b_i - ref_row)).astype(BF16)
                k_s = (k_t * jnp.exp2(jnp.minimum(ref_col - b_t, 0.0))).astype(BF16)
                a_i = a_i + jnp.where(col < r0, _dot(q_s, k_s), 0.0)
            blocks.append(a_i)
        amat = jnp.concatenate(blocks, axis=0)
        o = _dot((q * jnp.exp2(b)).astype(BF16), st.astype(BF16)) + _dot(amat.astype(BF16), vh)
        st_ref[hd] = jnp.exp2(b_last) * st + _dot((k_t * jnp.exp2(b_last - b_t)).astype(BF16), vh)
        gate = g_ref[:, vs]
        o_ref[:, vs] = (_rms(o, ng_ref[:, vs], NORM_EPS) * (gate * _sigmoid(gate))).astype(o_ref.dtype)


def _gla_mix(q, k, v, g, z, gate_w2, gate_b, norm_g):
    B, S, _ = q.shape
    L = SCAN_CHUNK
    kw = GLA_HEADS * GLA_KEY_DIM
    w2 = jnp.pad(gate_w2, ((0, LANES - GLA_GATE_RANK), (0, 0))).astype(BF16)
    tok = lambda n: pl.BlockSpec((None, L, n), lambda b, c: (b, c, 0))
    return pl.pallas_call(
        functools.partial(_gla_kernel, L=L),
        grid=(B, S // L),
        in_specs=[tok(kw), tok(kw), tok(D_MODEL), tok(D_MODEL), tok(LANES),
                  _resident(w2.shape), _resident((1, kw)), _resident((1, D_MODEL))],
        out_specs=tok(D_MODEL),
        out_shape=jax.ShapeDtypeStruct((B, S, D_MODEL), BF16),
        scratch_shapes=[pltpu.VMEM((GLA_HEADS, GLA_KEY_DIM, GLA_VALUE_DIM), F32)],
        compiler_params=_params("parallel", "arbitrary"),
        name="gla",
    )(q, k, v, g, z, w2, gate_b.reshape(1, kw), norm_g.reshape(1, D_MODEL))


def kernel(x, mem, norm_g, ffn_w_up, ffn_w_down, mem_norm_g, mem_w_kv, pool_w_in, pool_w_group, pool_scale, pool_w_out, diff_w_in, diff_lambda, diff_norm_g, diff_w_out, mlstm_w_in, mlstm_conv_w, mlstm_gate_b, mlstm_norm_g, mlstm_w_out, gla_w_in, gla_gate_w2, gla_gate_b, gla_norm_g, gla_w_out, final_norm_g):
    B, S, D = x.shape
    M = mem.shape[1]
    T = B * S
    depth = norm_g.shape[0]
    n_mixers = 4
    mem2 = mem.reshape(B * M, D)
    x2 = x.reshape(T, D)

    def seq(a):
        return a.reshape(B, S, a.shape[-1])

    for i in range(depth):
        kind, j = i % n_mixers, i // n_mixers
        mem_k, mem_v = _norm_proj(mem2, mem_norm_g, [(mem_w_kv[i][:, :X_WIDTH], BF16), (mem_w_kv[i][:, X_WIDTH:], BF16)])
        mem_k = mem_k.reshape(B, M, X_WIDTH)
        mem_v = mem_v.reshape(B, M, X_WIDTH)
        x2 = _ffn(x2, norm_g[i, 0], ffn_w_up[i, 0], ffn_w_down[i, 0])
        if kind == 0:
            w = pool_w_in[j]
            u, xq = _norm_proj(x2, norm_g[i, 1], [(w[:, :D], F32), (w[:, D:], BF16)])
            mix = _pool_mix(seq(u), pool_w_group[j], pool_scale[j])
            w_out = pool_w_out[j]
        elif kind == 1:
            w = diff_w_in[j]
            q_scale = DIFF_HEAD_DIM ** -0.5 * math.log2(math.e)
            q, k, v, xq = _norm_proj(x2, norm_g[i, 1], [(w[:, :D] * q_scale, BF16), (w[:, D:2 * D], BF16),
                                                        (w[:, 2 * D:3 * D], BF16), (w[:, 3 * D:], BF16)])
            mix = _diff_mix(seq(q), seq(k), seq(v), diff_lambda[j], diff_norm_g[j], i)
            w_out = diff_w_out[j]
        elif kind == 2:
            w = mlstm_w_in[j]
            ng = 2 * MLSTM_HEADS
            qk, v, og, gates, xq = _norm_proj(x2, norm_g[i, 1], [
                (w[:, :2 * D], F32), (w[:, 2 * D:3 * D], BF16), (w[:, 3 * D:4 * D], F32),
                (w[:, 4 * D:4 * D + ng], F32), (w[:, 4 * D + ng:], BF16)])
            mix = _mlstm_mix(seq(qk), seq(v), seq(og), seq(gates), mlstm_conv_w[j], mlstm_gate_b[j], mlstm_norm_g[j])
            w_out = mlstm_w_out[j]
        else:
            w = gla_w_in[j]
            kw = GLA_HEADS * GLA_KEY_DIM
            o0 = 2 * kw + 2 * D
            q, k, v, g, z, xq = _norm_proj(x2, norm_g[i, 1], [
                (w[:, :kw], F32), (w[:, kw:2 * kw], F32), (w[:, 2 * kw:2 * kw + D], BF16),
                (w[:, 2 * kw + D:o0], F32), (w[:, o0:o0 + GLA_GATE_RANK], BF16), (w[:, o0 + GLA_GATE_RANK:], BF16)])
            mix = _gla_mix(seq(q), seq(k), seq(v), seq(g), seq(z), gla_gate_w2[j], gla_gate_b[j], gla_norm_g[j])
            w_out = gla_w_out[j]
        x3 = _attn_out_ffn(seq(x2), mix, seq(xq), mem_k, mem_v, w_out, norm_g[i, 2], ffn_w_up[i, 1],
                           ffn_w_down[i, 1], final_g=final_norm_g if i == depth - 1 else None)
        x2 = x3.reshape(T, D)
    return x2.reshape(B, S, D)
```

```python
import functools
import math

import jax
import jax.numpy as jnp
from jax import lax
from jax.experimental import pallas as pl
from jax.experimental.pallas import tpu as pltpu

F32 = jnp.float32
BF16 = jnp.bfloat16

D_MODEL = 1024
D_FF = 2816
X_HEADS = 4
X_HEAD_DIM = 128
X_WIDTH = X_HEADS * X_HEAD_DIM
POOL_WINDOWS = (2, 4, 8, 16)
POOL_GROUP_DIM = D_MODEL // len(POOL_WINDOWS)
POOL_HALO = 16
DIFF_HEAD_DIM = 64
DIFF_HEADS = 8
DIFF_HEAD_BLOCK = 8
MLSTM_HEADS = 4
MLSTM_HEAD_DIM = 256
MLSTM_CONV = 4
CONV_HALO = 8
GLA_HEADS = 4
GLA_KEY_DIM = 128
GLA_VALUE_DIM = 256
GLA_GATE_RANK = 16
GLA_TAU = 16.0
GLA_SUBCHUNK = 16
LOG2E = math.log2(math.e)
NORM_EPS = 1e-6
SUBLN_EPS = 1e-5
LANES = 128
SUBLANES = 8

VMEM_LIMIT = 56 * 1024 * 1024
TOKEN_TILE = 512
FF_CHUNKS = (0, 1536, D_FF)
ATTN_TILE = 256
SCAN_CHUNK = 128


def _params(*sem):
    return pltpu.CompilerParams(dimension_semantics=sem, vmem_limit_bytes=VMEM_LIMIT)


def _resident(shape):
    nd = len(shape)
    return pl.BlockSpec(shape, lambda *_: (0,) * nd, pipeline_mode=pl.Buffered(1))


def _rms(x, g, eps):
    return x * lax.rsqrt(jnp.mean(x * x, axis=-1, keepdims=True) + eps) * g


def _sigmoid(x):
    return 1.0 / (1.0 + jnp.exp(-x))


def _log_sigmoid(x):
    return jnp.minimum(x, 0.0) - jnp.log(1.0 + jnp.exp(-jnp.abs(x)))


def _dot(a, b):
    return jnp.dot(a, b, preferred_element_type=F32)


def _dot_nt(a, b):
    return lax.dot_general(a, b, (((1,), (1,)), ((), ())), preferred_element_type=F32)


def _split3(x):
    hi = x.astype(BF16)
    r1 = x - hi.astype(F32)
    mid = r1.astype(BF16)
    lo = (r1 - mid.astype(F32)).astype(BF16)
    return hi, mid, lo


def _tri(n, upper):
    r = lax.broadcasted_iota(jnp.int32, (n, n), 0)
    c = lax.broadcasted_iota(jnp.int32, (n, n), 1)
    keep = (r <= c) if upper else (c <= r)
    return jnp.where(keep, 1.0, 0.0).astype(BF16)


def _cumsum_rows(x):
    tri = _tri(x.shape[0], upper=False)
    hi, mid, lo = _split3(x)
    return _dot(tri, hi) + _dot(tri, mid) + _dot(tri, lo)


def _cumsum_lanes(x):
    tri = _tri(x.shape[1], upper=True)
    hi, mid, lo = _split3(x)
    return _dot(hi, tri) + _dot(mid, tri) + _dot(lo, tri)


def _ffn_half_step(x, g_ref, wg_ref, wu_ref, wd_ref):
    h = _rms(x, g_ref[...], NORM_EPS).astype(BF16)
    acc = jnp.zeros_like(x)
    for lo, hi in zip(FF_CHUNKS[:-1], FF_CHUNKS[1:]):
        sl = slice(lo, hi)
        gate = _dot(h, wg_ref[:, sl])
        up = _dot(h, wu_ref[:, sl])
        act = (gate * _sigmoid(gate) * up).astype(BF16)
        acc = acc + _dot(act, wd_ref[sl, :])
    return x + 0.5 * acc


def _stacked(shape, lead, tail=None):
    tail = tail or (0,) * len(shape)
    return pl.BlockSpec((None,) * len(lead) + shape, lambda *_: lead + tail, pipeline_mode=pl.Buffered(1))


def _ffn_weights(norm_g, w_up, w_down, layer, norm_slot, ffn_slot):
    lead = (layer, ffn_slot)
    args = [norm_g, w_up, w_up, w_down]
    specs = [_stacked((1, D_MODEL), (layer, norm_slot)), _stacked((D_MODEL, D_FF), lead, (0, 0)),
             _stacked((D_MODEL, D_FF), lead, (0, 1)), _stacked((D_FF, D_MODEL), lead)]
    return args, specs


def _ffn_kernel(x_ref, g_ref, wg_ref, wu_ref, wd_ref, o_ref):
    o_ref[...] = _ffn_half_step(x_ref[...], g_ref, wg_ref, wu_ref, wd_ref)


def _ffn(x, ffn_weights):
    T = x.shape[0]
    w_args, w_specs = ffn_weights
    tok = pl.BlockSpec((TOKEN_TILE, D_MODEL), lambda i: (i, 0))
    return pl.pallas_call(
        _ffn_kernel,
        grid=(T // TOKEN_TILE,),
        in_specs=[tok] + w_specs,
        out_specs=tok,
        out_shape=jax.ShapeDtypeStruct((T, D_MODEL), F32),
        compiler_params=_params("parallel"),
        name="ffn",
    )(x, *w_args)


def _norm_proj_kernel(x_ref, g_ref, *refs, n_out):
    h = _rms(x_ref[...], g_ref[...], NORM_EPS).astype(BF16)
    for w_ref, o_ref in zip(refs[:n_out], refs[n_out:]):
        o_ref[...] = _dot(h, w_ref[...]).astype(o_ref.dtype)


def _norm_proj(x, g, pieces):
    T = x.shape[0]
    ws, out_specs, out_shapes = [], [], []
    for w, dt in pieces:
        n = w.shape[1]
        n_pad = -(-n // LANES) * LANES
        if n_pad != n:
            w = jnp.pad(w, ((0, 0), (0, n_pad - n)))
        ws.append(w.astype(BF16))
        out_specs.append(pl.BlockSpec((TOKEN_TILE, n_pad), lambda i: (i, 0)))
        out_shapes.append(jax.ShapeDtypeStruct((T, n_pad), dt))
    in_specs = [pl.BlockSpec((TOKEN_TILE, D_MODEL), lambda i: (i, 0)), _resident((1, D_MODEL))]
    in_specs += [_resident(w.shape) for w in ws]
    return pl.pallas_call(
        functools.partial(_norm_proj_kernel, n_out=len(ws)),
        grid=(T // TOKEN_TILE,),
        in_specs=in_specs,
        out_specs=out_specs,
        out_shape=out_shapes,
        compiler_params=_params("parallel"),
        name="norm_proj",
    )(x, g.reshape(1, D_MODEL), *ws)


def _attn_out_kernel(x_ref, mix_ref, xq_ref, mk_ref, mv_ref, w1_ref, w2_ref, g_ref, wg_ref, wu_ref, wd_ref,
                     *rest, final):
    o_ref = rest[-1]
    xq = xq_ref[...]
    mk = mk_ref[...]
    mv = mv_ref[...]
    outs = []
    for h in range(X_HEADS):
        sl = slice(h * X_HEAD_DIM, (h + 1) * X_HEAD_DIM)
        s = _dot_nt(xq[:, sl], mk[:, sl]) * (X_HEAD_DIM ** -0.5)
        e = jnp.exp(s - jnp.max(s, axis=-1, keepdims=True))
        p = e * (1.0 / jnp.sum(e, axis=-1, keepdims=True))
        outs.append(_dot(p.astype(BF16), mv[:, sl]).astype(BF16))
    xo = jnp.concatenate(outs, axis=-1)
    x = x_ref[...] + _dot(mix_ref[...], w1_ref[...]) + _dot(xo, w2_ref[...])
    y = _ffn_half_step(x, g_ref, wg_ref, wu_ref, wd_ref)
    if final:
        y = _rms(y, rest[0][...], NORM_EPS)
    o_ref[...] = y


def _attn_out_ffn(x, mix, xq, mem_k, mem_v, w_out, ffn_weights, final_g=None):
    B, S, _ = x.shape
    M = mem_k.shape[1]
    w1 = w_out[:D_MODEL].astype(BF16)
    w2 = w_out[D_MODEL:].astype(BF16)
    w_args, w_specs = ffn_weights
    if final_g is not None:
        w_args.append(final_g.reshape(1, D_MODEL))
        w_specs.append(_resident((1, D_MODEL)))
    tok = lambda n: pl.BlockSpec((None, TOKEN_TILE, n), lambda b, i: (b, i, 0))
    mem = pl.BlockSpec((None, M, X_WIDTH), lambda b, i: (b, 0, 0))
    return pl.pallas_call(
        functools.partial(_attn_out_kernel, final=final_g is not None),
        grid=(B, S // TOKEN_TILE),
        in_specs=[tok(D_MODEL), tok(D_MODEL), tok(X_WIDTH), mem, mem, _resident(w1.shape), _resident(w2.shape)]
        + w_specs,
        out_specs=tok(D_MODEL),
        out_shape=jax.ShapeDtypeStruct((B, S, D_MODEL), F32),
        compiler_params=_params("parallel", "parallel"),
        name="attn_out_ffn",
    )(x, mix, xq, mem_k, mem_v, w1, w2, *w_args)


def _pool_kernel(u_ref, wg_ref, sc_ref, o_ref, ext_ref, *, ts):
    j = pl.program_id(1)

    @pl.when(j == 0)
    def _():
        ext_ref[0:POOL_HALO, :] = jnp.zeros((POOL_HALO, D_MODEL), F32)

    ext_ref[POOL_HALO:POOL_HALO + ts, :] = u_ref[...]
    pos = j * ts + lax.broadcasted_iota(jnp.int32, (ts, 1), 0)
    for g, w in enumerate(POOL_WINDOWS):
        sl = slice(g * POOL_GROUP_DIM, (g + 1) * POOL_GROUP_DIM)
        u = u_ref[:, sl]
        acc = u
        for k in range(1, w):
            acc = acc + ext_ref[POOL_HALO - k:POOL_HALO - k + ts, sl]
        inv = 1.0 / jnp.minimum(pos + 1, w).astype(F32)
        pooled = (acc * inv - u).astype(BF16)
        o_ref[:, sl] = (_dot(pooled, wg_ref[g]) * sc_ref[:, sl]).astype(o_ref.dtype)
    ext_ref[0:POOL_HALO, :] = ext_ref[ts:ts + POOL_HALO, :]


def _pool_mix(u, w_group, scale):
    B, S, _ = u.shape
    ts = TOKEN_TILE
    tok = pl.BlockSpec((None, ts, D_MODEL), lambda b, j: (b, j, 0))
    return pl.pallas_call(
        functools.partial(_pool_kernel, ts=ts),
        grid=(B, S // ts),
        in_specs=[tok, _resident(w_group.shape), _resident((1, D_MODEL))],
        out_specs=tok,
        out_shape=jax.ShapeDtypeStruct((B, S, D_MODEL), BF16),
        scratch_shapes=[pltpu.VMEM((ts + POOL_HALO, D_MODEL), F32)],
        compiler_params=_params("parallel", "arbitrary"),
        name="pool_mix",
    )(u, w_group.astype(BF16), scale.reshape(1, D_MODEL))


DIFF_SLOPE_TERMS = 3
DIFF_AUX_ROWS = 16


def _lane_features(lane, groups):
    out = jnp.zeros(lane.shape, F32)
    for g, v in enumerate(groups):
        for r in range(DIFF_SLOPE_TERMS):
            out = jnp.where(lane == DIFF_SLOPE_TERMS * g + r, v[r] if isinstance(v, list) else v, out)
    return out


def _diff_kernel(slopes_ref, lam_ref, q_ref, k_ref, vt_ref, ng_ref, o_ref,
                 ka_ref, acc_ref, m_ref, *, t, n_tiles, lam_init):
    hg = pl.program_id(1)
    qi = pl.program_id(2)
    hd = 2 * DIFF_HEAD_DIM
    lane = lax.broadcasted_iota(jnp.int32, (t, hd), 1)
    rowf = lax.broadcasted_iota(jnp.int32, (t, hd), 0).astype(F32)
    own = (lane < DIFF_HEAD_DIM, lane >= DIFF_HEAD_DIM)
    heads = range(DIFF_HEAD_BLOCK)
    c_terms = [[slopes_ref[hg * DIFF_HEAD_BLOCK + hb, r] for r in range(DIFF_SLOPE_TERMS)] for hb in heads]
    chains = [(hb, c) for hb in heads for c in range(2)]

    @pl.when(qi == 0)
    def _():
        def build(j, carry):
            start = pl.multiple_of(j * t, t)
            tile_idx = jnp.asarray(j, F32)
            for hb in heads:
                neg = [-cr for cr in c_terms[hb]]
                feat = _lane_features(lane, [rowf, tile_idx, neg, [cr * t for cr in neg]])
                ka_ref[hb, pl.ds(start, t), 0:hd] = k_ref[pl.ds(start, t), hb * hd:(hb + 1) * hd]
                ka_ref[hb, pl.ds(start, t), hd:2 * hd] = feat.astype(BF16)
            return carry

        lax.fori_loop(0, n_tiles, build, 0)

    qa = {}
    for hb in heads:
        q = q_ref[:, hb * hd:(hb + 1) * hd]
        feat = _lane_features(lane, [c_terms[hb], [cr * t for cr in c_terms[hb]], rowf, jnp.asarray(qi, F32)])
        for c in range(2):
            qa[hb, c] = jnp.concatenate([jnp.where(own[c], q, jnp.zeros_like(q)), feat.astype(BF16)], axis=1)
    acc_ref[...] = jnp.zeros_like(acc_ref)
    m_ref[...] = jnp.full(m_ref.shape, -jnp.inf, F32)

    def tile(kj, masked):
        start = pl.multiple_of(kj * t, t)
        scores = {}
        for hb, c in chains:
            s = _dot_nt(ka_ref[hb, pl.ds(start, t), :], qa[hb, c])
            if masked:
                key = lax.broadcasted_iota(jnp.int32, (t, t), 0)
                qry = lax.broadcasted_iota(jnp.int32, (t, t), 1)
                s = jnp.where(key <= qry, s, -jnp.inf)
            scores[hb, c] = s
        probs = {}
        for hb, c in chains:
            s = scores[hb, c]
            m_prev = m_ref[hb, c]
            m_new = jnp.maximum(m_prev, jnp.max(s, axis=0, keepdims=True))
            alpha = jnp.exp2(m_prev - m_new)
            p = jnp.exp2(s - m_new)
            m_ref[hb, c] = m_new
            probs[hb, c] = (alpha, p.astype(BF16))
        for hb, c in chains:
            alpha, p = probs[hb, c]
            acc_ref[hb, c] = alpha * acc_ref[hb, c] + _dot(vt_ref[hb, kj], p)

    def body(kj, carry):
        tile(kj, masked=False)
        return carry

    lax.fori_loop(0, qi, body, 0)
    tile(qi, masked=True)

    lp = lam_ref[...]
    lam = (jnp.exp(jnp.sum(lp[0:1] * lp[1:2], axis=-1, keepdims=True))
           - jnp.exp(jnp.sum(lp[2:3] * lp[3:4], axis=-1, keepdims=True)) + lam_init)
    for hb in range(DIFF_HEAD_BLOCK):
        a0, a1 = acc_ref[hb, 0], acc_ref[hb, 1]
        o_t = (a0[:hd] * (1.0 / a0[hd:hd + 1]) - lam * (a1[:hd] * (1.0 / a1[hd:hd + 1])))
        o_ref[:, hb * hd:(hb + 1) * hd] = (_rms(o_t.T, ng_ref[...], SUBLN_EPS) * (1.0 - lam_init)).astype(o_ref.dtype)


def _diff_mix(q, k, v, lam_p, norm_g, layer_idx):
    B, S, _ = q.shape
    t = ATTN_TILE
    n_tiles = S // t
    hd = 2 * DIFF_HEAD_DIM
    hb = DIFF_HEAD_BLOCK
    lam_init = 0.8 - 0.6 * math.exp(-0.3 * layer_idx)
    slopes = jnp.asarray([2.0 ** (-8.0 * (h + 1) / DIFF_HEADS) for h in range(DIFF_HEADS)], dtype=F32)
    rest = slopes * math.log2(math.e)
    terms = []
    for _ in range(DIFF_SLOPE_TERMS):
        terms.append(rest.astype(BF16).astype(F32))
        rest = rest - terms[-1]
    slopes = jnp.stack(terms, axis=1)
    v_t = v.reshape(B, n_tiles, t, DIFF_HEADS, hd).transpose(0, 3, 1, 4, 2)
    ones_row = (jnp.arange(DIFF_AUX_ROWS) == 0).astype(BF16)[:, None]
    v_t = jnp.concatenate([v_t, jnp.broadcast_to(ones_row, v_t.shape[:3] + (DIFF_AUX_ROWS, t))], axis=3)
    qspec = pl.BlockSpec((None, t, hb * hd), lambda b, h, i: (b, i, h))
    return pl.pallas_call(
        functools.partial(_diff_kernel, t=t, n_tiles=n_tiles, lam_init=lam_init),
        grid=(B, DIFF_HEADS // hb, n_tiles),
        in_specs=[pl.BlockSpec(memory_space=pltpu.SMEM), _resident(lam_p.shape), qspec,
                  pl.BlockSpec((None, S, hb * hd), lambda b, h, i: (b, 0, h), pipeline_mode=pl.Buffered(1)),
                  pl.BlockSpec((None, hb, n_tiles, hd + DIFF_AUX_ROWS, t), lambda b, h, i: (b, h, 0, 0, 0),
                               pipeline_mode=pl.Buffered(1)),
                  _resident((1, hd))],
        out_specs=qspec,
        out_shape=jax.ShapeDtypeStruct((B, S, D_MODEL), BF16),
        scratch_shapes=[pltpu.VMEM((hb, S, 2 * hd), BF16), pltpu.VMEM((hb, 2, hd + DIFF_AUX_ROWS, t), F32),
                        pltpu.VMEM((hb, 2, 1, t), F32)],
        compiler_params=_params("parallel", "parallel", "arbitrary"),
        name="diff_attn",
    )(slopes, lam_p, q, k, v_t, norm_g.reshape(1, hd))


def _mlstm_kernel(qk_ref, v_ref, og_ref, gc_ref, gr_ref, cw_ref, gbc_ref, gbr_ref, ng_ref, o_ref,
                  ext_ref, c_ref, n_ref, m_ref, *, L):
    c = pl.program_id(1)
    W = 2 * D_MODEL

    @pl.when(c == 0)
    def _():
        ext_ref[0:CONV_HALO, :] = jnp.zeros((CONV_HALO, W), F32)
        c_ref[...] = jnp.zeros_like(c_ref)
        n_ref[...] = jnp.zeros_like(n_ref)
        m_ref[...] = jnp.zeros_like(m_ref)

    ext_ref[CONV_HALO:CONV_HALO + L, :] = qk_ref[...]
    conv = cw_ref[MLSTM_CONV - 1:MLSTM_CONV, :] * qk_ref[...]
    for j in range(1, MLSTM_CONV):
        conv = conv + (cw_ref[MLSTM_CONV - 1 - j:MLSTM_CONV - j, :]
                       * ext_ref[CONV_HALO - j:CONV_HALO - j + L, :])
    ext_ref[0:CONV_HALO, :] = ext_ref[L:L + CONV_HALO, :]
    qk = conv * _sigmoid(conv)

    gc = gc_ref[...] + gbc_ref[...]
    gr = gr_ref[...] + gbr_ref[...]
    b_c = _cumsum_rows(_log_sigmoid(gc))
    b_r = _cumsum_lanes(_log_sigmoid(gr))
    row = lax.broadcasted_iota(jnp.int32, (L, L), 0)
    col = lax.broadcasted_iota(jnp.int32, (L, L), 1)
    causal = col <= row

    for hd in range(MLSTM_HEADS):
        sl = slice(hd * MLSTM_HEAD_DIM, (hd + 1) * MLSTM_HEAD_DIM)
        q32 = qk[:, sl] * (MLSTM_HEAD_DIM ** -0.5)
        k32 = qk[:, D_MODEL + hd * MLSTM_HEAD_DIM:D_MODEL + (hd + 1) * MLSTM_HEAD_DIM]
        vh = v_ref[:, sl]
        fi = MLSTM_HEADS + hd
        bcol, icol = b_c[:, fi:fi + 1], gc[:, hd:hd + 1]
        brow, irow = b_r[fi:fi + 1, :], gr[hd:hd + 1, :]
        b_last = brow[:, L - 1:L]
        m = m_ref[hd:hd + 1, 0:1]
        cmat = c_ref[hd]
        nrow = n_ref[hd]

        dmat = jnp.where(causal, bcol - brow + irow, -jnp.inf)
        inter = bcol + m
        m_t = jnp.maximum(inter, jnp.max(dmat, axis=-1, keepdims=True))
        dec = jnp.exp(inter - m_t)
        qb = q32.astype(BF16)
        k_t = k32.T
        sqk = _dot(qb, k_t.astype(BF16)) * jnp.exp(dmat - m_t)
        num = dec * _dot(qb, cmat.astype(BF16)) + _dot(sqk.astype(BF16), vh)
        den = dec * jnp.sum(q32 * nrow, axis=-1, keepdims=True) + jnp.sum(sqk, axis=-1, keepdims=True)
        hc = num * (1.0 / jnp.maximum(jnp.abs(den), jnp.exp(-m_t)))

        gs_r = b_last - brow + irow
        gs_c = b_last - bcol + icol
        m_new = jnp.maximum(b_last + m, jnp.max(gs_r, axis=-1, keepdims=True))
        carry_dec = jnp.exp(b_last + m - m_new)
        c_ref[hd] = carry_dec * cmat + _dot((k_t * jnp.exp(gs_r - m_new)).astype(BF16), vh)
        n_ref[hd] = carry_dec * nrow + jnp.sum(k32 * jnp.exp(gs_c - m_new), axis=0, keepdims=True)
        m_ref[hd:hd + 1, :] = jnp.broadcast_to(m_new, (1, LANES))

        o_ref[:, sl] = (_rms(hc, ng_ref[:, sl], NORM_EPS) * _sigmoid(og_ref[:, sl])).astype(o_ref.dtype)


def _mlstm_mix(qk, v, og, gates, conv_w, gate_b, norm_g):
    B, S, _ = qk.shape
    L = SCAN_CHUNK
    H = MLSTM_HEADS
    gates_r = jnp.transpose(gates[..., :2 * H], (0, 2, 1))
    gb = gate_b.reshape(2 * H)
    gb_c = jnp.pad(gb, (0, LANES - 2 * H)).reshape(1, LANES)
    gb_r = gb.reshape(2 * H, 1)
    tok = lambda n: pl.BlockSpec((None, L, n), lambda b, c: (b, c, 0))
    return pl.pallas_call(
        functools.partial(_mlstm_kernel, L=L),
        grid=(B, S // L),
        in_specs=[tok(2 * D_MODEL), tok(D_MODEL), tok(D_MODEL), tok(LANES),
                  pl.BlockSpec((None, 2 * H, L), lambda b, c: (b, 0, c)),
                  _resident(conv_w.shape), _resident((1, LANES)), _resident((2 * H, 1)), _resident((1, D_MODEL))],
        out_specs=tok(D_MODEL),
        out_shape=jax.ShapeDtypeStruct((B, S, D_MODEL), BF16),
        scratch_shapes=[pltpu.VMEM((L + CONV_HALO, 2 * D_MODEL), F32),
                        pltpu.VMEM((H, MLSTM_HEAD_DIM, MLSTM_HEAD_DIM), F32),
                        pltpu.VMEM((H, 1, MLSTM_HEAD_DIM), F32),
                        pltpu.VMEM((8, LANES), F32)],
        compiler_params=_params("parallel", "arbitrary"),
        name="mlstm",
    )(qk, v, og, gates, gates_r, conv_w, gb_c, gb_r, norm_g.reshape(1, D_MODEL))


def _gla_kernel(q_ref, k_ref, v_ref, g_ref, z_ref, w2_ref, gb_ref, ng_ref, o_ref, st_ref, *, L):
    c = pl.program_id(1)

    @pl.when(c == 0)
    def _():
        st_ref[...] = jnp.zeros_like(st_ref)

    log_a = _log_sigmoid(_dot(z_ref[...], w2_ref[...]) + gb_ref[...]) * (1.0 / GLA_TAU)
    b_all = _cumsum_rows(log_a) * LOG2E
    sub = GLA_SUBCHUNK
    row = lax.broadcasted_iota(jnp.int32, (sub, L), 0)
    col = lax.broadcasted_iota(jnp.int32, (sub, L), 1)
    ones = jnp.ones((GLA_KEY_DIM, LANES), BF16)

    for hd in range(GLA_HEADS):
        ks = slice(hd * GLA_KEY_DIM, (hd + 1) * GLA_KEY_DIM)
        vs = slice(hd * GLA_VALUE_DIM, (hd + 1) * GLA_VALUE_DIM)
        b = b_all[:, ks]
        vh = v_ref[:, vs]
        st = st_ref[hd]
        q = q_ref[:, ks] * (GLA_KEY_DIM ** -0.5)
        k = k_ref[:, ks]
        b_t = b.T
        k_t = k.T
        b_last = b_t[:, L - 1:L]
        blocks = []
        for i in range(L // sub):
            r0 = i * sub
            q_i, k_i, b_i = q[r0:r0 + sub], k[r0:r0 + sub], b[r0:r0 + sub]
            terms, offs = [], []
            for s in range(sub):
                t0 = s // SUBLANES * SUBLANES
                offs.append(sum(x.shape[0] for x in terms))
                terms.append(q_i[t0:] * k_i[s:s + 1] * jnp.exp2(b_i[t0:] - b_i[s:s + 1]))
            sums = _dot(jnp.concatenate(terms, axis=0).astype(BF16), ones)
            groups = []
            for t0 in range(0, sub, SUBLANES):
                a_g = jnp.zeros((SUBLANES, L), F32)
                for s in range(min(sub, t0 + SUBLANES)):
                    lo = offs[s] + t0 - s // SUBLANES * SUBLANES
                    a_g = jnp.where(col[:SUBLANES] == r0 + s, sums[lo:lo + SUBLANES], a_g)
                groups.append(a_g)
            a_i = jnp.where(col <= r0 + row, jnp.concatenate(groups, axis=0), 0.0)
            if i > 0:
                ref_row, ref_col = b[r0 - 1:r0], b_t[:, r0 - 1:r0]
                q_s = (q_i * jnp.exp2(b_i - ref_row)).astype(BF16)
                k_s = (k_t * jnp.exp2(ref_col - b_t)).astype(BF16)
                a_i = jnp.where(col < r0, _dot(q_s, k_s), a_i)
            blocks.append(a_i)
        amat = jnp.concatenate(blocks, axis=0)
        o = _dot((q * jnp.exp2(b)).astype(BF16), st.astype(BF16)) + _dot(amat.astype(BF16), vh)
        st_ref[hd] = jnp.exp2(b_last) * st + _dot((k_t * jnp.exp2(b_last - b_t)).astype(BF16), vh)
        gate = g_ref[:, vs]
        o_ref[:, vs] = (_rms(o, ng_ref[:, vs], NORM_EPS) * (gate * _sigmoid(gate))).astype(o_ref.dtype)


def _gla_mix(q, k, v, g, z, gate_w2, gate_b, norm_g):
    B, S, _ = q.shape
    L = SCAN_CHUNK
    kw = GLA_HEADS * GLA_KEY_DIM
    w2 = jnp.pad(gate_w2, ((0, LANES - GLA_GATE_RANK), (0, 0))).astype(BF16)
    tok = lambda n: pl.BlockSpec((None, L, n), lambda b, c: (b, c, 0))
    return pl.pallas_call(
        functools.partial(_gla_kernel, L=L),
        grid=(B, S // L),
        in_specs=[tok(kw), tok(kw), tok(D_MODEL), tok(D_MODEL), tok(LANES),
                  _resident(w2.shape), _resident((1, kw)), _resident((1, D_MODEL))],
        out_specs=tok(D_MODEL),
        out_shape=jax.ShapeDtypeStruct((B, S, D_MODEL), BF16),
        scratch_shapes=[pltpu.VMEM((GLA_HEADS, GLA_KEY_DIM, GLA_VALUE_DIM), F32)],
        compiler_params=_params("parallel", "arbitrary"),
        name="gla",
    )(q, k, v, g, z, w2, gate_b.reshape(1, kw), norm_g.reshape(1, D_MODEL))


def kernel(x, mem, norm_g, ffn_w_up, ffn_w_down, mem_norm_g, mem_w_kv, pool_w_in, pool_w_group, pool_scale, pool_w_out, diff_w_in, diff_lambda, diff_norm_g, diff_w_out, mlstm_w_in, mlstm_conv_w, mlstm_gate_b, mlstm_norm_g, mlstm_w_out, gla_w_in, gla_gate_w2, gla_gate_b, gla_norm_g, gla_w_out, final_norm_g):
    B, S, D = x.shape
    M = mem.shape[1]
    T = B * S
    depth = norm_g.shape[0]
    n_mixers = 4
    mem2 = mem.reshape(B * M, D)
    x2 = x.reshape(T, D)
    norm_g4 = norm_g.reshape(depth, 3, 1, D)
    w_up_all = ffn_w_up.astype(BF16)
    w_down_all = ffn_w_down.astype(BF16)

    def seq(a):
        return a.reshape(B, S, a.shape[-1])

    for i in range(depth):
        kind, j = i % n_mixers, i // n_mixers
        mem_k, mem_v = _norm_proj(mem2, mem_norm_g, [(mem_w_kv[i][:, :X_WIDTH], BF16), (mem_w_kv[i][:, X_WIDTH:], BF16)])
        mem_k = mem_k.reshape(B, M, X_WIDTH)
        mem_v = mem_v.reshape(B, M, X_WIDTH)
        x2 = _ffn(x2, _ffn_weights(norm_g4, w_up_all, w_down_all, i, 0, 0))
        if kind == 0:
            w = pool_w_in[j]
            u, xq = _norm_proj(x2, norm_g[i, 1], [(w[:, :D], F32), (w[:, D:], BF16)])
            mix = _pool_mix(seq(u), pool_w_group[j], pool_scale[j])
            w_out = pool_w_out[j]
        elif kind == 1:
            w = diff_w_in[j]
            q_scale = DIFF_HEAD_DIM ** -0.5 * math.log2(math.e)
            q, k, v, xq = _norm_proj(x2, norm_g[i, 1], [(w[:, :D] * q_scale, BF16), (w[:, D:2 * D], BF16),
                                                        (w[:, 2 * D:3 * D], BF16), (w[:, 3 * D:], BF16)])
            mix = _diff_mix(seq(q), seq(k), seq(v), diff_lambda[j], diff_norm_g[j], i)
            w_out = diff_w_out[j]
        elif kind == 2:
            w = mlstm_w_in[j]
            ng = 2 * MLSTM_HEADS
            qk, v, og, gates, xq = _norm_proj(x2, norm_g[i, 1], [
                (w[:, :2 * D], F32), (w[:, 2 * D:3 * D], BF16), (w[:, 3 * D:4 * D], F32),
                (w[:, 4 * D:4 * D + ng], F32), (w[:, 4 * D + ng:], BF16)])
            mix = _mlstm_mix(seq(qk), seq(v), seq(og), seq(gates), mlstm_conv_w[j], mlstm_gate_b[j], mlstm_norm_g[j])
            w_out = mlstm_w_out[j]
        else:
            w = gla_w_in[j]
            kw = GLA_HEADS * GLA_KEY_DIM
            o0 = 2 * kw + 2 * D
            q, k, v, g, z, xq = _norm_proj(x2, norm_g[i, 1], [
                (w[:, :kw], F32), (w[:, kw:2 * kw], F32), (w[:, 2 * kw:2 * kw + D], BF16),
                (w[:, 2 * kw + D:o0], F32), (w[:, o0:o0 + GLA_GATE_RANK], BF16), (w[:, o0 + GLA_GATE_RANK:], BF16)])
            mix = _gla_mix(seq(q), seq(k), seq(v), seq(g), seq(z), gla_gate_w2[j], gla_gate_b[j], gla_norm_g[j])
            w_out = gla_w_out[j]
        x3 = _attn_out_ffn(seq(x2), mix, seq(xq), mem_k, mem_v, w_out,
                           _ffn_weights(norm_g4, w_up_all, w_down_all, i, 2, 1),
                           final_g=final_norm_g if i == depth - 1 else None)
        x2 = x3.reshape(T, D)
    return x2.reshape(B, S, D)
```

```python
import functools
import math

import jax
import jax.numpy as jnp
from jax import lax
from jax.experimental import pallas as pl
from jax.experimental.pallas import tpu as pltpu

F32 = jnp.float32
BF16 = jnp.bfloat16

D_MODEL = 1024
D_FF = 2816
X_HEADS = 4
X_HEAD_DIM = 128
X_WIDTH = X_HEADS * X_HEAD_DIM
POOL_WINDOWS = (2, 4, 8, 16)
POOL_GROUP_DIM = D_MODEL // len(POOL_WINDOWS)
POOL_HALO = 16
DIFF_HEAD_DIM = 64
DIFF_HEADS = 8
DIFF_HEAD_BLOCK = 8
MLSTM_HEADS = 4
MLSTM_HEAD_DIM = 256
MLSTM_CONV = 4
CONV_HALO = 8
GLA_HEADS = 4
GLA_KEY_DIM = 128
GLA_VALUE_DIM = 256
GLA_GATE_RANK = 16
GLA_TAU = 16.0
GLA_SUBCHUNK = 16
LOG2E = math.log2(math.e)
NORM_EPS = 1e-6
SUBLN_EPS = 1e-5
LANES = 128
SUBLANES = 8

VMEM_LIMIT = 56 * 1024 * 1024
TOKEN_TILE = 512
FF_CHUNKS = (0, 1536, D_FF)
ATTN_TILE = 256
MLSTM_CHUNK = 256
GLA_CHUNK = 128


def _params(*sem):
    return pltpu.CompilerParams(dimension_semantics=sem, vmem_limit_bytes=VMEM_LIMIT)


def _resident(shape):
    nd = len(shape)
    return pl.BlockSpec(shape, lambda *_: (0,) * nd, pipeline_mode=pl.Buffered(1))


def _rms(x, g, eps):
    return x * lax.rsqrt(jnp.mean(x * x, axis=-1, keepdims=True) + eps) * g


def _sigmoid(x):
    return 1.0 / (1.0 + jnp.exp(-x))


def _log_sigmoid(x):
    return jnp.minimum(x, 0.0) - jnp.log(1.0 + jnp.exp(-jnp.abs(x)))


def _dot(a, b):
    return jnp.dot(a, b, preferred_element_type=F32)


def _dot_nt(a, b):
    return lax.dot_general(a, b, (((1,), (1,)), ((), ())), preferred_element_type=F32)


def _split3(x):
    hi = x.astype(BF16)
    r1 = x - hi.astype(F32)
    mid = r1.astype(BF16)
    lo = (r1 - mid.astype(F32)).astype(BF16)
    return hi, mid, lo


def _tri(n, upper):
    r = lax.broadcasted_iota(jnp.int32, (n, n), 0)
    c = lax.broadcasted_iota(jnp.int32, (n, n), 1)
    keep = (r <= c) if upper else (c <= r)
    return jnp.where(keep, 1.0, 0.0).astype(BF16)


def _cumsum_rows(x):
    tri = _tri(x.shape[0], upper=False)
    hi, mid, lo = _split3(x)
    return _dot(tri, hi) + _dot(tri, mid) + _dot(tri, lo)


def _cumsum_lanes(x):
    tri = _tri(x.shape[1], upper=True)
    hi, mid, lo = _split3(x)
    return _dot(hi, tri) + _dot(mid, tri) + _dot(lo, tri)


def _ffn_half_step(x, g_ref, wg_ref, wu_ref, wd_ref):
    h = _rms(x, g_ref[...], NORM_EPS).astype(BF16)
    acc = jnp.zeros_like(x)
    for lo, hi in zip(FF_CHUNKS[:-1], FF_CHUNKS[1:]):
        sl = slice(lo, hi)
        gate = _dot(h, wg_ref[:, sl])
        up = _dot(h, wu_ref[:, sl])
        act = (gate * _sigmoid(gate) * up).astype(BF16)
        acc = acc + _dot(act, wd_ref[sl, :])
    return x + 0.5 * acc


def _stacked(shape, lead, tail=None):
    tail = tail or (0,) * len(shape)
    return pl.BlockSpec((None,) * len(lead) + shape, lambda *_: lead + tail, pipeline_mode=pl.Buffered(1))


def _ffn_weights(norm_g, w_up, w_down, layer, norm_slot, ffn_slot):
    lead = (layer, ffn_slot)
    args = [norm_g, w_up, w_up, w_down]
    specs = [_stacked((1, D_MODEL), (layer, norm_slot)), _stacked((D_MODEL, D_FF), lead, (0, 0)),
             _stacked((D_MODEL, D_FF), lead, (0, 1)), _stacked((D_FF, D_MODEL), lead)]
    return args, specs


def _ffn_kernel(x_ref, g_ref, wg_ref, wu_ref, wd_ref, o_ref):
    o_ref[...] = _ffn_half_step(x_ref[...], g_ref, wg_ref, wu_ref, wd_ref)


def _ffn(x, ffn_weights):
    T = x.shape[0]
    w_args, w_specs = ffn_weights
    tok = pl.BlockSpec((TOKEN_TILE, D_MODEL), lambda i: (i, 0))
    return pl.pallas_call(
        _ffn_kernel,
        grid=(T // TOKEN_TILE,),
        in_specs=[tok] + w_specs,
        out_specs=tok,
        out_shape=jax.ShapeDtypeStruct((T, D_MODEL), F32),
        compiler_params=_params("parallel"),
        name="ffn",
    )(x, *w_args)


def _norm_proj_kernel(x_ref, g_ref, *refs, n_out):
    h = _rms(x_ref[...], g_ref[...], NORM_EPS).astype(BF16)
    for w_ref, o_ref in zip(refs[:n_out], refs[n_out:]):
        y = _dot(h, w_ref[...])
        if len(o_ref.shape) == 2:
            o_ref[...] = y.astype(o_ref.dtype)
            continue
        n_heads, n_sub, rows, t = o_ref.shape
        hd = rows - DIFF_AUX_ROWS
        aux = jnp.where(lax.broadcasted_iota(jnp.int32, (DIFF_AUX_ROWS, t), 0) == 0, 1.0, 0.0).astype(o_ref.dtype)
        for head in range(n_heads):
            for s in range(n_sub):
                o_ref[head, s, 0:hd, :] = y[s * t:(s + 1) * t, head * hd:(head + 1) * hd].T.astype(o_ref.dtype)
                o_ref[head, s, hd:rows, :] = aux


def _norm_proj(x, g, pieces, seq_len=None):
    T = x.shape[0]
    ws, out_specs, out_shapes = [], [], []
    for w, dt, *layout in pieces:
        n = w.shape[1]
        n_pad = -(-n // LANES) * LANES
        if n_pad != n:
            w = jnp.pad(w, ((0, 0), (0, n_pad - n)))
        ws.append(w.astype(BF16))
        if layout:
            hd, t, per_seq = 2 * DIFF_HEAD_DIM, ATTN_TILE, seq_len // TOKEN_TILE
            blk = (None, n // hd, TOKEN_TILE // t, hd + DIFF_AUX_ROWS, t)
            out_specs.append(pl.BlockSpec(blk, lambda i: (i // per_seq, 0, i % per_seq, 0, 0)))
            out_shapes.append(jax.ShapeDtypeStruct((T // seq_len, n // hd, seq_len // t) + blk[3:], dt))
            continue
        out_specs.append(pl.BlockSpec((TOKEN_TILE, n_pad), lambda i: (i, 0)))
        out_shapes.append(jax.ShapeDtypeStruct((T, n_pad), dt))
    in_specs = [pl.BlockSpec((TOKEN_TILE, D_MODEL), lambda i: (i, 0)), _resident((1, D_MODEL))]
    in_specs += [_resident(w.shape) for w in ws]
    return pl.pallas_call(
        functools.partial(_norm_proj_kernel, n_out=len(ws)),
        grid=(T // TOKEN_TILE,),
        in_specs=in_specs,
        out_specs=out_specs,
        out_shape=out_shapes,
        compiler_params=_params("parallel"),
        name="norm_proj",
    )(x, g.reshape(1, D_MODEL), *ws)


def _attn_out_kernel(x_ref, mix_ref, xq_ref, mk_ref, mv_ref, w1_ref, w2_ref, g_ref, wg_ref, wu_ref, wd_ref,
                     *rest, final):
    o_ref = rest[-1]
    xq = xq_ref[...]
    mk = mk_ref[...]
    mv = mv_ref[...]
    outs = []
    for h in range(X_HEADS):
        sl = slice(h * X_HEAD_DIM, (h + 1) * X_HEAD_DIM)
        s = _dot_nt(xq[:, sl], mk[:, sl]) * (X_HEAD_DIM ** -0.5)
        e = jnp.exp(s - jnp.max(s, axis=-1, keepdims=True))
        p = e * (1.0 / jnp.sum(e, axis=-1, keepdims=True))
        outs.append(_dot(p.astype(BF16), mv[:, sl]).astype(BF16))
    xo = jnp.concatenate(outs, axis=-1)
    x = x_ref[...] + _dot(mix_ref[...], w1_ref[...]) + _dot(xo, w2_ref[...])
    y = _ffn_half_step(x, g_ref, wg_ref, wu_ref, wd_ref)
    if final:
        y = _rms(y, rest[0][...], NORM_EPS)
    o_ref[...] = y


def _attn_out_ffn(x, mix, xq, mem_k, mem_v, w_out, ffn_weights, final_g=None):
    B, S, _ = x.shape
    M = mem_k.shape[1]
    w1 = w_out[:D_MODEL].astype(BF16)
    w2 = w_out[D_MODEL:].astype(BF16)
    w_args, w_specs = ffn_weights
    if final_g is not None:
        w_args.append(final_g.reshape(1, D_MODEL))
        w_specs.append(_resident((1, D_MODEL)))
    tok = lambda n: pl.BlockSpec((None, TOKEN_TILE, n), lambda b, i: (b, i, 0))
    mem = pl.BlockSpec((None, M, X_WIDTH), lambda b, i: (b, 0, 0))
    return pl.pallas_call(
        functools.partial(_attn_out_kernel, final=final_g is not None),
        grid=(B, S // TOKEN_TILE),
        in_specs=[tok(D_MODEL), tok(D_MODEL), tok(X_WIDTH), mem, mem, _resident(w1.shape), _resident(w2.shape)]
        + w_specs,
        out_specs=tok(D_MODEL),
        out_shape=jax.ShapeDtypeStruct((B, S, D_MODEL), F32),
        compiler_params=_params("parallel", "parallel"),
        name="attn_out_ffn",
    )(x, mix, xq, mem_k, mem_v, w1, w2, *w_args)


def _pool_kernel(u_ref, wg_ref, sc_ref, o_ref, ext_ref, *, ts):
    j = pl.program_id(1)

    @pl.when(j == 0)
    def _():
        ext_ref[0:POOL_HALO, :] = jnp.zeros((POOL_HALO, D_MODEL), F32)

    ext_ref[POOL_HALO:POOL_HALO + ts, :] = u_ref[...]
    pos = j * ts + lax.broadcasted_iota(jnp.int32, (ts, 1), 0)
    for g, w in enumerate(POOL_WINDOWS):
        sl = slice(g * POOL_GROUP_DIM, (g + 1) * POOL_GROUP_DIM)
        u = u_ref[:, sl]
        acc = u
        for k in range(1, w):
            acc = acc + ext_ref[POOL_HALO - k:POOL_HALO - k + ts, sl]
        inv = 1.0 / jnp.minimum(pos + 1, w).astype(F32)
        pooled = (acc * inv - u).astype(BF16)
        o_ref[:, sl] = (_dot(pooled, wg_ref[g]) * sc_ref[:, sl]).astype(o_ref.dtype)
    ext_ref[0:POOL_HALO, :] = ext_ref[ts:ts + POOL_HALO, :]


def _pool_mix(u, w_group, scale):
    B, S, _ = u.shape
    ts = TOKEN_TILE
    tok = pl.BlockSpec((None, ts, D_MODEL), lambda b, j: (b, j, 0))
    return pl.pallas_call(
        functools.partial(_pool_kernel, ts=ts),
        grid=(B, S // ts),
        in_specs=[tok, _resident(w_group.shape), _resident((1, D_MODEL))],
        out_specs=tok,
        out_shape=jax.ShapeDtypeStruct((B, S, D_MODEL), BF16),
        scratch_shapes=[pltpu.VMEM((ts + POOL_HALO, D_MODEL), F32)],
        compiler_params=_params("parallel", "arbitrary"),
        name="pool_mix",
    )(u, w_group.astype(BF16), scale.reshape(1, D_MODEL))


DIFF_SLOPE_TERMS = 3
DIFF_AUX_ROWS = 16


def _lane_features(lane, groups):
    out = jnp.zeros(lane.shape, F32)
    for g, v in enumerate(groups):
        for r in range(DIFF_SLOPE_TERMS):
            out = jnp.where(lane == DIFF_SLOPE_TERMS * g + r, v[r] if isinstance(v, list) else v, out)
    return out


def _diff_kernel(slopes_ref, lam_ref, q_ref, k_ref, vt_ref, ng_ref, o_ref,
                 ka_ref, acc_ref, m_ref, *, t, n_tiles, lam_init):
    hg = pl.program_id(1)
    qi = pl.program_id(2)
    hd = 2 * DIFF_HEAD_DIM
    lane = lax.broadcasted_iota(jnp.int32, (t, hd), 1)
    rowf = lax.broadcasted_iota(jnp.int32, (t, hd), 0).astype(F32)
    own = (lane < DIFF_HEAD_DIM, lane >= DIFF_HEAD_DIM)
    heads = range(DIFF_HEAD_BLOCK)
    c_terms = [[slopes_ref[hg * DIFF_HEAD_BLOCK + hb, r] for r in range(DIFF_SLOPE_TERMS)] for hb in heads]
    chains = [(hb, c) for hb in heads for c in range(2)]

    @pl.when(qi == 0)
    def _():
        def build(j, carry):
            start = pl.multiple_of(j * t, t)
            tile_idx = jnp.asarray(j, F32)
            for hb in heads:
                neg = [-cr for cr in c_terms[hb]]
                feat = _lane_features(lane, [rowf, tile_idx, neg, [cr * t for cr in neg]])
                ka_ref[hb, pl.ds(start, t), 0:hd] = k_ref[pl.ds(start, t), hb * hd:(hb + 1) * hd]
                ka_ref[hb, pl.ds(start, t), hd:2 * hd] = feat.astype(BF16)
            return carry

        lax.fori_loop(0, n_tiles, build, 0)

    qa = {}
    for hb in heads:
        q = q_ref[:, hb * hd:(hb + 1) * hd]
        feat = _lane_features(lane, [c_terms[hb], [cr * t for cr in c_terms[hb]], rowf, jnp.asarray(qi, F32)])
        for c in range(2):
            qa[hb, c] = jnp.concatenate([jnp.where(own[c], q, jnp.zeros_like(q)), feat.astype(BF16)], axis=1)
    acc_ref[...] = jnp.zeros_like(acc_ref)
    m_ref[...] = jnp.full(m_ref.shape, -jnp.inf, F32)

    def tile(kj, masked):
        start = pl.multiple_of(kj * t, t)
        scores = {}
        for hb, c in chains:
            s = _dot_nt(ka_ref[hb, pl.ds(start, t), :], qa[hb, c])
            if masked:
                key = lax.broadcasted_iota(jnp.int32, (t, t), 0)
                qry = lax.broadcasted_iota(jnp.int32, (t, t), 1)
                s = jnp.where(key <= qry, s, -jnp.inf)
            scores[hb, c] = s
        probs = {}
        for hb, c in chains:
            s = scores[hb, c]
            m_prev = m_ref[hb, c]
            m_new = jnp.maximum(m_prev, jnp.max(s, axis=0, keepdims=True))
            alpha = jnp.exp2(m_prev - m_new)
            p = jnp.exp2(s - m_new)
            m_ref[hb, c] = m_new
            probs[hb, c] = (alpha, p.astype(BF16))
        for hb, c in chains:
            alpha, p = probs[hb, c]
            acc_ref[hb, c] = alpha * acc_ref[hb, c] + _dot(vt_ref[hb, kj], p)

    def body(kj, carry):
        tile(kj, masked=False)
        return carry

    lax.fori_loop(0, qi, body, 0)
    tile(qi, masked=True)

    lp = lam_ref[...]
    lam = (jnp.exp(jnp.sum(lp[0:1] * lp[1:2], axis=-1, keepdims=True))
           - jnp.exp(jnp.sum(lp[2:3] * lp[3:4], axis=-1, keepdims=True)) + lam_init)
    for hb in range(DIFF_HEAD_BLOCK):
        a0, a1 = acc_ref[hb, 0], acc_ref[hb, 1]
        o_t = (a0[:hd] * (1.0 / a0[hd:hd + 1]) - lam * (a1[:hd] * (1.0 / a1[hd:hd + 1])))
        o_ref[:, hb * hd:(hb + 1) * hd] = (_rms(o_t.T, ng_ref[...], SUBLN_EPS) * (1.0 - lam_init)).astype(o_ref.dtype)


def _diff_mix(q, k, v_t, lam_p, norm_g, layer_idx):
    B, S, _ = q.shape
    t = ATTN_TILE
    n_tiles = S // t
    hd = 2 * DIFF_HEAD_DIM
    hb = DIFF_HEAD_BLOCK
    lam_init = 0.8 - 0.6 * math.exp(-0.3 * layer_idx)
    slopes = jnp.asarray([2.0 ** (-8.0 * (h + 1) / DIFF_HEADS) for h in range(DIFF_HEADS)], dtype=F32)
    rest = slopes * math.log2(math.e)
    terms = []
    for _ in range(DIFF_SLOPE_TERMS):
        terms.append(rest.astype(BF16).astype(F32))
        rest = rest - terms[-1]
    slopes = jnp.stack(terms, axis=1)
    qspec = pl.BlockSpec((None, t, hb * hd), lambda b, h, i: (b, i, h))
    return pl.pallas_call(
        functools.partial(_diff_kernel, t=t, n_tiles=n_tiles, lam_init=lam_init),
        grid=(B, DIFF_HEADS // hb, n_tiles),
        in_specs=[pl.BlockSpec(memory_space=pltpu.SMEM), _resident(lam_p.shape), qspec,
                  pl.BlockSpec((None, S, hb * hd), lambda b, h, i: (b, 0, h), pipeline_mode=pl.Buffered(1)),
                  pl.BlockSpec((None, hb, n_tiles, hd + DIFF_AUX_ROWS, t), lambda b, h, i: (b, h, 0, 0, 0),
                               pipeline_mode=pl.Buffered(1)),
                  _resident((1, hd))],
        out_specs=qspec,
        out_shape=jax.ShapeDtypeStruct((B, S, D_MODEL), BF16),
        scratch_shapes=[pltpu.VMEM((hb, S, 2 * hd), BF16), pltpu.VMEM((hb, 2, hd + DIFF_AUX_ROWS, t), F32),
                        pltpu.VMEM((hb, 2, 1, t), F32)],
        compiler_params=_params("parallel", "parallel", "arbitrary"),
        name="diff_attn",
    )(slopes, lam_p, q, k, v_t, norm_g.reshape(1, hd))


def _mlstm_kernel(qk_ref, v_ref, og_ref, gc_ref, gr_ref, cw_ref, gbc_ref, gbr_ref, ng_ref, o_ref,
                  ext_ref, c_ref, n_ref, m_ref, *, L):
    c = pl.program_id(1)
    W = 2 * D_MODEL

    @pl.when(c == 0)
    def _():
        ext_ref[0:CONV_HALO, :] = jnp.zeros((CONV_HALO, W), F32)
        c_ref[...] = jnp.zeros_like(c_ref)
        n_ref[...] = jnp.zeros_like(n_ref)
        m_ref[...] = jnp.zeros_like(m_ref)

    ext_ref[CONV_HALO:CONV_HALO + L, :] = qk_ref[...]
    conv = cw_ref[MLSTM_CONV - 1:MLSTM_CONV, :] * qk_ref[...]
    for j in range(1, MLSTM_CONV):
        conv = conv + (cw_ref[MLSTM_CONV - 1 - j:MLSTM_CONV - j, :]
                       * ext_ref[CONV_HALO - j:CONV_HALO - j + L, :])
    ext_ref[0:CONV_HALO, :] = ext_ref[L:L + CONV_HALO, :]
    qk = conv * _sigmoid(conv)

    gc = gc_ref[...] + gbc_ref[...]
    gr = gr_ref[...] + gbr_ref[...]
    b_c = _cumsum_rows(_log_sigmoid(gc))
    b_r = _cumsum_lanes(_log_sigmoid(gr))
    row = lax.broadcasted_iota(jnp.int32, (L, L), 0)
    col = lax.broadcasted_iota(jnp.int32, (L, L), 1)
    causal = col <= row

    for hd in range(MLSTM_HEADS):
        sl = slice(hd * MLSTM_HEAD_DIM, (hd + 1) * MLSTM_HEAD_DIM)
        q32 = qk[:, sl] * (MLSTM_HEAD_DIM ** -0.5)
        k32 = qk[:, D_MODEL + hd * MLSTM_HEAD_DIM:D_MODEL + (hd + 1) * MLSTM_HEAD_DIM]
        vh = v_ref[:, sl]
        fi = MLSTM_HEADS + hd
        bcol, icol = b_c[:, fi:fi + 1], gc[:, hd:hd + 1]
        brow, irow = b_r[fi:fi + 1, :], gr[hd:hd + 1, :]
        b_last = brow[:, L - 1:L]
        m = m_ref[hd:hd + 1, 0:1]
        cmat = c_ref[hd]
        nrow = n_ref[hd]

        dmat = jnp.where(causal, bcol - brow + irow, -jnp.inf)
        inter = bcol + m
        m_t = jnp.maximum(inter, jnp.max(dmat, axis=-1, keepdims=True))
        dec = jnp.exp(inter - m_t)
        qb = q32.astype(BF16)
        k_t = k32.T
        sqk = _dot(qb, k_t.astype(BF16)) * jnp.exp(dmat - m_t)
        num = dec * _dot(qb, cmat.astype(BF16)) + _dot(sqk.astype(BF16), vh)
        den = dec * jnp.sum(q32 * nrow, axis=-1, keepdims=True) + jnp.sum(sqk, axis=-1, keepdims=True)
        hc = num * (1.0 / jnp.maximum(jnp.abs(den), jnp.exp(-m_t)))

        gs_r = b_last - brow + irow
        gs_c = b_last - bcol + icol
        m_new = jnp.maximum(b_last + m, jnp.max(gs_r, axis=-1, keepdims=True))
        carry_dec = jnp.exp(b_last + m - m_new)
        c_ref[hd] = carry_dec * cmat + _dot((k_t * jnp.exp(gs_r - m_new)).astype(BF16), vh)
        n_ref[hd] = carry_dec * nrow + jnp.sum(k32 * jnp.exp(gs_c - m_new), axis=0, keepdims=True)
        m_ref[hd:hd + 1, :] = jnp.broadcast_to(m_new, (1, LANES))

        o_ref[:, sl] = (_rms(hc, ng_ref[:, sl], NORM_EPS) * _sigmoid(og_ref[:, sl])).astype(o_ref.dtype)


def _mlstm_mix(qk, v, og, gates, conv_w, gate_b, norm_g):
    B, S, _ = qk.shape
    L = MLSTM_CHUNK
    H = MLSTM_HEADS
    gates_r = jnp.transpose(gates[..., :2 * H], (0, 2, 1))
    gb = gate_b.reshape(2 * H)
    gb_c = jnp.pad(gb, (0, LANES - 2 * H)).reshape(1, LANES)
    gb_r = gb.reshape(2 * H, 1)
    tok = lambda n: pl.BlockSpec((None, L, n), lambda b, c: (b, c, 0))
    return pl.pallas_call(
        functools.partial(_mlstm_kernel, L=L),
        grid=(B, S // L),
        in_specs=[tok(2 * D_MODEL), tok(D_MODEL), tok(D_MODEL), tok(LANES),
                  pl.BlockSpec((None, 2 * H, L), lambda b, c: (b, 0, c)),
                  _resident(conv_w.shape), _resident((1, LANES)), _resident((2 * H, 1)), _resident((1, D_MODEL))],
        out_specs=tok(D_MODEL),
        out_shape=jax.ShapeDtypeStruct((B, S, D_MODEL), BF16),
        scratch_shapes=[pltpu.VMEM((L + CONV_HALO, 2 * D_MODEL), F32),
                        pltpu.VMEM((H, MLSTM_HEAD_DIM, MLSTM_HEAD_DIM), F32),
                        pltpu.VMEM((H, 1, MLSTM_HEAD_DIM), F32),
                        pltpu.VMEM((8, LANES), F32)],
        compiler_params=_params("parallel", "arbitrary"),
        name="mlstm",
    )(qk, v, og, gates, gates_r, conv_w, gb_c, gb_r, norm_g.reshape(1, D_MODEL))


def _gla_kernel(q_ref, k_ref, v_ref, g_ref, z_ref, w2_ref, gb_ref, ng_ref, o_ref, st_ref, *, L):
    c = pl.program_id(1)

    @pl.when(c == 0)
    def _():
        st_ref[...] = jnp.zeros_like(st_ref)

    log_a = _log_sigmoid(_dot(z_ref[...], w2_ref[...]) + gb_ref[...]) * (1.0 / GLA_TAU)
    b_all = _cumsum_rows(log_a) * LOG2E
    sub = GLA_SUBCHUNK
    row = lax.broadcasted_iota(jnp.int32, (sub, L), 0)
    col = lax.broadcasted_iota(jnp.int32, (sub, L), 1)
    ones = jnp.ones((GLA_KEY_DIM, LANES), BF16)

    for hd in range(GLA_HEADS):
        ks = slice(hd * GLA_KEY_DIM, (hd + 1) * GLA_KEY_DIM)
        vs = slice(hd * GLA_VALUE_DIM, (hd + 1) * GLA_VALUE_DIM)
        b = b_all[:, ks]
        vh = v_ref[:, vs]
        st = st_ref[hd]
        q = q_ref[:, ks] * (GLA_KEY_DIM ** -0.5)
        k = k_ref[:, ks]
        b_t = b.T
        k_t = k.T
        b_last = b_t[:, L - 1:L]
        blocks = []
        for i in range(L // sub):
            r0 = i * sub
            q_i, k_i, b_i = q[r0:r0 + sub], k[r0:r0 + sub], b[r0:r0 + sub]
            terms, offs = [], []
            for s in range(sub):
                t0 = s // SUBLANES * SUBLANES
                offs.append(sum(x.shape[0] for x in terms))
                terms.append(q_i[t0:] * k_i[s:s + 1] * jnp.exp2(b_i[t0:] - b_i[s:s + 1]))
            sums = _dot(jnp.concatenate(terms, axis=0).astype(BF16), ones)
            groups = []
            for t0 in range(0, sub, SUBLANES):
                a_g = jnp.zeros((SUBLANES, L), F32)
                for s in range(min(sub, t0 + SUBLANES)):
                    lo = offs[s] + t0 - s // SUBLANES * SUBLANES
                    a_g = jnp.where(col[:SUBLANES] == r0 + s, sums[lo:lo + SUBLANES], a_g)
                groups.append(a_g)
            a_i = jnp.where(col <= r0 + row, jnp.concatenate(groups, axis=0), 0.0)
            if i > 0:
                ref_row, ref_col = b[r0 - 1:r0], b_t[:, r0 - 1:r0]
                q_s = (q_i * jnp.exp2(b_i - ref_row)).astype(BF16)
                k_s = (k_t * jnp.exp2(ref_col - b_t)).astype(BF16)
                a_i = jnp.where(col < r0, _dot(q_s, k_s), a_i)
            blocks.append(a_i)
        amat = jnp.concatenate(blocks, axis=0)
        o = _dot((q * jnp.exp2(b)).astype(BF16), st.astype(BF16)) + _dot(amat.astype(BF16), vh)
        st_ref[hd] = jnp.exp2(b_last) * st + _dot((k_t * jnp.exp2(b_last - b_t)).astype(BF16), vh)
        gate = g_ref[:, vs]
        o_ref[:, vs] = (_rms(o, ng_ref[:, vs], NORM_EPS) * (gate * _sigmoid(gate))).astype(o_ref.dtype)


def _gla_mix(q, k, v, g, z, gate_w2, gate_b, norm_g):
    B, S, _ = q.shape
    L = GLA_CHUNK
    kw =GLA_HEADS * GLA_KEY_DIM
    w2 = jnp.pad(gate_w2, ((0, LANES - GLA_GATE_RANK), (0, 0))).astype(BF16)
    tok = lambda n: pl.BlockSpec((None, L, n), lambda b, c: (b, c, 0))
    return pl.pallas_call(
        functools.partial(_gla_kernel, L=L),
        grid=(B, S // L),
        in_specs=[tok(kw), tok(kw), tok(D_MODEL), tok(D_MODEL), tok(LANES),
                  _resident(w2.shape), _resident((1, kw)), _resident((1, D_MODEL))],
        out_specs=tok(D_MODEL),
        out_shape=jax.ShapeDtypeStruct((B, S, D_MODEL), BF16),
        scratch_shapes=[pltpu.VMEM((GLA_HEADS, GLA_KEY_DIM, GLA_VALUE_DIM), F32)],
        compiler_params=_params("parallel", "arbitrary"),
        name="gla",
    )(q, k, v, g, z, w2, gate_b.reshape(1, kw), norm_g.reshape(1, D_MODEL))


def kernel(x, mem, norm_g, ffn_w_up, ffn_w_down, mem_norm_g, mem_w_kv, pool_w_in, pool_w_group, pool_scale, pool_w_out, diff_w_in, diff_lambda, diff_norm_g, diff_w_out, mlstm_w_in, mlstm_conv_w, mlstm_gate_b, mlstm_norm_g, mlstm_w_out, gla_w_in, gla_gate_w2, gla_gate_b, gla_norm_g, gla_w_out, final_norm_g):
    B, S, D = x.shape
    M = mem.shape[1]
    T = B * S
    depth = norm_g.shape[0]
    n_mixers = 4
    mem2 = mem.reshape(B * M, D)
    x2 = x.reshape(T, D)
    norm_g4 = norm_g.reshape(depth, 3, 1, D)
    w_up_all = ffn_w_up.astype(BF16)
    w_down_all = ffn_w_down.astype(BF16)

    def seq(a):
        return a.reshape(B, S, a.shape[-1])

    for i in range(depth):
        kind, j = i % n_mixers, i // n_mixers
        mem_k, mem_v = _norm_proj(mem2, mem_norm_g, [(mem_w_kv[i][:, :X_WIDTH], BF16), (mem_w_kv[i][:, X_WIDTH:], BF16)])
        mem_k = mem_k.reshape(B, M, X_WIDTH)
        mem_v = mem_v.reshape(B, M, X_WIDTH)
        x2 = _ffn(x2, _ffn_weights(norm_g4, w_up_all, w_down_all, i, 0, 0))
        if kind == 0:
            w = pool_w_in[j]
            u, xq = _norm_proj(x2, norm_g[i, 1], [(w[:, :D], F32), (w[:, D:], BF16)])
            mix = _pool_mix(seq(u), pool_w_group[j], pool_scale[j])
            w_out = pool_w_out[j]
        elif kind == 1:
            w = diff_w_in[j]
            q_scale = DIFF_HEAD_DIM ** -0.5 * math.log2(math.e)
            q, k, v_t, xq = _norm_proj(x2, norm_g[i, 1], [
                (w[:, :D] * q_scale, BF16), (w[:, D:2 * D], BF16), (w[:, 2 * D:3 * D], BF16, "values_t"),
                (w[:, 3 * D:], BF16)], seq_len=S)
            mix = _diff_mix(seq(q), seq(k), v_t, diff_lambda[j], diff_norm_g[j], i)
            w_out = diff_w_out[j]
        elif kind == 2:
            w = mlstm_w_in[j]
            ng = 2 * MLSTM_HEADS
            qk, v, og, gates, xq = _norm_proj(x2, norm_g[i, 1], [
                (w[:, :2 * D], F32), (w[:, 2 * D:3 * D], BF16), (w[:, 3 * D:4 * D], F32),
                (w[:, 4 * D:4 * D + ng], F32), (w[:, 4 * D + ng:], BF16)])
            mix = _mlstm_mix(seq(qk), seq(v), seq(og), seq(gates), mlstm_conv_w[j], mlstm_gate_b[j], mlstm_norm_g[j])
            w_out = mlstm_w_out[j]
        else:
            w = gla_w_in[j]
            kw = GLA_HEADS * GLA_KEY_DIM
            o0 = 2 * kw + 2 * D
            q, k, v, g, z, xq = _norm_proj(x2, norm_g[i, 1], [
                (w[:, :kw], F32), (w[:, kw:2 * kw], F32), (w[:, 2 * kw:2 * kw + D], BF16),
                (w[:, 2 * kw + D:o0], F32), (w[:, o0:o0 + GLA_GATE_RANK], BF16), (w[:, o0 + GLA_GATE_RANK:], BF16)])
            mix = _gla_mix(seq(q), seq(k), seq(v), seq(g), seq(z), gla_gate_w2[j], gla_gate_b[j], gla_norm_g[j])
            w_out = gla_w_out[j]
        x3 = _attn_out_ffn(seq(x2), mix, seq(xq), mem_k, mem_v, w_out,
                           _ffn_weights(norm_g4, w_up_all, w_down_all, i, 2, 1),
                           final_g=final_norm_g if i == depth - 1 else None)
        x2 = x3.reshape(T, D)
    return x2.reshape(B, S, D)
```

```python
import functools
import math

import jax
import jax.numpy as jnp
from jax import lax
from jax.experimental import pallas as pl
from jax.experimental.pallas import tpu as pltpu

F32 = jnp.float32
BF16 = jnp.bfloat16

D_MODEL = 1024
D_FF = 2816
X_HEADS = 4
X_HEAD_DIM = 128
X_WIDTH = X_HEADS * X_HEAD_DIM
POOL_WINDOWS = (2, 4, 8, 16)
POOL_GROUP_DIM = D_MODEL // len(POOL_WINDOWS)
POOL_HALO = 16
DIFF_HEAD_DIM = 64
DIFF_HEADS = 8
DIFF_HEAD_BLOCK = 8
MLSTM_HEADS = 4
MLSTM_HEAD_DIM = 256
MLSTM_CONV = 4
CONV_HALO = 8
GLA_HEADS = 4
GLA_KEY_DIM = 128
GLA_VALUE_DIM = 256
GLA_GATE_RANK = 16
GLA_TAU = 16.0
GLA_SUBCHUNK = 16
LOG2E = math.log2(math.e)
NORM_EPS = 1e-6
SUBLN_EPS = 1e-5
LANES = 128
SUBLANES = 8

VMEM_LIMIT = 56 * 1024 * 1024
TOKEN_TILE = 512
FF_CHUNKS = (0, 1536, D_FF)
ATTN_TILE = 256
MLSTM_CHUNK = 256
GLA_CHUNK = 128


def _params(*sem):
    return pltpu.CompilerParams(dimension_semantics=sem, vmem_limit_bytes=VMEM_LIMIT)


def _resident(shape):
    nd = len(shape)
    return pl.BlockSpec(shape, lambda *_: (0,) * nd, pipeline_mode=pl.Buffered(1))


def _rms(x, g, eps):
    return x * lax.rsqrt(jnp.mean(x * x, axis=-1, keepdims=True) + eps) * g


def _sigmoid(x):
    return 1.0 / (1.0 + jnp.exp(-x))


def _log_sigmoid(x):
    return jnp.minimum(x, 0.0) - jnp.log(1.0 + jnp.exp(-jnp.abs(x)))


def _dot(a, b):
    return jnp.dot(a, b, preferred_element_type=F32)


def _dot_nt(a, b):
    return lax.dot_general(a, b, (((1,), (1,)), ((), ())), preferred_element_type=F32)


def _split3(x):
    hi = x.astype(BF16)
    r1 = x - hi.astype(F32)
    mid = r1.astype(BF16)
    lo = (r1 - mid.astype(F32)).astype(BF16)
    return hi, mid, lo


def _tri(n, upper):
    r = lax.broadcasted_iota(jnp.int32, (n, n), 0)
    c = lax.broadcasted_iota(jnp.int32, (n, n), 1)
    keep = (r <= c) if upper else (c <= r)
    return jnp.where(keep, 1.0, 0.0).astype(BF16)


def _cumsum_rows(x):
    tri = _tri(x.shape[0], upper=False)
    hi, mid, lo = _split3(x)
    return _dot(tri, hi) + _dot(tri, mid) + _dot(tri, lo)


def _cumsum_lanes(x):
    tri = _tri(x.shape[1], upper=True)
    hi, mid, lo = _split3(x)
    return _dot(hi, tri) + _dot(mid, tri) + _dot(lo, tri)


def _ffn_half_step(x, g_ref, wg_ref, wu_ref, wd_ref):
    h = _rms(x, g_ref[...], NORM_EPS).astype(BF16)
    acc = jnp.zeros_like(x)
    for lo, hi in zip(FF_CHUNKS[:-1], FF_CHUNKS[1:]):
        sl = slice(lo, hi)
        gate = _dot(h, wg_ref[:, sl])
        up = _dot(h, wu_ref[:, sl])
        act = (gate * _sigmoid(gate) * up).astype(BF16)
        acc = acc + _dot(act, wd_ref[sl, :])
    return x + 0.5 * acc


def _stacked(shape, lead, tail=None):
    tail = tail or (0,) * len(shape)
    return pl.BlockSpec((None,) * len(lead) + shape, lambda *_: lead + tail, pipeline_mode=pl.Buffered(1))


def _ffn_weights(norm_g, w_up, w_down, layer, norm_slot):
    args = [norm_g, w_up, w_up, w_down]
    specs = [_stacked((1, D_MODEL), (layer, norm_slot)), _stacked((D_MODEL, D_FF), (), (0, 0)),
             _stacked((D_MODEL, D_FF), (), (0, 1)), _stacked((D_FF, D_MODEL), ())]
    return args, specs


def _next_weights_cast(w_up_all, w_down_all, layer, slot, step_of, n_steps):
    rows = D_MODEL // n_steps
    assert rows * n_steps == D_MODEL and rows % 16 == 0
    args = [w_up_all, w_down_all.reshape(w_down_all.shape[:2] + (D_MODEL, D_FF))]
    in_specs = [pl.BlockSpec((None, None, rows, n), lambda *idx: (layer, slot, step_of(*idx), 0))
                for n in (2 * D_FF, D_FF)]
    out_specs = [pl.BlockSpec((rows, n), lambda *idx: (step_of(*idx), 0)) for n in (2 * D_FF, D_FF)]
    out_shapes = [jax.ShapeDtypeStruct((D_MODEL, n), BF16) for n in (2 * D_FF, D_FF)]
    return args, in_specs, out_specs, out_shapes


def _ffn_kernel(x_ref, g_ref, wg_ref, wu_ref, wd_ref, nu_ref, nd_ref, o_ref, nu_out_ref, nd_out_ref):
    o_ref[...] = _ffn_half_step(x_ref[...], g_ref, wg_ref, wu_ref, wd_ref)
    nu_out_ref[...] = nu_ref[...].astype(BF16)
    nd_out_ref[...] = nd_ref[...].astype(BF16)


def _ffn(x, ffn_weights, next_cast):
    T = x.shape[0]
    w_args, w_specs = ffn_weights
    n_steps = T // TOKEN_TILE
    c_args, c_in, c_out, c_shapes = _next_weights_cast(*next_cast, step_of=lambda i: i, n_steps=n_steps)
    tok = pl.BlockSpec((TOKEN_TILE, D_MODEL), lambda i: (i, 0))
    y, w_up, w_down = pl.pallas_call(
        _ffn_kernel,
        grid=(n_steps,),
        in_specs=[tok] + w_specs + c_in,
        out_specs=[tok] + c_out,
        out_shape=[jax.ShapeDtypeStruct((T, D_MODEL), F32)] + c_shapes,
        compiler_params=_params("parallel"),
        name="ffn",
    )(x, *w_args, *c_args)
    return y, w_up, w_down.reshape(D_FF, D_MODEL)


def _norm_proj_kernel(x_ref, g_ref, *refs, n_out):
    h = _rms(x_ref[...], g_ref[...], NORM_EPS).astype(BF16)
    for w_ref, o_ref in zip(refs[:n_out], refs[n_out:]):
        y = _dot(h, w_ref[...])
        if len(o_ref.shape) == 2:
            o_ref[...] = y.astype(o_ref.dtype)
            continue
        n_heads, n_sub, rows, t = o_ref.shape
        hd = rows - DIFF_AUX_ROWS
        aux = jnp.where(lax.broadcasted_iota(jnp.int32, (DIFF_AUX_ROWS, t), 0) == 0, 1.0, 0.0).astype(o_ref.dtype)
        for head in range(n_heads):
            for s in range(n_sub):
                o_ref[head, s, 0:hd, :] = y[s * t:(s + 1) * t, head * hd:(head + 1) * hd].T.astype(o_ref.dtype)
                o_ref[head, s, hd:rows, :] = aux


def _norm_proj(x, g, pieces, seq_len=None):
    T = x.shape[0]
    ws, out_specs, out_shapes = [], [], []
    for w, dt, *layout in pieces:
        n = w.shape[1]
        n_pad = -(-n // LANES) * LANES
        if n_pad != n:
            w = jnp.pad(w, ((0, 0), (0, n_pad - n)))
        ws.append(w.astype(BF16))
        if layout:
            hd, t, per_seq = 2 * DIFF_HEAD_DIM, ATTN_TILE, seq_len // TOKEN_TILE
            blk = (None, n // hd, TOKEN_TILE // t, hd + DIFF_AUX_ROWS, t)
            out_specs.append(pl.BlockSpec(blk, lambda i: (i // per_seq, 0, i % per_seq, 0, 0)))
            out_shapes.append(jax.ShapeDtypeStruct((T // seq_len, n // hd, seq_len // t) + blk[3:], dt))
            continue
        out_specs.append(pl.BlockSpec((TOKEN_TILE, n_pad), lambda i: (i, 0)))
        out_shapes.append(jax.ShapeDtypeStruct((T, n_pad), dt))
    in_specs = [pl.BlockSpec((TOKEN_TILE, D_MODEL), lambda i: (i, 0)), _resident((1, D_MODEL))]
    in_specs += [_resident(w.shape) for w in ws]
    return pl.pallas_call(
        functools.partial(_norm_proj_kernel, n_out=len(ws)),
        grid=(T // TOKEN_TILE,),
        in_specs=in_specs,
        out_specs=out_specs,
        out_shape=out_shapes,
        compiler_params=_params("parallel"),
        name="norm_proj",
    )(x, g.reshape(1, D_MODEL), *ws)


def _attn_out_kernel(x_ref, mix_ref, xq_ref, mk_ref, mv_ref, w1_ref, w2_ref, g_ref, wg_ref, wu_ref, wd_ref,
                     *rest, final):
    xq = xq_ref[...]
    mk = mk_ref[...]
    mv = mv_ref[...]
    outs = []
    for h in range(X_HEADS):
        sl = slice(h * X_HEAD_DIM, (h + 1) * X_HEAD_DIM)
        s = _dot_nt(xq[:, sl], mk[:, sl]) * (X_HEAD_DIM ** -0.5)
        e = jnp.exp(s - jnp.max(s, axis=-1, keepdims=True))
        p = e * (1.0 / jnp.sum(e, axis=-1, keepdims=True))
        outs.append(_dot(p.astype(BF16), mv[:, sl]).astype(BF16))
    xo = jnp.concatenate(outs, axis=-1)
    x = x_ref[...] + _dot(mix_ref[...], w1_ref[...]) + _dot(xo, w2_ref[...])
    y = _ffn_half_step(x, g_ref, wg_ref, wu_ref, wd_ref)
    if final:
        fg_ref, o_ref = rest
        o_ref[...] = _rms(y, fg_ref[...], NORM_EPS)
    else:
        nu_ref, nd_ref, o_ref, nu_out_ref, nd_out_ref = rest
        o_ref[...] = y
        nu_out_ref[...] = nu_ref[...].astype(BF16)
        nd_out_ref[...] = nd_ref[...].astype(BF16)


def _attn_out_ffn(x, mix, xq, mem_k, mem_v, w_out, ffn_weights, final_g=None, next_cast=None):
    B, S, _ = x.shape
    M = mem_k.shape[1]
    w1 = w_out[:D_MODEL].astype(BF16)
    w2 = w_out[D_MODEL:].astype(BF16)
    w_args, w_specs = ffn_weights
    tok = lambda n: pl.BlockSpec((None, TOKEN_TILE, n), lambda b, i: (b, i, 0))
    mem = pl.BlockSpec((None, M, X_WIDTH), lambda b, i: (b, 0, 0))
    out_specs, out_shapes = [tok(D_MODEL)], [jax.ShapeDtypeStruct((B, S, D_MODEL), F32)]
    per_seq = S // TOKEN_TILE
    if final_g is not None:
        w_args = w_args + [final_g.reshape(1, D_MODEL)]
        w_specs = w_specs + [_resident((1, D_MODEL))]
    else:
        c_args, c_in, c_out, c_shapes = _next_weights_cast(*next_cast, step_of=lambda b, i: b * per_seq + i,
                                                           n_steps=B * per_seq)
        w_args, w_specs = w_args + c_args, w_specs + c_in
        out_specs, out_shapes = out_specs + c_out, out_shapes + c_shapes
    outs = pl.pallas_call(
        functools.partial(_attn_out_kernel, final=final_g is not None),
        grid=(B, per_seq),
        in_specs=[tok(D_MODEL), tok(D_MODEL), tok(X_WIDTH), mem, mem, _resident(w1.shape), _resident(w2.shape)]
        + w_specs,
        out_specs=out_specs,
        out_shape=out_shapes,
        compiler_params=_params("parallel", "parallel"),
        name="attn_out_ffn",
    )(x, mix, xq, mem_k, mem_v, w1, w2, *w_args)
    if final_g is not None:
        return outs[0], None, None
    return outs[0], outs[1], outs[2].reshape(D_FF, D_MODEL)


def _pool_kernel(u_ref, wg_ref, sc_ref, o_ref, ext_ref, *, ts):
    j = pl.program_id(1)

    @pl.when(j == 0)
    def _():
        ext_ref[0:POOL_HALO, :] = jnp.zeros((POOL_HALO, D_MODEL), F32)

    ext_ref[POOL_HALO:POOL_HALO + ts, :] = u_ref[...]
    pos = j * ts + lax.broadcasted_iota(jnp.int32, (ts, 1), 0)
    for g, w in enumerate(POOL_WINDOWS):
        sl = slice(g * POOL_GROUP_DIM, (g + 1) * POOL_GROUP_DIM)
        u = u_ref[:, sl]
        acc = u
        for k in range(1, w):
            acc = acc + ext_ref[POOL_HALO - k:POOL_HALO - k + ts, sl]
        inv = 1.0 / jnp.minimum(pos + 1, w).astype(F32)
        pooled = (acc * inv - u).astype(BF16)
        o_ref[:, sl] = (_dot(pooled, wg_ref[g]) * sc_ref[:, sl]).astype(o_ref.dtype)
    ext_ref[0:POOL_HALO, :] = ext_ref[ts:ts + POOL_HALO, :]


def _pool_mix(u, w_group, scale):
    B, S, _ = u.shape
    ts = TOKEN_TILE
    tok = pl.BlockSpec((None, ts, D_MODEL), lambda b, j: (b, j, 0))
    return pl.pallas_call(
        functools.partial(_pool_kernel, ts=ts),
        grid=(B, S // ts),
        in_specs=[tok, _resident(w_group.shape), _resident((1, D_MODEL))],
        out_specs=tok,
        out_shape=jax.ShapeDtypeStruct((B, S, D_MODEL), BF16),
        scratch_shapes=[pltpu.VMEM((ts + POOL_HALO, D_MODEL), F32)],
        compiler_params=_params("parallel", "arbitrary"),
        name="pool_mix",
    )(u, w_group.astype(BF16), scale.reshape(1, D_MODEL))


DIFF_SLOPE_TERMS = 3
DIFF_AUX_ROWS = 16


def _lane_features(lane, groups):
    out = jnp.zeros(lane.shape, F32)
    for g, v in enumerate(groups):
        for r in range(DIFF_SLOPE_TERMS):
            out = jnp.where(lane == DIFF_SLOPE_TERMS * g + r, v[r] if isinstance(v, list) else v, out)
    return out


def _diff_kernel(slopes_ref, lam_ref, q_ref, k_ref, vt_ref, ng_ref, o_ref,
                 ka_ref, qf_ref, acc_ref, m_ref, *, t, n_tiles, lam_init):
    hg = pl.program_id(1)
    qi = pl.program_id(2)
    hd = 2 * DIFF_HEAD_DIM
    lane = lax.broadcasted_iota(jnp.int32, (t, hd), 1)
    rowf = lax.broadcasted_iota(jnp.int32, (t, hd), 0).astype(F32)
    own = (lane < DIFF_HEAD_DIM, lane >= DIFF_HEAD_DIM)
    heads = range(DIFF_HEAD_BLOCK)
    c_terms = [[slopes_ref[hg * DIFF_HEAD_BLOCK + hb, r] for r in range(DIFF_SLOPE_TERMS)] for hb in heads]
    chains = [(hb, c) for hb in heads for c in range(2)]

    @pl.when(qi == 0)
    def _():
        def build(j, carry):
            start = pl.multiple_of(j * t, t)
            tile_idx = jnp.asarray(j, F32)
            for hb in heads:
                neg = [-cr for cr in c_terms[hb]]
                feat = _lane_features(lane, [rowf, tile_idx, neg, [cr * t for cr in neg]])
                ka_ref[hb, pl.ds(start, t), 0:hd] = k_ref[pl.ds(start, t), hb * hd:(hb + 1) * hd]
                ka_ref[hb, pl.ds(start, t), hd:2 * hd] = feat.astype(BF16)
            return carry

        lax.fori_loop(0, n_tiles, build, 0)
        for hb in heads:
            qf_ref[hb] = _lane_features(lane, [c_terms[hb], [cr * t for cr in c_terms[hb]], rowf, 0.0])

    qa = {}
    tile_lanes = (lane >= 3 * DIFF_SLOPE_TERMS) & (lane < 4 * DIFF_SLOPE_TERMS)
    for hb in heads:
        q = q_ref[:, hb * hd:(hb + 1) * hd]
        feat = jnp.where(tile_lanes, jnp.asarray(qi, F32), qf_ref[hb])
        for c in range(2):
            qa[hb, c] = jnp.concatenate([jnp.where(own[c], q, jnp.zeros_like(q)), feat.astype(BF16)], axis=1)
    acc_ref[...] = jnp.zeros_like(acc_ref)
    m_ref[...] = jnp.full(m_ref.shape, -jnp.inf, F32)

    def tile(kj, masked):
        start = pl.multiple_of(kj * t, t)
        scores = {}
        for hb, c in chains:
            s = _dot_nt(ka_ref[hb, pl.ds(start, t), :], qa[hb, c])
            if masked:
                key = lax.broadcasted_iota(jnp.int32, (t, t), 0)
                qry = lax.broadcasted_iota(jnp.int32, (t, t), 1)
                s = jnp.where(key <= qry, s, -jnp.inf)
            scores[hb, c] = s
        probs = {}
        for hb, c in chains:
            s = scores[hb, c]
            m_prev = m_ref[hb, c]
            m_new = jnp.maximum(m_prev, jnp.max(s, axis=0, keepdims=True))
            alpha = jnp.exp2(m_prev - m_new)
            p = jnp.exp2(s - m_new)
            m_ref[hb, c] = m_new
            probs[hb, c] = (alpha, p.astype(BF16))
        for hb, c in chains:
            alpha, p = probs[hb, c]
            acc_ref[hb, c] = alpha * acc_ref[hb, c] + _dot(vt_ref[hb, kj], p)

    def body(kj, carry):
        tile(kj, masked=False)
        return carry

    lax.fori_loop(0, qi, body, 0)
    tile(qi, masked=True)

    lp = lam_ref[...]
    lam = (jnp.exp(jnp.sum(lp[0:1] * lp[1:2], axis=-1, keepdims=True))
           - jnp.exp(jnp.sum(lp[2:3] * lp[3:4], axis=-1, keepdims=True)) + lam_init)
    for hb in range(DIFF_HEAD_BLOCK):
        a0, a1 = acc_ref[hb, 0], acc_ref[hb, 1]
        o_t = (a0[:hd] * (1.0 / a0[hd:hd + 1]) - lam * (a1[:hd] * (1.0 / a1[hd:hd + 1])))
        o_ref[:, hb * hd:(hb + 1) * hd] = (_rms(o_t.T, ng_ref[...], SUBLN_EPS) * (1.0 - lam_init)).astype(o_ref.dtype)


def _diff_mix(q, k, v_t, lam_p, norm_g, layer_idx):
    B, S, _ = q.shape
    t = ATTN_TILE
    n_tiles = S // t
    hd = 2 * DIFF_HEAD_DIM
    hb = DIFF_HEAD_BLOCK
    lam_init = 0.8 - 0.6 * math.exp(-0.3 * layer_idx)
    slopes = jnp.asarray([2.0 ** (-8.0 * (h + 1) / DIFF_HEADS) for h in range(DIFF_HEADS)], dtype=F32)
    rest = slopes * math.log2(math.e)
    terms = []
    for _ in range(DIFF_SLOPE_TERMS):
        terms.append(rest.astype(BF16).astype(F32))
        rest = rest - terms[-1]
    slopes = jnp.stack(terms, axis=1)
    qspec = pl.BlockSpec((None, t, hb * hd), lambda b, h, i: (b, i, h))
    return pl.pallas_call(
        functools.partial(_diff_kernel, t=t, n_tiles=n_tiles, lam_init=lam_init),
        grid=(B, DIFF_HEADS // hb, n_tiles),
        in_specs=[pl.BlockSpec(memory_space=pltpu.SMEM), _resident(lam_p.shape), qspec,
                  pl.BlockSpec((None, S, hb * hd), lambda b, h, i: (b, 0, h), pipeline_mode=pl.Buffered(1)),
                  pl.BlockSpec((None, hb, n_tiles, hd + DIFF_AUX_ROWS, t), lambda b, h, i: (b, h, 0, 0, 0),
                               pipeline_mode=pl.Buffered(1)),
                  _resident((1, hd))],
        out_specs=qspec,
        out_shape=jax.ShapeDtypeStruct((B, S, D_MODEL), BF16),
        scratch_shapes=[pltpu.VMEM((hb, S, 2 * hd), BF16), pltpu.VMEM((hb, t, hd), F32),
                        pltpu.VMEM((hb, 2, hd + DIFF_AUX_ROWS, t), F32),
                        pltpu.VMEM((hb, 2, 1, t), F32)],
        compiler_params=_params("parallel", "parallel", "arbitrary"),
        name="diff_attn",
    )(slopes, lam_p, q, k, v_t, norm_g.reshape(1, hd))


def _mlstm_kernel(qk_ref, v_ref, og_ref, gc_ref, gr_ref, cw_ref, gbc_ref, gbr_ref, ng_ref, o_ref,
                  ext_ref, c_ref, n_ref, m_ref, *, L):
    c = pl.program_id(1)
    W = 2 * D_MODEL

    @pl.when(c == 0)
    def _():
        ext_ref[0:CONV_HALO, :] = jnp.zeros((CONV_HALO, W), F32)
        c_ref[...] = jnp.zeros_like(c_ref)
        n_ref[...] = jnp.zeros_like(n_ref)
        m_ref[...] = jnp.zeros_like(m_ref)

    ext_ref[CONV_HALO:CONV_HALO + L, :] = qk_ref[...]
    conv = cw_ref[MLSTM_CONV - 1:MLSTM_CONV, :] * qk_ref[...]
    for j in range(1, MLSTM_CONV):
        conv = conv + (cw_ref[MLSTM_CONV - 1 - j:MLSTM_CONV - j, :]
                       * ext_ref[CONV_HALO - j:CONV_HALO - j + L, :])
    ext_ref[0:CONV_HALO, :] = ext_ref[L:L + CONV_HALO, :]
    qk = conv * _sigmoid(conv)

    gc = gc_ref[...] + gbc_ref[...]
    gr = gr_ref[...] + gbr_ref[...]
    b_c = _cumsum_rows(_log_sigmoid(gc))
    b_r = _cumsum_lanes(_log_sigmoid(gr))
    row = lax.broadcasted_iota(jnp.int32, (L, L), 0)
    col = lax.broadcasted_iota(jnp.int32, (L, L), 1)
    causal = col <= row

    for hd in range(MLSTM_HEADS):
        sl = slice(hd * MLSTM_HEAD_DIM, (hd + 1) * MLSTM_HEAD_DIM)
        q32 = qk[:, sl] * (MLSTM_HEAD_DIM ** -0.5)
        k32 = qk[:, D_MODEL + hd * MLSTM_HEAD_DIM:D_MODEL + (hd + 1) * MLSTM_HEAD_DIM]
        vh = v_ref[:, sl]
        fi = MLSTM_HEADS + hd
        bcol, icol = b_c[:, fi:fi + 1], gc[:, hd:hd + 1]
        brow, irow = b_r[fi:fi + 1, :], gr[hd:hd + 1, :]
        b_last = brow[:, L - 1:L]
        m = m_ref[hd:hd + 1, 0:1]
        cmat = c_ref[hd]
        nrow = n_ref[hd]

        dmat = jnp.where(causal, bcol - brow + irow, -jnp.inf)
        inter = bcol + m
        m_t = jnp.maximum(inter, jnp.max(dmat, axis=-1, keepdims=True))
        dec = jnp.exp(inter - m_t)
        qb = q32.astype(BF16)
        k_t = k32.T
        sqk = _dot(qb, k_t.astype(BF16)) * jnp.exp(dmat - m_t)
        num = dec * _dot(qb, cmat.astype(BF16)) + _dot(sqk.astype(BF16), vh)
        den = dec * jnp.sum(q32 * nrow, axis=-1, keepdims=True) + jnp.sum(sqk, axis=-1, keepdims=True)
        hc = num * (1.0 / jnp.maximum(jnp.abs(den), jnp.exp(-m_t)))

        gs_r = b_last - brow + irow
        gs_c = b_last - bcol + icol
        m_new = jnp.maximum(b_last + m, jnp.max(gs_r, axis=-1, keepdims=True))
        carry_dec = jnp.exp(b_last + m - m_new)
        c_ref[hd] = carry_dec * cmat + _dot((k_t * jnp.exp(gs_r - m_new)).astype(BF16), vh)
        n_ref[hd] = carry_dec * nrow + jnp.sum(k32 * jnp.exp(gs_c - m_new), axis=0, keepdims=True)
        m_ref[hd:hd + 1, :] = jnp.broadcast_to(m_new, (1, LANES))

        o_ref[:, sl] = (_rms(hc, ng_ref[:, sl], NORM_EPS) * _sigmoid(og_ref[:, sl])).astype(o_ref.dtype)


def _mlstm_mix(qk, v, og, gates, conv_w, gate_b, norm_g):
    B, S, _ = qk.shape
    L = MLSTM_CHUNK
    H = MLSTM_HEADS
    gates_r = jnp.transpose(gates[..., :2 * H], (0, 2, 1))
    gb = gate_b.reshape(2 * H)
    gb_c = jnp.pad(gb, (0, LANES - 2 * H)).reshape(1, LANES)
    gb_r = gb.reshape(2 * H, 1)
    tok = lambda n: pl.BlockSpec((None, L, n), lambda b, c: (b, c, 0))
    return pl.pallas_call(
        functools.partial(_mlstm_kernel, L=L),
        grid=(B, S // L),
        in_specs=[tok(2 * D_MODEL), tok(D_MODEL), tok(D_MODEL), tok(LANES),
                  pl.BlockSpec((None, 2 * H, L), lambda b, c: (b, 0, c)),
                  _resident(conv_w.shape), _resident((1, LANES)), _resident((2 * H, 1)), _resident((1, D_MODEL))],
        out_specs=tok(D_MODEL),
        out_shape=jax.ShapeDtypeStruct((B, S, D_MODEL), BF16),
        scratch_shapes=[pltpu.VMEM((L + CONV_HALO, 2 * D_MODEL), F32),
                        pltpu.VMEM((H, MLSTM_HEAD_DIM, MLSTM_HEAD_DIM), F32),
                        pltpu.VMEM((H, 1, MLSTM_HEAD_DIM), F32),
                        pltpu.VMEM((8, LANES), F32)],
        compiler_params=_params("parallel", "arbitrary"),
        name="mlstm",
    )(qk, v, og, gates, gates_r, conv_w, gb_c, gb_r, norm_g.reshape(1, D_MODEL))


def _gla_kernel(q_ref, k_ref, v_ref, g_ref, z_ref, w2_ref, gb_ref, ng_ref, o_ref, st_ref, *, L):
    c = pl.program_id(1)

    @pl.when(c == 0)
    def _():
        st_ref[...] = jnp.zeros_like(st_ref)

    log_a = _log_sigmoid(_dot(z_ref[...], w2_ref[...]) + gb_ref[...]) * (1.0 / GLA_TAU)
    b_all = _cumsum_rows(log_a) * LOG2E
    sub = GLA_SUBCHUNK
    row = lax.broadcasted_iota(jnp.int32, (sub, L), 0)
    col = lax.broadcasted_iota(jnp.int32, (sub, L), 1)
    ones = jnp.ones((GLA_KEY_DIM, LANES), BF16)

    for hd in range(GLA_HEADS):
        ks = slice(hd * GLA_KEY_DIM, (hd + 1) * GLA_KEY_DIM)
        vs = slice(hd * GLA_VALUE_DIM, (hd + 1) * GLA_VALUE_DIM)
        b = b_all[:, ks]
        vh = v_ref[:, vs]
        st = st_ref[hd]
        q = q_ref[:, ks] * (GLA_KEY_DIM ** -0.5)
        k = k_ref[:, ks]
        b_t = b.T
        k_t = k.T
        b_last = b_t[:, L - 1:L]
        blocks = []
        for i in range(L // sub):
            r0 = i * sub
            q_i, k_i, b_i = q[r0:r0 + sub], k[r0:r0 + sub], b[r0:r0 + sub]
            terms, offs = [], []
            for s in range(sub):
                t0 = s // SUBLANES * SUBLANES
                offs.append(sum(x.shape[0] for x in terms))
                terms.append(q_i[t0:] * k_i[s:s + 1] * jnp.exp2(b_i[t0:] - b_i[s:s + 1]))
            sums = _dot(jnp.concatenate(terms, axis=0).astype(BF16), ones)
            groups = []
            for t0 in range(0, sub, SUBLANES):
                a_g = jnp.zeros((SUBLANES, L), F32)
                for s in range(min(sub, t0 + SUBLANES)):
                    lo = offs[s] + t0 - s // SUBLANES * SUBLANES
                    a_g = jnp.where(col[:SUBLANES] == r0 + s, sums[lo:lo + SUBLANES], a_g)
                groups.append(a_g)
            a_i = jnp.where(col <= r0 + row, jnp.concatenate(groups, axis=0), 0.0)
            if i > 0:
                ref_row, ref_col = b[r0 - 1:r0], b_t[:, r0 - 1:r0]
                q_s = (q_i * jnp.exp2(b_i - ref_row)).astype(BF16)
                k_s = (k_t * jnp.exp2(ref_col - b_t)).astype(BF16)
                a_i = jnp.where(col < r0, _dot(q_s, k_s), a_i)
            blocks.append(a_i)
        amat = jnp.concatenate(blocks, axis=0)
        o = _dot((q * jnp.exp2(b)).astype(BF16), st.astype(BF16)) + _dot(amat.astype(BF16), vh)
        st_ref[hd] = jnp.exp2(b_last) * st + _dot((k_t * jnp.exp2(b_last - b_t)).astype(BF16), vh)
        gate = g_ref[:, vs]
        o_ref[:, vs] = (_rms(o, ng_ref[:, vs], NORM_EPS) * (gate * _sigmoid(gate))).astype(o_ref.dtype)


def _gla_mix(q, k, v, g, z, gate_w2, gate_b, norm_g):
    B, S, _ = q.shape
    L = GLA_CHUNK
    kw =GLA_HEADS * GLA_KEY_DIM
    w2 = jnp.pad(gate_w2, ((0, LANES - GLA_GATE_RANK), (0, 0))).astype(BF16)
    tok = lambda n: pl.BlockSpec((None, L, n), lambda b, c: (b, c, 0))
    return pl.pallas_call(
        functools.partial(_gla_kernel, L=L),
        grid=(B, S // L),
        in_specs=[tok(kw), tok(kw), tok(D_MODEL), tok(D_MODEL), tok(LANES),
                  _resident(w2.shape), _resident((1, kw)), _resident((1, D_MODEL))],
        out_specs=tok(D_MODEL),
        out_shape=jax.ShapeDtypeStruct((B, S, D_MODEL), BF16),
        scratch_shapes=[pltpu.VMEM((GLA_HEADS, GLA_KEY_DIM, GLA_VALUE_DIM), F32)],
        compiler_params=_params("parallel", "arbitrary"),
        name="gla",
    )(q, k, v, g, z, w2, gate_b.reshape(1, kw), norm_g.reshape(1, D_MODEL))


def kernel(x, mem, norm_g, ffn_w_up, ffn_w_down, mem_norm_g, mem_w_kv, pool_w_in, pool_w_group, pool_scale, pool_w_out, diff_w_in, diff_lambda, diff_norm_g, diff_w_out, mlstm_w_in, mlstm_conv_w, mlstm_gate_b, mlstm_norm_g, mlstm_w_out, gla_w_in, gla_gate_w2, gla_gate_b, gla_norm_g, gla_w_out, final_norm_g):
    B, S, D = x.shape
    M = mem.shape[1]
    T = B * S
    depth = norm_g.shape[0]
    n_mixers = 4
    mem2 = mem.reshape(B * M, D)
    x2 = x.reshape(T, D)
    norm_g4 = norm_g.reshape(depth, 3, 1, D)
    w_up, w_down = ffn_w_up[0, 0].astype(BF16), ffn_w_down[0, 0].astype(BF16)

    def seq(a):
        return a.reshape(B, S, a.shape[-1])

    for i in range(depth):
        kind, j = i % n_mixers, i // n_mixers
        mem_k, mem_v = _norm_proj(mem2, mem_norm_g, [(mem_w_kv[i][:, :X_WIDTH], BF16), (mem_w_kv[i][:, X_WIDTH:], BF16)])
        mem_k = mem_k.reshape(B, M, X_WIDTH)
        mem_v = mem_v.reshape(B, M, X_WIDTH)
        x2, w_up, w_down = _ffn(x2, _ffn_weights(norm_g4, w_up, w_down, i, 0), (ffn_w_up, ffn_w_down, i, 1))
        if kind == 0:
            w = pool_w_in[j]
            u, xq = _norm_proj(x2, norm_g[i, 1], [(w[:, :D], F32), (w[:, D:], BF16)])
            mix = _pool_mix(seq(u), pool_w_group[j], pool_scale[j])
            w_out = pool_w_out[j]
        elif kind == 1:
            w = diff_w_in[j]
            q_scale = DIFF_HEAD_DIM ** -0.5 * math.log2(math.e)
            q, k, v_t, xq = _norm_proj(x2, norm_g[i, 1], [
                (w[:, :D] * q_scale, BF16), (w[:, D:2 * D], BF16), (w[:, 2 * D:3 * D], BF16, "values_t"),
                (w[:, 3 * D:], BF16)], seq_len=S)
            mix = _diff_mix(seq(q), seq(k), v_t, diff_lambda[j], diff_norm_g[j], i)
            w_out = diff_w_out[j]
        elif kind == 2:
            w = mlstm_w_in[j]
            ng = 2 * MLSTM_HEADS
            qk, v, og, gates, xq = _norm_proj(x2, norm_g[i, 1], [
                (w[:, :2 * D], F32), (w[:, 2 * D:3 * D], BF16), (w[:, 3 * D:4 * D], F32),
                (w[:, 4 * D:4 * D + ng], F32), (w[:, 4 * D + ng:], BF16)])
            mix = _mlstm_mix(seq(qk), seq(v), seq(og), seq(gates), mlstm_conv_w[j], mlstm_gate_b[j], mlstm_norm_g[j])
            w_out = mlstm_w_out[j]
        else:
            w = gla_w_in[j]
            kw = GLA_HEADS * GLA_KEY_DIM
            o0 = 2 * kw + 2 * D
            q, k, v, g, z, xq = _norm_proj(x2, norm_g[i, 1], [
                (w[:, :kw], F32), (w[:, kw:2 * kw], F32), (w[:, 2 * kw:2 * kw + D], BF16),
                (w[:, 2 * kw + D:o0], F32), (w[:, o0:o0 + GLA_GATE_RANK], BF16), (w[:, o0 + GLA_GATE_RANK:], BF16)])
            mix = _gla_mix(seq(q), seq(k), seq(v), seq(g), seq(z), gla_gate_w2[j], gla_gate_b[j], gla_norm_g[j])
            w_out = gla_w_out[j]
        last = i == depth - 1
        x3, w_up, w_down = _attn_out_ffn(seq(x2), mix, seq(xq), mem_k, mem_v, w_out,
                                         _ffn_weights(norm_g4, w_up, w_down, i, 2),
                                         final_g=final_norm_g if last else None,
                                         next_cast=None if last else (ffn_w_up, ffn_w_down, i + 1, 0))
        x2 = x3.reshape(T, D)
    return x2.reshape(B, S, D)
```

```python
import functools
import math

import jax
import jax.numpy as jnp
from jax import lax
from jax.experimental import pallas as pl
from jax.experimental.pallas import tpu as pltpu

F32 = jnp.float32
BF16 = jnp.bfloat16

D_MODEL = 1024
D_FF = 2816
X_HEADS = 4
X_HEAD_DIM = 128
X_WIDTH = X_HEADS * X_HEAD_DIM
POOL_WINDOWS = (2, 4, 8, 16)
POOL_GROUP_DIM = D_MODEL // len(POOL_WINDOWS)
POOL_HALO = 16
DIFF_HEAD_DIM = 64
DIFF_HEADS = 8
DIFF_HEAD_BLOCK = 8
MLSTM_HEADS = 4
MLSTM_HEAD_DIM = 256
MLSTM_CONV = 4
CONV_HALO = 8
GLA_HEADS = 4
GLA_KEY_DIM = 128
GLA_VALUE_DIM = 256
GLA_GATE_RANK = 16
GLA_TAU = 16.0
GLA_SUBCHUNK = 16
LOG2E = math.log2(math.e)
NORM_EPS = 1e-6
SUBLN_EPS = 1e-5
LANES = 128
SUBLANES = 8

VMEM_LIMIT = 56 * 1024 * 1024
TOKEN_TILE = 512
FF_CHUNKS = (0, 1536, D_FF)
ATTN_TILE = 256
MLSTM_CHUNK = 256
GLA_CHUNK = 128


def _params(*sem):
    return pltpu.CompilerParams(dimension_semantics=sem, vmem_limit_bytes=VMEM_LIMIT)


def _resident(shape):
    nd = len(shape)
    return pl.BlockSpec(shape, lambda *_: (0,) * nd, pipeline_mode=pl.Buffered(1))


def _rms(x, g, eps):
    return x * lax.rsqrt(jnp.mean(x * x, axis=-1, keepdims=True) + eps) * g


def _sigmoid(x):
    return 1.0 / (1.0 + jnp.exp(-x))


def _log_sigmoid(x):
    return jnp.minimum(x, 0.0) - jnp.log(1.0 + jnp.exp(-jnp.abs(x)))


def _dot(a, b):
    return jnp.dot(a, b, preferred_element_type=F32)


def _dot_nt(a, b):
    return lax.dot_general(a, b, (((1,), (1,)), ((), ())), preferred_element_type=F32)


def _split3(x):
    hi = x.astype(BF16)
    r1 = x - hi.astype(F32)
    mid = r1.astype(BF16)
    lo = (r1 - mid.astype(F32)).astype(BF16)
    return hi, mid, lo


def _tri(n, upper):
    r = lax.broadcasted_iota(jnp.int32, (n, n), 0)
    c = lax.broadcasted_iota(jnp.int32, (n, n), 1)
    keep = (r <= c) if upper else (c <= r)
    return jnp.where(keep, 1.0, 0.0).astype(BF16)


def _cumsum_rows(x):
    tri = _tri(x.shape[0], upper=False)
    hi, mid, lo = _split3(x)
    return _dot(tri, hi) + _dot(tri, mid) + _dot(tri, lo)


def _cumsum_lanes(x):
    tri = _tri(x.shape[1], upper=True)
    hi, mid, lo = _split3(x)
    return _dot(hi, tri) + _dot(mid, tri) + _dot(lo, tri)


def _ffn_half_step(x, g_ref, wg_ref, wu_ref, wd_ref):
    h = _rms(x, g_ref[...], NORM_EPS).astype(BF16)
    acc = jnp.zeros_like(x)
    for lo, hi in zip(FF_CHUNKS[:-1], FF_CHUNKS[1:]):
        sl = slice(lo, hi)
        gate = _dot(h, wg_ref[:, sl])
        up = _dot(h, wu_ref[:, sl])
        act = (gate * _sigmoid(gate) * up).astype(BF16)
        acc = acc + _dot(act, wd_ref[sl, :])
    return x + 0.5 * acc


def _stacked(shape, lead, tail=None):
    tail = tail or (0,) * len(shape)
    return pl.BlockSpec((None,) * len(lead) + shape, lambda *_: lead + tail, pipeline_mode=pl.Buffered(1))


def _ffn_weights(norm_g, w_up, w_down, layer, norm_slot):
    args = [norm_g, w_up, w_up, w_down]
    specs = [_stacked((1, D_MODEL), (layer, norm_slot)), _stacked((D_MODEL, D_FF), (), (0, 0)),
             _stacked((D_MODEL, D_FF), (), (0, 1)), _stacked((D_FF, D_MODEL), ())]
    return args, specs


def _next_weights_cast(w_up_all, w_down_all, layer, slot, step_of, n_steps):
    args, in_specs, out_specs, out_shapes = [w_up_all, w_down_all], [], [], []
    for w in args:
        n_rows, n_cols = w.shape[2:]
        n_slabs = max(s for s in range(1, n_steps + 1) if n_rows % (16 * s) == 0)
        slab = lambda *idx, n_slabs=n_slabs: jnp.minimum(step_of(*idx), n_slabs - 1)
        in_specs.append(pl.BlockSpec((None, None, n_rows // n_slabs, n_cols),
                                     lambda *idx, slab=slab: (layer, slot, slab(*idx), 0)))
        out_specs.append(pl.BlockSpec((n_rows // n_slabs, n_cols), lambda *idx, slab=slab: (slab(*idx), 0)))
        out_shapes.append(jax.ShapeDtypeStruct((n_rows, n_cols), BF16))
    return args, in_specs, out_specs, out_shapes


def _ffn_kernel(x_ref, g_ref, wg_ref, wu_ref, wd_ref, nu_ref, nd_ref, o_ref, nu_out_ref, nd_out_ref):
    o_ref[...] = _ffn_half_step(x_ref[...], g_ref, wg_ref, wu_ref, wd_ref)
    nu_out_ref[...] = nu_ref[...].astype(BF16)
    nd_out_ref[...] = nd_ref[...].astype(BF16)


def _ffn(x, ffn_weights, next_cast):
    T = x.shape[0]
    w_args, w_specs = ffn_weights
    n_steps = T // TOKEN_TILE
    c_args, c_in, c_out, c_shapes = _next_weights_cast(*next_cast, step_of=lambda i: i, n_steps=n_steps)
    tok = pl.BlockSpec((TOKEN_TILE, D_MODEL), lambda i: (i, 0))
    y, w_up, w_down = pl.pallas_call(
        _ffn_kernel,
        grid=(n_steps,),
        in_specs=[tok] + w_specs + c_in,
        out_specs=[tok] + c_out,
        out_shape=[jax.ShapeDtypeStruct((T, D_MODEL), F32)] + c_shapes,
        compiler_params=_params("parallel"),
        name="ffn",
    )(x, *w_args, *c_args)
    return y, w_up, w_down


def _norm_proj_kernel(x_ref, g_ref, *refs, n_out):
    h = _rms(x_ref[...], g_ref[...], NORM_EPS).astype(BF16)
    for w_ref, o_ref in zip(refs[:n_out], refs[n_out:]):
        y = _dot(h, w_ref[...])
        if len(o_ref.shape) == 2:
            o_ref[...] = y.astype(o_ref.dtype)
            continue
        n_heads, n_sub, rows, t = o_ref.shape
        hd = rows - DIFF_AUX_ROWS
        aux = jnp.where(lax.broadcasted_iota(jnp.int32, (DIFF_AUX_ROWS, t), 0) == 0, 1.0, 0.0).astype(o_ref.dtype)
        for head in range(n_heads):
            for s in range(n_sub):
                o_ref[head, s, 0:hd, :] = y[s * t:(s + 1) * t, head * hd:(head + 1) * hd].T.astype(o_ref.dtype)
                o_ref[head, s, hd:rows, :] = aux


def _norm_proj(x, g, pieces, seq_len=None):
    T = x.shape[0]
    ws, out_specs, out_shapes = [], [], []
    for w, dt, *layout in pieces:
        n = w.shape[1]
        n_pad = -(-n // LANES) * LANES
        if n_pad != n:
            w = jnp.pad(w, ((0, 0), (0, n_pad - n)))
        ws.append(w.astype(BF16))
        if layout:
            hd, t, per_seq = 2 * DIFF_HEAD_DIM, ATTN_TILE, seq_len // TOKEN_TILE
            blk = (None, n // hd, TOKEN_TILE // t, hd + DIFF_AUX_ROWS, t)
            out_specs.append(pl.BlockSpec(blk, lambda i: (i // per_seq, 0, i % per_seq, 0, 0)))
            out_shapes.append(jax.ShapeDtypeStruct((T // seq_len, n // hd, seq_len // t) + blk[3:], dt))
            continue
        out_specs.append(pl.BlockSpec((TOKEN_TILE, n_pad), lambda i: (i, 0)))
        out_shapes.append(jax.ShapeDtypeStruct((T, n_pad), dt))
    in_specs = [pl.BlockSpec((TOKEN_TILE, D_MODEL), lambda i: (i, 0)), _resident((1, D_MODEL))]
    in_specs += [_resident(w.shape) for w in ws]
    return pl.pallas_call(
        functools.partial(_norm_proj_kernel, n_out=len(ws)),
        grid=(T // TOKEN_TILE,),
        in_specs=in_specs,
        out_specs=out_specs,
        out_shape=out_shapes,
        compiler_params=_params("parallel"),
        name="norm_proj",
    )(x, g.reshape(1, D_MODEL), *ws)


def _attn_out_kernel(x_ref, mix_ref, xq_ref, mk_ref, mv_ref, w1_ref, w2_ref, g_ref, wg_ref, wu_ref, wd_ref,
                     *rest, final):
    xq = xq_ref[...]
    mk = mk_ref[...]
    mv = mv_ref[...]
    outs = []
    for h in range(X_HEADS):
        sl = slice(h * X_HEAD_DIM, (h + 1) * X_HEAD_DIM)
        s = _dot_nt(xq[:, sl], mk[:, sl]) * (X_HEAD_DIM ** -0.5)
        e = jnp.exp(s - jnp.max(s, axis=-1, keepdims=True))
        p = e * (1.0 / jnp.sum(e, axis=-1, keepdims=True))
        outs.append(_dot(p.astype(BF16), mv[:, sl]).astype(BF16))
    xo = jnp.concatenate(outs, axis=-1)
    x = x_ref[...] + _dot(mix_ref[...], w1_ref[...]) + _dot(xo, w2_ref[...])
    y = _ffn_half_step(x, g_ref, wg_ref, wu_ref, wd_ref)
    if final:
        fg_ref, o_ref = rest
        o_ref[...] = _rms(y, fg_ref[...], NORM_EPS)
    else:
        nu_ref, nd_ref, o_ref, nu_out_ref, nd_out_ref = rest
        o_ref[...] = y
        nu_out_ref[...] = nu_ref[...].astype(BF16)
        nd_out_ref[...] = nd_ref[...].astype(BF16)


def _attn_out_ffn(x, mix, xq, mem_k, mem_v, w_out, ffn_weights, final_g=None, next_cast=None):
    B, S, _ = x.shape
    M = mem_k.shape[1]
    w1 = w_out[:D_MODEL].astype(BF16)
    w2 = w_out[D_MODEL:].astype(BF16)
    w_args, w_specs = ffn_weights
    tok = lambda n: pl.BlockSpec((None, TOKEN_TILE, n), lambda b, i: (b, i, 0))
    mem = pl.BlockSpec((None, M, X_WIDTH), lambda b, i: (b, 0, 0))
    out_specs, out_shapes = [tok(D_MODEL)], [jax.ShapeDtypeStruct((B, S, D_MODEL), F32)]
    per_seq = S // TOKEN_TILE
    if final_g is not None:
        w_args = w_args + [final_g.reshape(1, D_MODEL)]
        w_specs = w_specs + [_resident((1, D_MODEL))]
    else:
        c_args, c_in, c_out, c_shapes = _next_weights_cast(*next_cast, step_of=lambda b, i: b * per_seq + i,
                                                           n_steps=B * per_seq)
        w_args, w_specs = w_args + c_args, w_specs + c_in
        out_specs, out_shapes = out_specs + c_out, out_shapes + c_shapes
    outs = pl.pallas_call(
        functools.partial(_attn_out_kernel, final=final_g is not None),
        grid=(B, per_seq),
        in_specs=[tok(D_MODEL), tok(D_MODEL), tok(X_WIDTH), mem, mem, _resident(w1.shape), _resident(w2.shape)]
        + w_specs,
        out_specs=out_specs,
        out_shape=out_shapes,
        compiler_params=_params("parallel", "parallel"),
        name="attn_out_ffn",
    )(x, mix, xq, mem_k, mem_v, w1, w2, *w_args)
    if final_g is not None:
        return outs[0], None, None
    return outs[0], outs[1], outs[2]


def _pool_kernel(u_ref, wg_ref, sc_ref, o_ref, ext_ref, *, ts):
    j = pl.program_id(1)

    @pl.when(j == 0)
    def _():
        ext_ref[0:POOL_HALO, :] = jnp.zeros((POOL_HALO, D_MODEL), F32)

    ext_ref[POOL_HALO:POOL_HALO + ts, :] = u_ref[...]
    pos = j * ts + lax.broadcasted_iota(jnp.int32, (ts, 1), 0)
    for g, w in enumerate(POOL_WINDOWS):
        sl = slice(g * POOL_GROUP_DIM, (g + 1) * POOL_GROUP_DIM)
        u = u_ref[:, sl]
        acc = u
        for k in range(1, w):
            acc = acc + ext_ref[POOL_HALO - k:POOL_HALO - k + ts, sl]
        inv = 1.0 / jnp.minimum(pos + 1, w).astype(F32)
        pooled = (acc * inv - u).astype(BF16)
        o_ref[:, sl] = (_dot(pooled, wg_ref[g]) * sc_ref[:, sl]).astype(o_ref.dtype)
    ext_ref[0:POOL_HALO, :] = ext_ref[ts:ts + POOL_HALO, :]


def _pool_mix(u, w_group, scale):
    B, S, _ = u.shape
    ts = TOKEN_TILE
    tok = pl.BlockSpec((None, ts, D_MODEL), lambda b, j: (b, j, 0))
    return pl.pallas_call(
        functools.partial(_pool_kernel, ts=ts),
        grid=(B, S // ts),
        in_specs=[tok, _resident(w_group.shape), _resident((1, D_MODEL))],
        out_specs=tok,
        out_shape=jax.ShapeDtypeStruct((B, S, D_MODEL), BF16),
        scratch_shapes=[pltpu.VMEM((ts + POOL_HALO, D_MODEL), F32)],
        compiler_params=_params("parallel", "arbitrary"),
        name="pool_mix",
    )(u, w_group.astype(BF16), scale.reshape(1, D_MODEL))


DIFF_SLOPE_TERMS = 3
DIFF_AUX_ROWS = 16


def _lane_features(lane, groups):
    out = jnp.zeros(lane.shape, F32)
    for g, v in enumerate(groups):
        for r in range(DIFF_SLOPE_TERMS):
            out = jnp.where(lane == DIFF_SLOPE_TERMS * g + r, v[r] if isinstance(v, list) else v, out)
    return out


def _diff_kernel(slopes_ref, lam_ref, q_ref, k_ref, vt_ref, ng_ref, o_ref,
                 ka_ref, qf_ref, acc_ref, m_ref, *, t, n_tiles, lam_init):
    hg = pl.program_id(1)
    qi = pl.program_id(2)
    hd = 2 * DIFF_HEAD_DIM
    lane = lax.broadcasted_iota(jnp.int32, (t, hd), 1)
    rowf = lax.broadcasted_iota(jnp.int32, (t, hd), 0).astype(F32)
    own = (lane < DIFF_HEAD_DIM, lane >= DIFF_HEAD_DIM)
    heads = range(DIFF_HEAD_BLOCK)
    c_terms = [[slopes_ref[hg * DIFF_HEAD_BLOCK + hb, r] for r in range(DIFF_SLOPE_TERMS)] for hb in heads]
    chains = [(hb, c) for hb in heads for c in range(2)]

    @pl.when(qi == 0)
    def _():
        def build(j, carry):
            start = pl.multiple_of(j * t, t)
            tile_idx = jnp.asarray(j, F32)
            for hb in heads:
                neg = [-cr for cr in c_terms[hb]]
                feat = _lane_features(lane, [rowf, tile_idx, neg, [cr * t for cr in neg]])
                ka_ref[hb, pl.ds(start, t), 0:hd] = k_ref[pl.ds(start, t), hb * hd:(hb + 1) * hd]
                ka_ref[hb, pl.ds(start, t), hd:2 * hd] = feat.astype(BF16)
            return carry

        lax.fori_loop(0, n_tiles, build, 0)
        for hb in heads:
            qf_ref[hb] = _lane_features(lane, [c_terms[hb], [cr * t for cr in c_terms[hb]], rowf, 0.0])

    qa = {}
    tile_lanes = (lane >= 3 * DIFF_SLOPE_TERMS) & (lane < 4 * DIFF_SLOPE_TERMS)
    for hb in heads:
        q = q_ref[:, hb * hd:(hb + 1) * hd]
        feat = jnp.where(tile_lanes, jnp.asarray(qi, F32), qf_ref[hb])
        for c in range(2):
            qa[hb, c] = jnp.concatenate([jnp.where(own[c], q, jnp.zeros_like(q)), feat.astype(BF16)], axis=1)
    acc_ref[...] = jnp.zeros_like(acc_ref)
    m_ref[...] = jnp.full(m_ref.shape, -jnp.inf, F32)

    def tile(kj, masked):
        start = pl.multiple_of(kj * t, t)
        scores = {}
        for hb, c in chains:
            s = _dot_nt(ka_ref[hb, pl.ds(start, t), :], qa[hb, c])
            if masked:
                key = lax.broadcasted_iota(jnp.int32, (t, t), 0)
                qry = lax.broadcasted_iota(jnp.int32, (t, t), 1)
                s = jnp.where(key <= qry, s, -jnp.inf)
            scores[hb, c] = s
        probs = {}
        for hb, c in chains:
            s = scores[hb, c]
            m_prev = m_ref[hb, c]
            m_new = jnp.maximum(m_prev, jnp.max(s, axis=0, keepdims=True))
            alpha = jnp.exp2(m_prev - m_new)
            p = jnp.exp2(s - m_new)
            m_ref[hb, c] = m_new
            probs[hb, c] = (alpha, p.astype(BF16))
        for hb, c in chains:
            alpha, p = probs[hb, c]
            acc_ref[hb, c] = alpha * acc_ref[hb, c] + _dot(vt_ref[hb, kj], p)

    def body(kj, carry):
        tile(kj, masked=False)
        return carry

    lax.fori_loop(0, qi, body, 0)
    tile(qi, masked=True)

    lp = lam_ref[...]
    lam = (jnp.exp(jnp.sum(lp[0:1] * lp[1:2], axis=-1, keepdims=True))
           - jnp.exp(jnp.sum(lp[2:3] * lp[3:4], axis=-1, keepdims=True)) + lam_init)
    for hb in range(DIFF_HEAD_BLOCK):
        a0, a1 = acc_ref[hb, 0], acc_ref[hb, 1]
        o_t = (a0[:hd] * (1.0 / a0[hd:hd + 1]) - lam * (a1[:hd] * (1.0 / a1[hd:hd + 1])))
        o_ref[:, hb * hd:(hb + 1) * hd] = (_rms(o_t.T, ng_ref[...], SUBLN_EPS) * (1.0 - lam_init)).astype(o_ref.dtype)


def _diff_mix(q, k, v_t, lam_p, norm_g, layer_idx):
    B, S, _ = q.shape
    t = ATTN_TILE
    n_tiles = S // t
    hd = 2 * DIFF_HEAD_DIM
    hb = DIFF_HEAD_BLOCK
    lam_init = 0.8 - 0.6 * math.exp(-0.3 * layer_idx)
    slopes = jnp.asarray([2.0 ** (-8.0 * (h + 1) / DIFF_HEADS) for h in range(DIFF_HEADS)], dtype=F32)
    rest = slopes * math.log2(math.e)
    terms = []
    for _ in range(DIFF_SLOPE_TERMS):
        terms.append(rest.astype(BF16).astype(F32))
        rest = rest - terms[-1]
    slopes = jnp.stack(terms, axis=1)
    qspec = pl.BlockSpec((None, t, hb * hd), lambda b, h, i: (b, i, h))
    return pl.pallas_call(
        functools.partial(_diff_kernel, t=t, n_tiles=n_tiles, lam_init=lam_init),
        grid=(B, DIFF_HEADS // hb, n_tiles),
        in_specs=[pl.BlockSpec(memory_space=pltpu.SMEM), _resident(lam_p.shape), qspec,
                  pl.BlockSpec((None, S, hb * hd), lambda b, h, i: (b, 0, h), pipeline_mode=pl.Buffered(1)),
                  pl.BlockSpec((None, hb, n_tiles, hd + DIFF_AUX_ROWS, t), lambda b, h, i: (b, h, 0, 0, 0),
                               pipeline_mode=pl.Buffered(1)),
                  _resident((1, hd))],
        out_specs=qspec,
        out_shape=jax.ShapeDtypeStruct((B, S, D_MODEL), BF16),
        scratch_shapes=[pltpu.VMEM((hb, S, 2 * hd), BF16), pltpu.VMEM((hb, t, hd), F32),
                        pltpu.VMEM((hb, 2, hd + DIFF_AUX_ROWS, t), F32),
                        pltpu.VMEM((hb, 2, 1, t), F32)],
        compiler_params=_params("parallel", "parallel", "arbitrary"),
        name="diff_attn",
    )(slopes, lam_p, q, k, v_t, norm_g.reshape(1, hd))


def _mlstm_kernel(qk_ref, v_ref, og_ref, gc_ref, gr_ref, cw_ref, gbc_ref, gbr_ref, ng_ref, o_ref,
                  ext_ref, c_ref, n_ref, m_ref, *, L):
    c = pl.program_id(1)
    W = 2 * D_MODEL

    @pl.when(c == 0)
    def _():
        ext_ref[0:CONV_HALO, :] = jnp.zeros((CONV_HALO, W), F32)
        c_ref[...] = jnp.zeros_like(c_ref)
        n_ref[...] = jnp.zeros_like(n_ref)
        m_ref[...] = jnp.zeros_like(m_ref)

    ext_ref[CONV_HALO:CONV_HALO + L, :] = qk_ref[...]
    conv = cw_ref[MLSTM_CONV - 1:MLSTM_CONV, :] * qk_ref[...]
    for j in range(1, MLSTM_CONV):
        conv = conv + (cw_ref[MLSTM_CONV - 1 - j:MLSTM_CONV - j, :]
                       * ext_ref[CONV_HALO - j:CONV_HALO - j + L, :])
    ext_ref[0:CONV_HALO, :] = ext_ref[L:L + CONV_HALO, :]
    qk = conv * _sigmoid(conv)

    gc = gc_ref[...] + gbc_ref[...]
    gr = gr_ref[...] + gbr_ref[...]
    b_c = _cumsum_rows(_log_sigmoid(gc))
    b_r = _cumsum_lanes(_log_sigmoid(gr))
    row = lax.broadcasted_iota(jnp.int32, (L, L), 0)
    col = lax.broadcasted_iota(jnp.int32, (L, L), 1)
    causal = col <= row

    for hd in range(MLSTM_HEADS):
        sl = slice(hd * MLSTM_HEAD_DIM, (hd + 1) * MLSTM_HEAD_DIM)
        q32 = qk[:, sl] * (MLSTM_HEAD_DIM ** -0.5)
        k32 = qk[:, D_MODEL + hd * MLSTM_HEAD_DIM:D_MODEL + (hd + 1) * MLSTM_HEAD_DIM]
        vh = v_ref[:, sl]
        fi = MLSTM_HEADS + hd
        bcol, icol = b_c[:, fi:fi + 1], gc[:, hd:hd + 1]
        brow, irow = b_r[fi:fi + 1, :], gr[hd:hd + 1, :]
        b_last = brow[:, L - 1:L]
        m = m_ref[hd:hd + 1, 0:1]
        cmat = c_ref[hd]
        nrow = n_ref[hd]

        dmat = jnp.where(causal, bcol - brow + irow, -jnp.inf)
        inter = bcol + m
        m_t = jnp.maximum(inter, jnp.max(dmat, axis=-1, keepdims=True))
        dec = jnp.exp(inter - m_t)
        qb = q32.astype(BF16)
        k_t = k32.T
        sqk = _dot(qb, k_t.astype(BF16)) * jnp.exp(dmat - m_t)
        num = dec * _dot(qb, cmat.astype(BF16)) + _dot(sqk.astype(BF16), vh)
        den = dec * jnp.sum(q32 * nrow, axis=-1, keepdims=True) + jnp.sum(sqk, axis=-1, keepdims=True)
        hc = num * (1.0 / jnp.maximum(jnp.abs(den), jnp.exp(-m_t)))

        gs_r = b_last - brow + irow
        gs_c = b_last - bcol + icol
        m_new = jnp.maximum(b_last + m, jnp.max(gs_r, axis=-1, keepdims=True))
        carry_dec = jnp.exp(b_last + m - m_new)
        c_ref[hd] = carry_dec * cmat + _dot((k_t * jnp.exp(gs_r - m_new)).astype(BF16), vh)
        n_ref[hd] = carry_dec * nrow + jnp.sum(k32 * jnp.exp(gs_c - m_new), axis=0, keepdims=True)
        m_ref[hd:hd + 1, :] = jnp.broadcast_to(m_new, (1, LANES))

        o_ref[:, sl] = (_rms(hc, ng_ref[:, sl], NORM_EPS) * _sigmoid(og_ref[:, sl])).astype(o_ref.dtype)


def _mlstm_mix(qk, v, og, gates, conv_w, gate_b, norm_g):
    B, S, _ = qk.shape
    L = MLSTM_CHUNK
    H = MLSTM_HEADS
    gates_r = jnp.transpose(gates[..., :2 * H], (0, 2, 1))
    gb = gate_b.reshape(2 * H)
    gb_c = jnp.pad(gb, (0, LANES - 2 * H)).reshape(1, LANES)
    gb_r = gb.reshape(2 * H, 1)
    tok = lambda n: pl.BlockSpec((None, L, n), lambda b, c: (b, c, 0))
    return pl.pallas_call(
        functools.partial(_mlstm_kernel, L=L),
        grid=(B, S // L),
        in_specs=[tok(2 * D_MODEL), tok(D_MODEL), tok(D_MODEL), tok(LANES),
                  pl.BlockSpec((None, 2 * H, L), lambda b, c: (b, 0, c)),
                  _resident(conv_w.shape), _resident((1, LANES)), _resident((2 * H, 1)), _resident((1, D_MODEL))],
        out_specs=tok(D_MODEL),
        out_shape=jax.ShapeDtypeStruct((B, S, D_MODEL), BF16),
        scratch_shapes=[pltpu.VMEM((L + CONV_HALO, 2 * D_MODEL), F32),
                        pltpu.VMEM((H, MLSTM_HEAD_DIM, MLSTM_HEAD_DIM), F32),
                        pltpu.VMEM((H, 1, MLSTM_HEAD_DIM), F32),
                        pltpu.VMEM((8, LANES), F32)],
        compiler_params=_params("parallel", "arbitrary"),
        name="mlstm",
    )(qk, v, og, gates, gates_r, conv_w, gb_c, gb_r, norm_g.reshape(1, D_MODEL))


def _gla_kernel(q_ref, k_ref, v_ref, g_ref, z_ref, w2_ref, gb_ref, ng_ref, o_ref, st_ref, *, L):
    c = pl.program_id(1)

    @pl.when(c == 0)
    def _():
        st_ref[...] = jnp.zeros_like(st_ref)

    log_a = _log_sigmoid(_dot(z_ref[...], w2_ref[...]) + gb_ref[...]) * (1.0 / GLA_TAU)
    b_all = _cumsum_rows(log_a) * LOG2E
    sub = GLA_SUBCHUNK
    row = lax.broadcasted_iota(jnp.int32, (sub, L), 0)
    col = lax.broadcasted_iota(jnp.int32, (sub, L), 1)
    ones = jnp.ones((GLA_KEY_DIM, LANES), BF16)

    for hd in range(GLA_HEADS):
        ks = slice(hd * GLA_KEY_DIM, (hd + 1) * GLA_KEY_DIM)
        vs = slice(hd * GLA_VALUE_DIM, (hd + 1) * GLA_VALUE_DIM)
        b = b_all[:, ks]
        vh = v_ref[:, vs]
        st = st_ref[hd]
        q = q_ref[:, ks] * (GLA_KEY_DIM ** -0.5)
        k = k_ref[:, ks]
        b_t = b.T
        k_t = k.T
        b_last = b_t[:, L - 1:L]
        blocks = []
        for i in range(L // sub):
            r0 = i * sub
            q_i, k_i, b_i = q[r0:r0 + sub], k[r0:r0 + sub], b[r0:r0 + sub]
            terms, offs = [], []
            for s in range(sub):
                t0 = s // SUBLANES * SUBLANES
                offs.append(sum(x.shape[0] for x in terms))
                terms.append(q_i[t0:] * k_i[s:s + 1] * jnp.exp2(b_i[t0:] - b_i[s:s + 1]))
            sums = _dot(jnp.concatenate(terms, axis=0).astype(BF16), ones)
            groups = []
            for t0 in range(0, sub, SUBLANES):
                a_g = jnp.zeros((SUBLANES, L), F32)
                for s in range(min(sub, t0 + SUBLANES)):
                    lo = offs[s] + t0 - s // SUBLANES * SUBLANES
                    a_g = jnp.where(col[:SUBLANES] == r0 + s, sums[lo:lo + SUBLANES], a_g)
                groups.append(a_g)
            a_i = jnp.where(col <= r0 + row, jnp.concatenate(groups, axis=0), 0.0)
            if i > 0:
                ref_row, ref_col = b[r0 - 1:r0], b_t[:, r0 - 1:r0]
                q_s = (q_i * jnp.exp2(b_i - ref_row)).astype(BF16)
                k_s = (k_t * jnp.exp2(ref_col - b_t)).astype(BF16)
                a_i = jnp.where(col < r0, _dot(q_s, k_s), a_i)
            blocks.append(a_i)
        amat = jnp.concatenate(blocks, axis=0)
        o = _dot((q * jnp.exp2(b)).astype(BF16), st.astype(BF16)) + _dot(amat.astype(BF16), vh)
        st_ref[hd] = jnp.exp2(b_last) * st + _dot((k_t * jnp.exp2(b_last - b_t)).astype(BF16), vh)
        gate = g_ref[:, vs]
        o_ref[:, vs] = (_rms(o, ng_ref[:, vs], NORM_EPS) * (gate * _sigmoid(gate))).astype(o_ref.dtype)


def _gla_mix(q, k, v, g, z, gate_w2, gate_b, norm_g):
    B, S, _ = q.shape
    L = GLA_CHUNK
    kw =GLA_HEADS * GLA_KEY_DIM
    w2 = jnp.pad(gate_w2, ((0, LANES - GLA_GATE_RANK), (0, 0))).astype(BF16)
    tok = lambda n: pl.BlockSpec((None, L, n), lambda b, c: (b, c, 0))
    return pl.pallas_call(
        functools.partial(_gla_kernel, L=L),
        grid=(B, S // L),
        in_specs=[tok(kw), tok(kw), tok(D_MODEL), tok(D_MODEL), tok(LANES),
                  _resident(w2.shape), _resident((1, kw)), _resident((1, D_MODEL))],
        out_specs=tok(D_MODEL),
        out_shape=jax.ShapeDtypeStruct((B, S, D_MODEL), BF16),
        scratch_shapes=[pltpu.VMEM((GLA_HEADS, GLA_KEY_DIM, GLA_VALUE_DIM), F32)],
        compiler_params=_params("parallel", "arbitrary"),
        name="gla",
    )(q, k, v, g, z, w2, gate_b.reshape(1, kw), norm_g.reshape(1, D_MODEL))


def kernel(x, mem, norm_g, ffn_w_up, ffn_w_down, mem_norm_g, mem_w_kv, pool_w_in, pool_w_group, pool_scale, pool_w_out, diff_w_in, diff_lambda, diff_norm_g, diff_w_out, mlstm_w_in, mlstm_conv_w, mlstm_gate_b, mlstm_norm_g, mlstm_w_out, gla_w_in, gla_gate_w2, gla_gate_b, gla_norm_g, gla_w_out, final_norm_g):
    B, S, D = x.shape
    M = mem.shape[1]
    T = B * S
    depth = norm_g.shape[0]
    n_mixers = 4
    mem2 = mem.reshape(B * M, D)
    x2 = x.reshape(T, D)
    norm_g4 = norm_g.reshape(depth, 3, 1, D)
    w_up, w_down = ffn_w_up[0, 0].astype(BF16), ffn_w_down[0, 0].astype(BF16)

    def seq(a):
        return a.reshape(B, S, a.shape[-1])

    for i in range(depth):
        kind, j = i % n_mixers, i // n_mixers
        mem_k, mem_v = _norm_proj(mem2, mem_norm_g, [(mem_w_kv[i][:, :X_WIDTH], BF16), (mem_w_kv[i][:, X_WIDTH:], BF16)])
        mem_k = mem_k.reshape(B, M, X_WIDTH)
        mem_v = mem_v.reshape(B, M, X_WIDTH)
        x2, w_up, w_down = _ffn(x2, _ffn_weights(norm_g4, w_up, w_down, i, 0), (ffn_w_up, ffn_w_down, i, 1))
        if kind == 0:
            w = pool_w_in[j]
            u, xq = _norm_proj(x2, norm_g[i, 1], [(w[:, :D], F32), (w[:, D:], BF16)])
            mix = _pool_mix(seq(u), pool_w_group[j], pool_scale[j])
            w_out = pool_w_out[j]
        elif kind == 1:
            w = diff_w_in[j]
            q_scale = DIFF_HEAD_DIM ** -0.5 * math.log2(math.e)
            q, k, v_t, xq = _norm_proj(x2, norm_g[i, 1], [
                (w[:, :D] * q_scale, BF16), (w[:, D:2 * D], BF16), (w[:, 2 * D:3 * D], BF16, "values_t"),
                (w[:, 3 * D:], BF16)], seq_len=S)
            mix = _diff_mix(seq(q), seq(k), v_t, diff_lambda[j], diff_norm_g[j], i)
            w_out = diff_w_out[j]
        elif kind == 2:
            w = mlstm_w_in[j]
            ng = 2 * MLSTM_HEADS
            qk, v, og, gates, xq = _norm_proj(x2, norm_g[i, 1], [
                (w[:, :2 * D], F32), (w[:, 2 * D:3 * D], BF16), (w[:, 3 * D:4 * D], F32),
                (w[:, 4 * D:4 * D + ng], F32), (w[:, 4 * D + ng:], BF16)])
            mix = _mlstm_mix(seq(qk), seq(v), seq(og), seq(gates), mlstm_conv_w[j], mlstm_gate_b[j], mlstm_norm_g[j])
            w_out = mlstm_w_out[j]
        else:
            w = gla_w_in[j]
            kw = GLA_HEADS * GLA_KEY_DIM
            o0 = 2 * kw + 2 * D
            q, k, v, g, z, xq = _norm_proj(x2, norm_g[i, 1], [
                (w[:, :kw], F32), (w[:, kw:2 * kw], F32), (w[:, 2 * kw:2 * kw + D], BF16),
                (w[:, 2 * kw + D:o0], F32), (w[:, o0:o0 + GLA_GATE_RANK], BF16), (w[:, o0 + GLA_GATE_RANK:], BF16)])
            mix = _gla_mix(seq(q), seq(k), seq(v), seq(g), seq(z), gla_gate_w2[j], gla_gate_b[j], gla_norm_g[j])
            w_out = gla_w_out[j]
        last = i == depth - 1
        x3, w_up, w_down = _attn_out_ffn(seq(x2), mix, seq(xq), mem_k, mem_v, w_out,
                                         _ffn_weights(norm_g4, w_up, w_down, i, 2),
                                         final_g=final_norm_g if last else None,
                                         next_cast=None if last else (ffn_w_up, ffn_w_down, i + 1, 0))
        x2 = x3.reshape(T, D)
    return x2.reshape(B, S, D)
```

```python
import functools
import math

import jax
import jax.numpy as jnp
from jax import lax
from jax.experimental import pallas as pl
from jax.experimental.pallas import tpu as pltpu

F32 = jnp.float32
BF16 = jnp.bfloat16

D_MODEL = 1024
D_FF = 2816
X_HEADS = 4
X_HEAD_DIM = 128
X_WIDTH = X_HEADS * X_HEAD_DIM
POOL_WINDOWS = (2, 4, 8, 16)
POOL_GROUP_DIM = D_MODEL // len(POOL_WINDOWS)
POOL_HALO = 16
DIFF_HEAD_DIM = 64
DIFF_HEADS = 8
DIFF_HEAD_BLOCK = 8
MLSTM_HEADS = 4
MLSTM_HEAD_DIM = 256
MLSTM_CONV = 4
CONV_HALO = 8
GLA_HEADS = 4
GLA_KEY_DIM = 128
GLA_VALUE_DIM = 256
GLA_GATE_RANK = 16
GLA_TAU = 16.0
GLA_SUBCHUNK = 16
LOG2E = math.log2(math.e)
NORM_EPS = 1e-6
SUBLN_EPS = 1e-5
LANES = 128
SUBLANES = 8

VMEM_LIMIT = 56 * 1024 * 1024
TOKEN_TILE = 512
FF_CHUNKS = (0, 1536, D_FF)
ATTN_TILE = 256
MLSTM_CHUNK = 256
GLA_CHUNK = 128


def _params(*sem):
    return pltpu.CompilerParams(dimension_semantics=sem, vmem_limit_bytes=VMEM_LIMIT)


def _resident(shape):
    nd = len(shape)
    return pl.BlockSpec(shape, lambda *_: (0,) * nd, pipeline_mode=pl.Buffered(1))


def _rms(x, g, eps):
    return x * lax.rsqrt(jnp.mean(x * x, axis=-1, keepdims=True) + eps) * g


def _sigmoid(x):
    return 1.0 / (1.0 + jnp.exp(-x))


def _log_sigmoid(x):
    return jnp.minimum(x, 0.0) - jnp.log(1.0 + jnp.exp(-jnp.abs(x)))


def _dot(a, b):
    return jnp.dot(a, b, preferred_element_type=F32)


def _dot_nt(a, b):
    return lax.dot_general(a, b, (((1,), (1,)), ((), ())), preferred_element_type=F32)


def _split3(x):
    hi = x.astype(BF16)
    r1 = x - hi.astype(F32)
    mid = r1.astype(BF16)
    lo = (r1 - mid.astype(F32)).astype(BF16)
    return hi, mid, lo


def _tri(n, upper):
    r = lax.broadcasted_iota(jnp.int32, (n, n), 0)
    c = lax.broadcasted_iota(jnp.int32, (n, n), 1)
    keep = (r <= c) if upper else (c <= r)
    return jnp.where(keep, 1.0, 0.0).astype(BF16)


def _cumsum_rows(x):
    tri = _tri(x.shape[0], upper=False)
    hi, mid, lo = _split3(x)
    return _dot(tri, hi) + _dot(tri, mid) + _dot(tri, lo)


def _cumsum_lanes(x):
    tri = _tri(x.shape[1], upper=True)
    hi, mid, lo = _split3(x)
    return _dot(hi, tri) + _dot(mid, tri) + _dot(lo, tri)


def _ffn_half_step(x, g_ref, wg_ref, wu_ref, wd_ref):
    h = _rms(x, g_ref[...], NORM_EPS).astype(BF16)
    acc = jnp.zeros_like(x)
    for lo, hi in zip(FF_CHUNKS[:-1], FF_CHUNKS[1:]):
        sl = slice(lo, hi)
        gate = _dot(h, wg_ref[:, sl])
        up = _dot(h, wu_ref[:, sl])
        act = (gate * _sigmoid(gate) * up).astype(BF16)
        acc = acc + _dot(act, wd_ref[sl, :])
    return x + 0.5 * acc


def _stacked(shape, lead, tail=None):
    tail = tail or (0,) * len(shape)
    return pl.BlockSpec((None,) * len(lead) + shape, lambda *_: lead + tail, pipeline_mode=pl.Buffered(1))


def _ffn_weights(norm_g, w_up, w_down, layer, norm_slot):
    args = [norm_g, w_up, w_up, w_down]
    specs = [_stacked((1, D_MODEL), (layer, norm_slot)), _stacked((D_MODEL, D_FF), (), (0, 0)),
             _stacked((D_MODEL, D_FF), (), (0, 1)), _stacked((D_FF, D_MODEL), ())]
    return args, specs


def _next_weights_cast(w_up_all, w_down_all, layer, slot, step_of, n_steps):
    args, in_specs, out_specs, out_shapes = [w_up_all, w_down_all], [], [], []
    for w in args:
        n_rows, n_cols = w.shape[2:]
        n_slabs = max(s for s in range(1, n_steps + 1) if n_rows % (16 * s) == 0)
        slab = lambda *idx, n_slabs=n_slabs: jnp.minimum(step_of(*idx), n_slabs - 1)
        in_specs.append(pl.BlockSpec((None, None, n_rows // n_slabs, n_cols),
                                     lambda *idx, slab=slab: (layer, slot, slab(*idx), 0)))
        out_specs.append(pl.BlockSpec((n_rows // n_slabs, n_cols), lambda *idx, slab=slab: (slab(*idx), 0)))
        out_shapes.append(jax.ShapeDtypeStruct((n_rows, n_cols), BF16))
    return args, in_specs, out_specs, out_shapes


def _ffn_kernel(x_ref, g_ref, wg_ref, wu_ref, wd_ref, nu_ref, nd_ref, o_ref, nu_out_ref, nd_out_ref):
    o_ref[...] = _ffn_half_step(x_ref[...], g_ref, wg_ref, wu_ref, wd_ref)
    nu_out_ref[...] = nu_ref[...].astype(BF16)
    nd_out_ref[...] = nd_ref[...].astype(BF16)


def _ffn(x, ffn_weights, next_cast):
    T = x.shape[0]
    w_args, w_specs = ffn_weights
    n_steps = T // TOKEN_TILE
    c_args, c_in, c_out, c_shapes = _next_weights_cast(*next_cast, step_of=lambda i: i, n_steps=n_steps)
    tok = pl.BlockSpec((TOKEN_TILE, D_MODEL), lambda i: (i, 0))
    y, w_up, w_down = pl.pallas_call(
        _ffn_kernel,
        grid=(n_steps,),
        in_specs=[tok] + w_specs + c_in,
        out_specs=[tok] + c_out,
        out_shape=[jax.ShapeDtypeStruct((T, D_MODEL), F32)] + c_shapes,
        compiler_params=_params("parallel"),
        name="ffn",
    )(x, *w_args, *c_args)
    return y, w_up, w_down


def _norm_proj_kernel(x_ref, g_ref, *refs, n_out):
    h = _rms(x_ref[...], g_ref[...], NORM_EPS).astype(BF16)
    for w_ref, o_ref in zip(refs[:n_out], refs[n_out:]):
        y = _dot(h, w_ref[...])
        if len(o_ref.shape) == 2:
            o_ref[...] = y.astype(o_ref.dtype)
            continue
        n_heads, n_sub, rows, t = o_ref.shape
        hd = rows - DIFF_AUX_ROWS
        aux = jnp.where(lax.broadcasted_iota(jnp.int32, (DIFF_AUX_ROWS, t), 0) == 0, 1.0, 0.0).astype(o_ref.dtype)
        for head in range(n_heads):
            for s in range(n_sub):
                o_ref[head, s, 0:hd, :] = y[s * t:(s + 1) * t, head * hd:(head + 1) * hd].T.astype(o_ref.dtype)
                o_ref[head, s, hd:rows, :] = aux


def _norm_proj(x, g, pieces, seq_len=None):
    T = x.shape[0]
    ws, out_specs, out_shapes = [], [], []
    for w, dt, *layout in pieces:
        n = w.shape[1]
        n_pad = -(-n // LANES) * LANES
        if n_pad != n:
            w = jnp.pad(w, ((0, 0), (0, n_pad - n)))
        ws.append(w.astype(BF16))
        if layout:
            hd, t, per_seq = 2 * DIFF_HEAD_DIM, ATTN_TILE, seq_len // TOKEN_TILE
            blk = (None, n // hd, TOKEN_TILE // t, hd + DIFF_AUX_ROWS, t)
            out_specs.append(pl.BlockSpec(blk, lambda i: (i // per_seq, 0, i % per_seq, 0, 0)))
            out_shapes.append(jax.ShapeDtypeStruct((T // seq_len, n // hd, seq_len // t) + blk[3:], dt))
            continue
        out_specs.append(pl.BlockSpec((TOKEN_TILE, n_pad), lambda i: (i, 0)))
        out_shapes.append(jax.ShapeDtypeStruct((T, n_pad), dt))
    in_specs = [pl.BlockSpec((TOKEN_TILE, D_MODEL), lambda i: (i, 0)), _resident((1, D_MODEL))]
    in_specs += [_resident(w.shape) for w in ws]
    return pl.pallas_call(
        functools.partial(_norm_proj_kernel, n_out=len(ws)),
        grid=(T // TOKEN_TILE,),
        in_specs=in_specs,
        out_specs=out_specs,
        out_shape=out_shapes,
        compiler_params=_params("parallel"),
        name="norm_proj",
    )(x, g.reshape(1, D_MODEL), *ws)


def _attn_out_kernel(x_ref, mix_ref, xq_ref, mk_ref, mv_ref, w1_ref, w2_ref, g_ref, wg_ref, wu_ref, wd_ref,
                     *rest, final):
    xq = xq_ref[...]
    mk = mk_ref[...]
    mv = mv_ref[...]
    outs = []
    for h in range(X_HEADS):
        sl = slice(h * X_HEAD_DIM, (h + 1) * X_HEAD_DIM)
        s = _dot_nt(xq[:, sl], mk[:, sl])
        e = jnp.exp2(s - jnp.max(s, axis=-1, keepdims=True))
        o = _dot(e.astype(BF16), mv[:, sl]) * (1.0 / jnp.sum(e, axis=-1, keepdims=True))
        outs.append(o.astype(BF16))
    xo = jnp.concatenate(outs, axis=-1)
    x = x_ref[...] + _dot(mix_ref[...], w1_ref[...]) + _dot(xo, w2_ref[...])
    y = _ffn_half_step(x, g_ref, wg_ref, wu_ref, wd_ref)
    if final:
        fg_ref, o_ref = rest
        o_ref[...] = _rms(y, fg_ref[...], NORM_EPS)
    else:
        nu_ref, nd_ref, o_ref, nu_out_ref, nd_out_ref = rest
        o_ref[...] = y
        nu_out_ref[...] = nu_ref[...].astype(BF16)
        nd_out_ref[...] = nd_ref[...].astype(BF16)


def _attn_out_ffn(x, mix, xq, mem_k, mem_v, w_out, ffn_weights, final_g=None, next_cast=None):
    B, S, _ = x.shape
    M = mem_k.shape[1]
    w1 = w_out[:D_MODEL].astype(BF16)
    w2 = w_out[D_MODEL:].astype(BF16)
    w_args, w_specs = ffn_weights
    tok = lambda n: pl.BlockSpec((None, TOKEN_TILE, n), lambda b, i: (b, i, 0))
    mem = pl.BlockSpec((None, M, X_WIDTH), lambda b, i: (b, 0, 0))
    out_specs, out_shapes = [tok(D_MODEL)], [jax.ShapeDtypeStruct((B, S, D_MODEL), F32)]
    per_seq = S // TOKEN_TILE
    if final_g is not None:
        w_args = w_args + [final_g.reshape(1, D_MODEL)]
        w_specs = w_specs + [_resident((1, D_MODEL))]
    else:
        c_args, c_in, c_out, c_shapes = _next_weights_cast(*next_cast, step_of=lambda b, i: b * per_seq + i,
                                                           n_steps=B * per_seq)
        w_args, w_specs = w_args + c_args, w_specs + c_in
        out_specs, out_shapes = out_specs + c_out, out_shapes + c_shapes
    outs = pl.pallas_call(
        functools.partial(_attn_out_kernel, final=final_g is not None),
        grid=(B, per_seq),
        in_specs=[tok(D_MODEL), tok(D_MODEL), tok(X_WIDTH), mem, mem, _resident(w1.shape), _resident(w2.shape)]
        + w_specs,
        out_specs=out_specs,
        out_shape=out_shapes,
        compiler_params=_params("parallel", "parallel"),
        name="attn_out_ffn",
    )(x, mix, xq, mem_k, mem_v, w1, w2, *w_args)
    if final_g is not None:
        return outs[0], None, None
    return outs[0], outs[1], outs[2]


def _pool_kernel(u_ref, wg_ref, sc_ref, o_ref, ext_ref, a_ref, b_ref, *, ts):
    j = pl.program_id(1)
    H = POOL_HALO

    @pl.when(j == 0)
    def _():
        ext_ref[0:2 * H, :] = jnp.zeros((2 * H, D_MODEL), F32)
        a_ref[0:H, :] = jnp.zeros((H, POOL_GROUP_DIM), F32)
        b_ref[0:H, :] = jnp.zeros((H, POOL_GROUP_DIM), F32)

    ext_ref[2 * H:2 * H + ts, :] = u_ref[...]
    pos = j * ts + lax.broadcasted_iota(jnp.int32, (ts, 1), 0)
    n = ts + H
    for g, w in enumerate(POOL_WINDOWS):
        sl = slice(g * POOL_GROUP_DIM, (g + 1) * POOL_GROUP_DIM)
        src, cols, d, bufs = ext_ref, sl, 1, [a_ref, b_ref]
        while d < w:
            dst = bufs.pop(0)
            dst[H:H + n, :] = src[H:H + n, cols] + src[H - d:H - d + n, cols]
            bufs.append(dst)
            src, cols, d = dst, slice(None), 2 * d
        u = u_ref[:, sl]
        inv = 1.0 / jnp.minimum(pos + 1, w).astype(F32)
        pooled = (src[2 * H:2 * H + ts, cols] * inv - u).astype(BF16)
        o_ref[:, sl] = (_dot(pooled, wg_ref[g]) * sc_ref[:, sl]).astype(o_ref.dtype)
    ext_ref[H:2 * H, :] = ext_ref[ts + H:ts + 2 * H, :]


def _pool_mix(u, w_group, scale):
    B, S, _ = u.shape
    ts = TOKEN_TILE
    tok = pl.BlockSpec((None, ts, D_MODEL), lambda b, j: (b, j, 0))
    return pl.pallas_call(
        functools.partial(_pool_kernel, ts=ts),
        grid=(B, S // ts),
        in_specs=[tok, _resident(w_group.shape), _resident((1, D_MODEL))],
        out_specs=tok,
        out_shape=jax.ShapeDtypeStruct((B, S, D_MODEL), BF16),
        scratch_shapes=[pltpu.VMEM((ts + 2 * POOL_HALO, D_MODEL), F32),
                        pltpu.VMEM((ts + 2 * POOL_HALO, POOL_GROUP_DIM), F32),
                        pltpu.VMEM((ts + 2 * POOL_HALO, POOL_GROUP_DIM), F32)],
        compiler_params=_params("parallel", "arbitrary"),
        name="pool_mix",
    )(u, w_group.astype(BF16), scale.reshape(1, D_MODEL))


DIFF_SLOPE_TERMS = 3
DIFF_AUX_ROWS = 16


def _lane_features(lane, groups):
    out = jnp.zeros(lane.shape, F32)
    for g, v in enumerate(groups):
        for r in range(DIFF_SLOPE_TERMS):
            out = jnp.where(lane == DIFF_SLOPE_TERMS * g + r, v[r] if isinstance(v, list) else v, out)
    return out


def _diff_kernel(slopes_ref, lam_ref, q_ref, k_ref, vt_ref, ng_ref, o_ref,
                 ka_ref, qf_ref, acc_ref, m_ref, *, t, n_tiles, lam_init):
    hg = pl.program_id(1)
    qi = pl.program_id(2)
    hd = 2 * DIFF_HEAD_DIM
    lane = lax.broadcasted_iota(jnp.int32, (t, hd), 1)
    rowf = lax.broadcasted_iota(jnp.int32, (t, hd), 0).astype(F32)
    own = (lane < DIFF_HEAD_DIM, lane >= DIFF_HEAD_DIM)
    heads = range(DIFF_HEAD_BLOCK)
    c_terms = [[slopes_ref[hg * DIFF_HEAD_BLOCK + hb, r] for r in range(DIFF_SLOPE_TERMS)] for hb in heads]
    chains = [(hb, c) for hb in heads for c in range(2)]

    @pl.when(qi == 0)
    def _():
        def build(j, carry):
            start = pl.multiple_of(j * t, t)
            tile_idx = jnp.asarray(j, F32)
            for hb in heads:
                neg = [-cr for cr in c_terms[hb]]
                feat = _lane_features(lane, [rowf, tile_idx, neg, [cr * t for cr in neg]])
                ka_ref[hb, pl.ds(start, t), 0:hd] = k_ref[pl.ds(start, t), hb * hd:(hb + 1) * hd]
                ka_ref[hb, pl.ds(start, t), hd:2 * hd] = feat.astype(BF16)
            return carry

        lax.fori_loop(0, n_tiles, build, 0)
        for hb in heads:
            qf_ref[hb] = _lane_features(lane, [c_terms[hb], [cr * t for cr in c_terms[hb]], rowf, 0.0])

    qa = {}
    tile_lanes = (lane >= 3 * DIFF_SLOPE_TERMS) & (lane < 4 * DIFF_SLOPE_TERMS)
    for hb in heads:
        q = q_ref[:, hb * hd:(hb + 1) * hd]
        feat = jnp.where(tile_lanes, jnp.asarray(qi, F32), qf_ref[hb])
        for c in range(2):
            qa[hb, c] = jnp.concatenate([jnp.where(own[c], q, jnp.zeros_like(q)), feat.astype(BF16)], axis=1)
    acc_ref[...] = jnp.zeros_like(acc_ref)
    m_ref[...] = jnp.full(m_ref.shape, -jnp.inf, F32)

    def tile(kj, masked):
        start = pl.multiple_of(kj * t, t)
        scores = {}
        for hb, c in chains:
            s = _dot_nt(ka_ref[hb, pl.ds(start, t), :], qa[hb, c])
            if masked:
                key = lax.broadcasted_iota(jnp.int32, (t, t), 0)
                qry = lax.broadcasted_iota(jnp.int32, (t, t), 1)
                s = jnp.where(key <= qry, s, -jnp.inf)
            scores[hb, c] = s
        probs = {}
        for hb, c in chains:
            s = scores[hb, c]
            m_prev = m_ref[hb, c]
            m_new = jnp.maximum(m_prev, jnp.max(s, axis=0, keepdims=True))
            alpha = jnp.exp2(m_prev - m_new)
            p = jnp.exp2(s - m_new)
            m_ref[hb, c] = m_new
            probs[hb, c] = (alpha, p.astype(BF16))
        for hb, c in chains:
            alpha, p = probs[hb, c]
            acc_ref[hb, c] = alpha * acc_ref[hb, c] + _dot(vt_ref[hb, kj], p)

    def body(kj, carry):
        tile(kj, masked=False)
        return carry

    lax.fori_loop(0, qi, body, 0)
    tile(qi, masked=True)

    lp = lam_ref[...]
    lam = (jnp.exp(jnp.sum(lp[0:1] * lp[1:2], axis=-1, keepdims=True))
           - jnp.exp(jnp.sum(lp[2:3] * lp[3:4], axis=-1, keepdims=True)) + lam_init)
    for hb in range(DIFF_HEAD_BLOCK):
        a0, a1 = acc_ref[hb, 0], acc_ref[hb, 1]
        o_t = (a0[:hd] * (1.0 / a0[hd:hd + 1]) - lam * (a1[:hd] * (1.0 / a1[hd:hd + 1])))
        o_ref[:, hb * hd:(hb + 1) * hd] = (_rms(o_t.T, ng_ref[...], SUBLN_EPS) * (1.0 - lam_init)).astype(o_ref.dtype)


def _diff_mix(q, k, v_t, lam_p, norm_g, layer_idx):
    B, S, _ = q.shape
    t = ATTN_TILE
    n_tiles = S // t
    hd = 2 * DIFF_HEAD_DIM
    hb = DIFF_HEAD_BLOCK
    lam_init = 0.8 - 0.6 * math.exp(-0.3 * layer_idx)
    slopes = jnp.asarray([2.0 ** (-8.0 * (h + 1) / DIFF_HEADS) for h in range(DIFF_HEADS)], dtype=F32)
    rest = slopes * math.log2(math.e)
    terms = []
    for _ in range(DIFF_SLOPE_TERMS):
        terms.append(rest.astype(BF16).astype(F32))
        rest = rest - terms[-1]
    slopes = jnp.stack(terms, axis=1)
    qspec = pl.BlockSpec((None, t, hb * hd), lambda b, h, i: (b, i, h))
    return pl.pallas_call(
        functools.partial(_diff_kernel, t=t, n_tiles=n_tiles, lam_init=lam_init),
        grid=(B, DIFF_HEADS // hb, n_tiles),
        in_specs=[pl.BlockSpec(memory_space=pltpu.SMEM), _resident(lam_p.shape), qspec,
                  pl.BlockSpec((None, S, hb * hd), lambda b, h, i: (b, 0, h), pipeline_mode=pl.Buffered(1)),
                  pl.BlockSpec((None, hb, n_tiles, hd + DIFF_AUX_ROWS, t), lambda b, h, i: (b, h, 0, 0, 0),
                               pipeline_mode=pl.Buffered(1)),
                  _resident((1, hd))],
        out_specs=qspec,
        out_shape=jax.ShapeDtypeStruct((B, S, D_MODEL), BF16),
        scratch_shapes=[pltpu.VMEM((hb, S, 2 * hd), BF16), pltpu.VMEM((hb, t, hd), F32),
                        pltpu.VMEM((hb, 2, hd + DIFF_AUX_ROWS, t), F32),
                        pltpu.VMEM((hb, 2, 1, t), F32)],
        compiler_params=_params("parallel", "parallel", "arbitrary"),
        name="diff_attn",
    )(slopes, lam_p, q, k, v_t, norm_g.reshape(1, hd))


def _mlstm_kernel(qk_ref, v_ref, og_ref, gc_ref, gr_ref, cw_ref, gbc_ref, gbr_ref, ng_ref, o_ref,
                  ext_ref, c_ref, n_ref, m_ref, *, L):
    c = pl.program_id(1)
    W = 2 * D_MODEL

    @pl.when(c == 0)
    def _():
        ext_ref[0:CONV_HALO, :] = jnp.zeros((CONV_HALO, W), F32)
        c_ref[...] = jnp.zeros_like(c_ref)
        n_ref[...] = jnp.zeros_like(n_ref)
        m_ref[...] = jnp.zeros_like(m_ref)

    ext_ref[CONV_HALO:CONV_HALO + L, :] = qk_ref[...]
    conv = cw_ref[MLSTM_CONV - 1:MLSTM_CONV, :] * qk_ref[...]
    for j in range(1, MLSTM_CONV):
        conv = conv + (cw_ref[MLSTM_CONV - 1 - j:MLSTM_CONV - j, :]
                       * ext_ref[CONV_HALO - j:CONV_HALO - j + L, :])
    ext_ref[0:CONV_HALO, :] = ext_ref[L:L + CONV_HALO, :]
    qk = conv * _sigmoid(conv)

    gc = gc_ref[...] + gbc_ref[...]
    gr = gr_ref[...] + gbr_ref[...]
    b_c = _cumsum_rows(_log_sigmoid(gc))
    b_r = _cumsum_lanes(_log_sigmoid(gr))
    row = lax.broadcasted_iota(jnp.int32, (L, L), 0)
    col = lax.broadcasted_iota(jnp.int32, (L, L), 1)
    causal = col <= row

    for hd in range(MLSTM_HEADS):
        sl = slice(hd * MLSTM_HEAD_DIM, (hd + 1) * MLSTM_HEAD_DIM)
        q32 = qk[:, sl] * (MLSTM_HEAD_DIM ** -0.5)
        k32 = qk[:, D_MODEL + hd * MLSTM_HEAD_DIM:D_MODEL + (hd + 1) * MLSTM_HEAD_DIM]
        vh = v_ref[:, sl]
        fi = MLSTM_HEADS + hd
        bcol, icol = b_c[:, fi:fi + 1], gc[:, hd:hd + 1]
        brow, irow = b_r[fi:fi + 1, :], gr[hd:hd + 1, :]
        b_last = brow[:, L - 1:L]
        m = m_ref[hd:hd + 1, 0:1]
        cmat = c_ref[hd]
        nrow = n_ref[hd]

        dmat = jnp.where(causal, bcol - brow + irow, -jnp.inf)
        inter = bcol + m
        m_t = jnp.maximum(inter, jnp.max(dmat, axis=-1, keepdims=True))
        dec = jnp.exp(inter - m_t)
        qb = q32.astype(BF16)
        k_t = k32.T
        sqk = _dot(qb, k_t.astype(BF16)) * jnp.exp(dmat - m_t)
        num = dec * _dot(qb, cmat.astype(BF16)) + _dot(sqk.astype(BF16), vh)
        den = dec * jnp.sum(q32 * nrow, axis=-1, keepdims=True) + jnp.sum(sqk, axis=-1, keepdims=True)
        hc = num * (1.0 / jnp.maximum(jnp.abs(den), jnp.exp(-m_t)))

        gs_r = b_last - brow + irow
        gs_c = b_last - bcol + icol
        m_new = jnp.maximum(b_last + m, jnp.max(gs_r, axis=-1, keepdims=True))
        carry_dec = jnp.exp(b_last + m - m_new)
        c_ref[hd] = carry_dec * cmat + _dot((k_t * jnp.exp(gs_r - m_new)).astype(BF16), vh)
        n_ref[hd] = carry_dec * nrow + jnp.sum(k32 * jnp.exp(gs_c - m_new), axis=0, keepdims=True)
        m_ref[hd:hd + 1, :] = jnp.broadcast_to(m_new, (1, LANES))

        o_ref[:, sl] = (_rms(hc, ng_ref[:, sl], NORM_EPS) * _sigmoid(og_ref[:, sl])).astype(o_ref.dtype)


def _mlstm_mix(qk, v, og, gates, conv_w, gate_b, norm_g):
    B, S, _ = qk.shape
    L = MLSTM_CHUNK
    H = MLSTM_HEADS
    gates_r = jnp.transpose(gates[..., :2 * H], (0, 2, 1))
    gb = gate_b.reshape(2 * H)
    gb_c = jnp.pad(gb, (0, LANES - 2 * H)).reshape(1, LANES)
    gb_r = gb.reshape(2 * H, 1)
    tok = lambda n: pl.BlockSpec((None, L, n), lambda b, c: (b, c, 0))
    return pl.pallas_call(
        functools.partial(_mlstm_kernel, L=L),
        grid=(B, S // L),
        in_specs=[tok(2 * D_MODEL), tok(D_MODEL), tok(D_MODEL), tok(LANES),
                  pl.BlockSpec((None, 2 * H, L), lambda b, c: (b, 0, c)),
                  _resident(conv_w.shape), _resident((1, LANES)), _resident((2 * H, 1)), _resident((1, D_MODEL))],
        out_specs=tok(D_MODEL),
        out_shape=jax.ShapeDtypeStruct((B, S, D_MODEL), BF16),
        scratch_shapes=[pltpu.VMEM((L + CONV_HALO, 2 * D_MODEL), F32),
                        pltpu.VMEM((H, MLSTM_HEAD_DIM, MLSTM_HEAD_DIM), F32),
                        pltpu.VMEM((H, 1, MLSTM_HEAD_DIM), F32),
                        pltpu.VMEM((8, LANES), F32)],
        compiler_params=_params("parallel", "arbitrary"),
        name="mlstm",
    )(qk, v, og, gates, gates_r, conv_w, gb_c, gb_r, norm_g.reshape(1, D_MODEL))


def _gla_kernel(q_ref, k_ref, v_ref, g_ref, z_ref, w2_ref, gb_ref, ng_ref, o_ref, st_ref, *, L):
    c = pl.program_id(1)

    @pl.when(c == 0)
    def _():
        st_ref[...] = jnp.zeros_like(st_ref)

    log_a = _log_sigmoid(_dot(z_ref[...], w2_ref[...]) + gb_ref[...]) * (1.0 / GLA_TAU)
    b_all = _cumsum_rows(log_a) * LOG2E
    sub = GLA_SUBCHUNK
    row = lax.broadcasted_iota(jnp.int32, (sub, L), 0)
    col = lax.broadcasted_iota(jnp.int32, (sub, L), 1)
    ones = jnp.ones((GLA_KEY_DIM, LANES), BF16)

    for hd in range(GLA_HEADS):
        ks = slice(hd * GLA_KEY_DIM, (hd + 1) * GLA_KEY_DIM)
        vs = slice(hd * GLA_VALUE_DIM, (hd + 1) * GLA_VALUE_DIM)
        b = b_all[:, ks]
        vh = v_ref[:, vs]
        st = st_ref[hd]
        q = q_ref[:, ks] * (GLA_KEY_DIM ** -0.5)
        k = k_ref[:, ks]
        b_t = b.T
        k_t = k.T
        b_last = b_t[:, L - 1:L]
        blocks = []
        for i in range(L // sub):
            r0 = i * sub
            q_i, k_i, b_i = q[r0:r0 + sub], k[r0:r0 + sub], b[r0:r0 + sub]
            terms, offs = [], []
            for s in range(sub):
                t0 = s // SUBLANES * SUBLANES
                offs.append(sum(x.shape[0] for x in terms))
                terms.append(q_i[t0:] * k_i[s:s + 1] * jnp.exp2(b_i[t0:] - b_i[s:s + 1]))
            sums = _dot(jnp.concatenate(terms, axis=0).astype(BF16), ones)
            groups = []
            for t0 in range(0, sub, SUBLANES):
                a_g = jnp.zeros((SUBLANES, L), F32)
                for s in range(min(sub, t0 + SUBLANES)):
                    lo = offs[s] + t0 - s // SUBLANES * SUBLANES
                    a_g = jnp.where(col[:SUBLANES] == r0 + s, sums[lo:lo + SUBLANES], a_g)
                groups.append(a_g)
            a_i = jnp.where(col <= r0 + row, jnp.concatenate(groups, axis=0), 0.0)
            if i > 0:
                ref_row, ref_col = b[r0 - 1:r0], b_t[:, r0 - 1:r0]
                q_s = (q_i * jnp.exp2(b_i - ref_row)).astype(BF16)
                k_s = (k_t * jnp.exp2(ref_col - b_t)).astype(BF16)
                a_i = jnp.where(col < r0, _dot(q_s, k_s), a_i)
            blocks.append(a_i)
        amat = jnp.concatenate(blocks, axis=0)
        o = _dot((q * jnp.exp2(b)).astype(BF16), st.astype(BF16)) + _dot(amat.astype(BF16), vh)
        st_ref[hd] = jnp.exp2(b_last) * st + _dot((k_t * jnp.exp2(b_last - b_t)).astype(BF16), vh)
        gate = g_ref[:, vs]
        o_ref[:, vs] = (_rms(o, ng_ref[:, vs], NORM_EPS) * (gate * _sigmoid(gate))).astype(o_ref.dtype)


def _gla_mix(q, k, v, g, z, gate_w2, gate_b, norm_g):
    B, S, _ = q.shape
    L = GLA_CHUNK
    kw =GLA_HEADS * GLA_KEY_DIM
    w2 = jnp.pad(gate_w2, ((0, LANES - GLA_GATE_RANK), (0, 0))).astype(BF16)
    tok = lambda n: pl.BlockSpec((None, L, n), lambda b, c: (b, c, 0))
    return pl.pallas_call(
        functools.partial(_gla_kernel, L=L),
        grid=(B, S // L),
        in_specs=[tok(kw), tok(kw), tok(D_MODEL), tok(D_MODEL), tok(LANES),
                  _resident(w2.shape), _resident((1, kw)), _resident((1, D_MODEL))],
        out_specs=tok(D_MODEL),
        out_shape=jax.ShapeDtypeStruct((B, S, D_MODEL), BF16),
        scratch_shapes=[pltpu.VMEM((GLA_HEADS, GLA_KEY_DIM, GLA_VALUE_DIM), F32)],
        compiler_params=_params("parallel", "arbitrary"),
        name="gla",
    )(q, k, v, g, z, w2, gate_b.reshape(1, kw), norm_g.reshape(1, D_MODEL))


def kernel(x, mem, norm_g, ffn_w_up, ffn_w_down, mem_norm_g, mem_w_kv, pool_w_in, pool_w_group, pool_scale, pool_w_out, diff_w_in, diff_lambda, diff_norm_g, diff_w_out, mlstm_w_in, mlstm_conv_w, mlstm_gate_b, mlstm_norm_g, mlstm_w_out, gla_w_in, gla_gate_w2, gla_gate_b, gla_norm_g, gla_w_out, final_norm_g):
    B, S, D = x.shape
    M = mem.shape[1]
    T = B * S
    depth = norm_g.shape[0]
    n_mixers = 4
    mem2 = mem.reshape(B * M, D)
    x2 = x.reshape(T, D)
    norm_g4 = norm_g.reshape(depth, 3, 1, D)
    w_up, w_down = ffn_w_up[0, 0].astype(BF16), ffn_w_down[0, 0].astype(BF16)

    def seq(a):
        return a.reshape(B, S, a.shape[-1])

    for i in range(depth):
        kind, j = i % n_mixers, i // n_mixers
        k_scale = X_HEAD_DIM ** -0.5 * LOG2E
        mem_k, mem_v = _norm_proj(mem2, mem_norm_g, [(mem_w_kv[i][:, :X_WIDTH] * k_scale, BF16),
                                                     (mem_w_kv[i][:, X_WIDTH:], BF16)])
        mem_k = mem_k.reshape(B, M, X_WIDTH)
        mem_v = mem_v.reshape(B, M, X_WIDTH)
        x2, w_up, w_down = _ffn(x2, _ffn_weights(norm_g4, w_up, w_down, i, 0), (ffn_w_up, ffn_w_down, i, 1))
        if kind == 0:
            w = pool_w_in[j]
            u, xq = _norm_proj(x2, norm_g[i, 1], [(w[:, :D], F32), (w[:, D:], BF16)])
            mix = _pool_mix(seq(u), pool_w_group[j], pool_scale[j])
            w_out = pool_w_out[j]
        elif kind == 1:
            w = diff_w_in[j]
            q_scale = DIFF_HEAD_DIM ** -0.5 * math.log2(math.e)
            q, k, v_t, xq = _norm_proj(x2, norm_g[i, 1], [
                (w[:, :D] * q_scale, BF16), (w[:, D:2 * D], BF16), (w[:, 2 * D:3 * D], BF16, "values_t"),
                (w[:, 3 * D:], BF16)], seq_len=S)
            mix = _diff_mix(seq(q), seq(k), v_t, diff_lambda[j], diff_norm_g[j], i)
            w_out = diff_w_out[j]
        elif kind == 2:
            w = mlstm_w_in[j]
            ng = 2 * MLSTM_HEADS
            qk, v, og, gates, xq = _norm_proj(x2, norm_g[i, 1], [
                (w[:, :2 * D], F32), (w[:, 2 * D:3 * D], BF16), (w[:, 3 * D:4 * D], F32),
                (w[:, 4 * D:4 * D + ng], F32), (w[:, 4 * D + ng:], BF16)])
            mix = _mlstm_mix(seq(qk), seq(v), seq(og), seq(gates), mlstm_conv_w[j], mlstm_gate_b[j], mlstm_norm_g[j])
            w_out = mlstm_w_out[j]
        else:
            w = gla_w_in[j]
            kw = GLA_HEADS * GLA_KEY_DIM
            o0 = 2 * kw + 2 * D
            q, k, v, g, z, xq = _norm_proj(x2, norm_g[i, 1], [
                (w[:, :kw], F32), (w[:, kw:2 * kw], F32), (w[:, 2 * kw:2 * kw + D], BF16),
                (w[:, 2 * kw + D:o0], F32), (w[:, o0:o0 + GLA_GATE_RANK], BF16), (w[:, o0 + GLA_GATE_RANK:], BF16)])
            mix = _gla_mix(seq(q), seq(k), seq(v), seq(g), seq(z), gla_gate_w2[j], gla_gate_b[j], gla_norm_g[j])
            w_out = gla_w_out[j]
        last = i == depth - 1
        x3, w_up, w_down = _attn_out_ffn(seq(x2), mix, seq(xq), mem_k, mem_v, w_out,
                                         _ffn_weights(norm_g4, w_up, w_down, i, 2),
                                         final_g=final_norm_g if last else None,
                                         next_cast=None if last else (ffn_w_up, ffn_w_down, i + 1, 0))
        x2 = x3.reshape(T, D)
    return x2.reshape(B, S, D)
```

```python
import functools
import math

import jax
import jax.numpy as jnp
from jax import lax
from jax.experimental import pallas as pl
from jax.experimental.pallas import tpu as pltpu

F32 = jnp.float32
BF16 = jnp.bfloat16

D_MODEL = 1024
D_FF = 2816
X_HEADS = 4
X_HEAD_DIM = 128
X_WIDTH = X_HEADS * X_HEAD_DIM
POOL_WINDOWS = (2, 4, 8, 16)
POOL_GROUP_DIM = D_MODEL // len(POOL_WINDOWS)
POOL_HALO = 16
DIFF_HEAD_DIM = 64
DIFF_HEADS = 8
DIFF_HEAD_BLOCK = 8
MLSTM_HEADS = 4
MLSTM_HEAD_DIM = 256
MLSTM_CONV = 4
CONV_HALO = 8
GLA_HEADS = 4
GLA_KEY_DIM = 128
GLA_VALUE_DIM = 256
GLA_GATE_RANK = 16
GLA_TAU = 16.0
GLA_SUBCHUNK = 16
LOG2E = math.log2(math.e)
NORM_EPS = 1e-6
SUBLN_EPS = 1e-5
LANES = 128
SUBLANES = 8

VMEM_LIMIT = 56 * 1024 * 1024
TOKEN_TILE = 512
FF_CHUNKS = (0, 1536, D_FF)
FFN_TOKEN_TILE = 1024
FFN_WIDE_CHUNKS = (0, 768, 1536, 2304, D_FF)
ATTN_TILE = 256
MLSTM_CHUNK = 256
GLA_CHUNK = 128


def _params(*sem):
    return pltpu.CompilerParams(dimension_semantics=sem, vmem_limit_bytes=VMEM_LIMIT)


def _resident(shape):
    nd = len(shape)
    return pl.BlockSpec(shape, lambda *_: (0,) * nd, pipeline_mode=pl.Buffered(1))


def _rms(x, g, eps):
    return x * lax.rsqrt(jnp.mean(x * x, axis=-1, keepdims=True) + eps) * g


def _sigmoid(x):
    return 1.0 / (1.0 + jnp.exp(-x))


def _log_sigmoid(x):
    return jnp.minimum(x, 0.0) - jnp.log(1.0 + jnp.exp(-jnp.abs(x)))


def _dot(a, b):
    return jnp.dot(a, b, preferred_element_type=F32)


def _dot_nt(a, b):
    return lax.dot_general(a, b, (((1,), (1,)), ((), ())), preferred_element_type=F32)


def _split3(x):
    hi = x.astype(BF16)
    r1 = x - hi.astype(F32)
    mid = r1.astype(BF16)
    lo = (r1 - mid.astype(F32)).astype(BF16)
    return hi, mid, lo


def _tri(n, upper):
    r = lax.broadcasted_iota(jnp.int32, (n, n), 0)
    c = lax.broadcasted_iota(jnp.int32, (n, n), 1)
    keep = (r <= c) if upper else (c <= r)
    return jnp.where(keep, 1.0, 0.0).astype(BF16)


def _cumsum_rows(x):
    tri = _tri(x.shape[0], upper=False)
    hi, mid, lo = _split3(x)
    return _dot(tri, hi) + _dot(tri, mid) + _dot(tri, lo)


def _cumsum_lanes(x):
    tri = _tri(x.shape[1], upper=True)
    hi, mid, lo = _split3(x)
    return _dot(hi, tri) + _dot(mid, tri) + _dot(lo, tri)


def _ffn_half_step(x, g_ref, wg_ref, wu_ref, wd_ref, chunks=FF_CHUNKS):
    h = _rms(x, g_ref[...], NORM_EPS).astype(BF16)
    acc = jnp.zeros_like(x)
    for lo, hi in zip(chunks[:-1], chunks[1:]):
        sl = slice(lo, hi)
        gate = _dot(h, wg_ref[:, sl])
        up = _dot(h, wu_ref[:, sl])
        act = (gate * _sigmoid(gate) * up).astype(BF16)
        acc = acc + _dot(act, wd_ref[sl, :])
    return x + 0.5 * acc


def _stacked(shape, lead, tail=None):
    tail = tail or (0,) * len(shape)
    return pl.BlockSpec((None,) * len(lead) + shape, lambda *_: lead + tail, pipeline_mode=pl.Buffered(1))


def _ffn_weights(norm_g, w_up, w_down, layer, norm_slot):
    args = [norm_g, w_up, w_up, w_down]
    specs = [_stacked((1, D_MODEL), (layer, norm_slot)), _stacked((D_MODEL, D_FF), (), (0, 0)),
             _stacked((D_MODEL, D_FF), (), (0, 1)), _stacked((D_FF, D_MODEL), ())]
    return args, specs


def _next_weights_cast(w_up_all, w_down_all, layer, slot, step_of, n_steps):
    args, in_specs, out_specs, out_shapes = [w_up_all, w_down_all], [], [], []
    for w in args:
        n_rows, n_cols = w.shape[2:]
        n_slabs = max(s for s in range(1, n_steps + 1) if n_rows % (16 * s) == 0)
        slab = lambda *idx, n_slabs=n_slabs: jnp.minimum(step_of(*idx), n_slabs - 1)
        in_specs.append(pl.BlockSpec((None, None, n_rows // n_slabs, n_cols),
                                     lambda *idx, slab=slab: (layer, slot, slab(*idx), 0)))
        out_specs.append(pl.BlockSpec((n_rows // n_slabs, n_cols), lambda *idx, slab=slab: (slab(*idx), 0)))
        out_shapes.append(jax.ShapeDtypeStruct((n_rows, n_cols), BF16))
    return args, in_specs, out_specs, out_shapes


def _ffn_kernel(x_ref, g_ref, wg_ref, wu_ref, wd_ref, nu_ref, nd_ref, o_ref, nu_out_ref, nd_out_ref):
    o_ref[...] = _ffn_half_step(x_ref[...], g_ref, wg_ref, wu_ref, wd_ref, FFN_WIDE_CHUNKS)
    nu_out_ref[...] = nu_ref[...].astype(BF16)
    nd_out_ref[...] = nd_ref[...].astype(BF16)


def _ffn(x, ffn_weights, next_cast):
    T = x.shape[0]
    w_args, w_specs = ffn_weights
    n_steps = T // FFN_TOKEN_TILE
    c_args, c_in, c_out, c_shapes = _next_weights_cast(*next_cast, step_of=lambda i: i, n_steps=n_steps)
    tok = pl.BlockSpec((FFN_TOKEN_TILE, D_MODEL), lambda i: (i, 0))
    y, w_up, w_down = pl.pallas_call(
        _ffn_kernel,
        grid=(n_steps,),
        in_specs=[tok] + w_specs + c_in,
        out_specs=[tok] + c_out,
        out_shape=[jax.ShapeDtypeStruct((T, D_MODEL), F32)] + c_shapes,
        compiler_params=_params("parallel"),
        name="ffn",
    )(x, *w_args, *c_args)
    return y, w_up, w_down


def _norm_proj_kernel(x_ref, g_ref, *refs, n_out):
    h = _rms(x_ref[...], g_ref[...], NORM_EPS).astype(BF16)
    for w_ref, o_ref in zip(refs[:n_out], refs[n_out:]):
        y = _dot(h, w_ref[...])
        if len(o_ref.shape) == 2:
            o_ref[...] = y.astype(o_ref.dtype)
            continue
        n_heads, n_sub, rows, t = o_ref.shape
        hd = rows - DIFF_AUX_ROWS
        aux = jnp.where(lax.broadcasted_iota(jnp.int32, (DIFF_AUX_ROWS, t), 0) == 0, 1.0, 0.0).astype(o_ref.dtype)
        for head in range(n_heads):
            for s in range(n_sub):
                o_ref[head, s, 0:hd, :] = y[s * t:(s + 1) * t, head * hd:(head + 1) * hd].T.astype(o_ref.dtype)
                o_ref[head, s, hd:rows, :] = aux


def _norm_proj(x, g, pieces, seq_len=None):
    T = x.shape[0]
    ws, out_specs, out_shapes = [], [], []
    for w, dt, *layout in pieces:
        n = w.shape[1]
        n_pad = -(-n // LANES) * LANES
        if n_pad != n:
            w = jnp.pad(w, ((0, 0), (0, n_pad - n)))
        ws.append(w.astype(BF16))
        if layout:
            hd, t, per_seq = 2 * DIFF_HEAD_DIM, ATTN_TILE, seq_len // TOKEN_TILE
            blk = (None, n // hd, TOKEN_TILE // t, hd + DIFF_AUX_ROWS, t)
            out_specs.append(pl.BlockSpec(blk, lambda i: (i // per_seq, 0, i % per_seq, 0, 0)))
            out_shapes.append(jax.ShapeDtypeStruct((T // seq_len, n // hd, seq_len // t) + blk[3:], dt))
            continue
        out_specs.append(pl.BlockSpec((TOKEN_TILE, n_pad), lambda i: (i, 0)))
        out_shapes.append(jax.ShapeDtypeStruct((T, n_pad), dt))
    in_specs = [pl.BlockSpec((TOKEN_TILE, D_MODEL), lambda i: (i, 0)), _resident((1, D_MODEL))]
    in_specs += [_resident(w.shape) for w in ws]
    return pl.pallas_call(
        functools.partial(_norm_proj_kernel, n_out=len(ws)),
        grid=(T // TOKEN_TILE,),
        in_specs=in_specs,
        out_specs=out_specs,
        out_shape=out_shapes,
        compiler_params=_params("parallel"),
        name="norm_proj",
    )(x, g.reshape(1, D_MODEL), *ws)


def _attn_out_kernel(x_ref, mix_ref, xq_ref, mk_ref, mv_ref, w1_ref, w2_ref, g_ref, wg_ref, wu_ref, wd_ref,
                     *rest, final):
    xq = xq_ref[...]
    mk = mk_ref[...]
    mv = mv_ref[...]
    outs = []
    for h in range(X_HEADS):
        sl = slice(h * X_HEAD_DIM, (h + 1) * X_HEAD_DIM)
        s = _dot_nt(xq[:, sl], mk[:, sl])
        e = jnp.exp2(s - jnp.max(s, axis=-1, keepdims=True))
        o = _dot(e.astype(BF16), mv[:, sl]) * (1.0 / jnp.sum(e, axis=-1, keepdims=True))
        outs.append(o.astype(BF16))
    xo = jnp.concatenate(outs, axis=-1)
    x = x_ref[...] + _dot(mix_ref[...], w1_ref[...]) + _dot(xo, w2_ref[...])
    y = _ffn_half_step(x, g_ref, wg_ref, wu_ref, wd_ref)
    if final:
        fg_ref, o_ref = rest
        o_ref[...] = _rms(y, fg_ref[...], NORM_EPS)
    else:
        nu_ref, nd_ref, o_ref, nu_out_ref, nd_out_ref = rest
        o_ref[...] = y
        nu_out_ref[...] = nu_ref[...].astype(BF16)
        nd_out_ref[...] = nd_ref[...].astype(BF16)


def _attn_out_ffn(x, mix, xq, mem_k, mem_v, w_out, ffn_weights, final_g=None, next_cast=None):
    B, S, _ = x.shape
    M = mem_k.shape[1]
    w1 = w_out[:D_MODEL].astype(BF16)
    w2 = w_out[D_MODEL:].astype(BF16)
    w_args, w_specs = ffn_weights
    tok = lambda n: pl.BlockSpec((None, TOKEN_TILE, n), lambda b, i: (b, i, 0))
    mem = pl.BlockSpec((None, M, X_WIDTH), lambda b, i: (b, 0, 0))
    out_specs, out_shapes = [tok(D_MODEL)], [jax.ShapeDtypeStruct((B, S, D_MODEL), F32)]
    per_seq = S // TOKEN_TILE
    if final_g is not None:
        w_args = w_args + [final_g.reshape(1, D_MODEL)]
        w_specs = w_specs + [_resident((1, D_MODEL))]
    else:
        c_args, c_in, c_out, c_shapes = _next_weights_cast(*next_cast, step_of=lambda b, i: b * per_seq + i,
                                                           n_steps=B * per_seq)
        w_args, w_specs = w_args + c_args, w_specs + c_in
        out_specs, out_shapes = out_specs + c_out, out_shapes + c_shapes
    outs = pl.pallas_call(
        functools.partial(_attn_out_kernel, final=final_g is not None),
        grid=(B, per_seq),
        in_specs=[tok(D_MODEL), tok(D_MODEL), tok(X_WIDTH), mem, mem, _resident(w1.shape), _resident(w2.shape)]
        + w_specs,
        out_specs=out_specs,
        out_shape=out_shapes,
        compiler_params=_params("parallel", "parallel"),
        name="attn_out_ffn",
    )(x, mix, xq, mem_k, mem_v, w1, w2, *w_args)
    if final_g is not None:
        return outs[0], None, None
    return outs[0], outs[1], outs[2]


def _pool_kernel(u_ref, wg_ref, sc_ref, o_ref, ext_ref, a_ref, b_ref, *, ts):
    j = pl.program_id(1)
    H = POOL_HALO

    @pl.when(j == 0)
    def _():
        ext_ref[0:2 * H, :] = jnp.zeros((2 * H, D_MODEL), F32)
        a_ref[0:H, :] = jnp.zeros((H, POOL_GROUP_DIM), F32)
        b_ref[0:H, :] = jnp.zeros((H, POOL_GROUP_DIM), F32)

    ext_ref[2 * H:2 * H + ts, :] = u_ref[...]
    pos = j * ts + lax.broadcasted_iota(jnp.int32, (ts, 1), 0)
    n = ts + H
    for g, w in enumerate(POOL_WINDOWS):
        sl = slice(g * POOL_GROUP_DIM, (g + 1) * POOL_GROUP_DIM)
        src, cols, d, bufs = ext_ref, sl, 1, [a_ref, b_ref]
        while d < w:
            dst = bufs.pop(0)
            dst[H:H + n, :] = src[H:H + n, cols] + src[H - d:H - d + n, cols]
            bufs.append(dst)
            src, cols, d = dst, slice(None), 2 * d
        u = u_ref[:, sl]
        inv = 1.0 / jnp.minimum(pos + 1, w).astype(F32)
        pooled = (src[2 * H:2 * H + ts, cols] * inv - u).astype(BF16)
        o_ref[:, sl] = (_dot(pooled, wg_ref[g]) * sc_ref[:, sl]).astype(o_ref.dtype)
    ext_ref[H:2 * H, :] = ext_ref[ts + H:ts + 2 * H, :]


def _pool_mix(u, w_group, scale):
    B, S, _ = u.shape
    ts = TOKEN_TILE
    tok = pl.BlockSpec((None, ts, D_MODEL), lambda b, j: (b, j, 0))
    return pl.pallas_call(
        functools.partial(_pool_kernel, ts=ts),
        grid=(B, S // ts),
        in_specs=[tok, _resident(w_group.shape), _resident((1, D_MODEL))],
        out_specs=tok,
        out_shape=jax.ShapeDtypeStruct((B, S, D_MODEL), BF16),
        scratch_shapes=[pltpu.VMEM((ts + 2 * POOL_HALO, D_MODEL), F32),
                        pltpu.VMEM((ts + 2 * POOL_HALO, POOL_GROUP_DIM), F32),
                        pltpu.VMEM((ts + 2 * POOL_HALO, POOL_GROUP_DIM), F32)],
        compiler_params=_params("parallel", "arbitrary"),
        name="pool_mix",
    )(u, w_group.astype(BF16), scale.reshape(1, D_MODEL))


DIFF_SLOPE_TERMS = 3
DIFF_AUX_ROWS = 16


def _lane_features(lane, groups):
    out = jnp.zeros(lane.shape, F32)
    for g, v in enumerate(groups):
        for r in range(DIFF_SLOPE_TERMS):
            out = jnp.where(lane == DIFF_SLOPE_TERMS * g + r, v[r] if isinstance(v, list) else v, out)
    return out


def _diff_kernel(slopes_ref, lam_ref, q_ref, k_ref, vt_ref, ng_ref, o_ref,
                 ka_ref, qf_ref, acc_ref, m_ref, *, t, n_tiles, lam_init):
    hg = pl.program_id(1)
    qi = pl.program_id(2)
    hd = 2 * DIFF_HEAD_DIM
    lane = lax.broadcasted_iota(jnp.int32, (t, hd), 1)
    rowf = lax.broadcasted_iota(jnp.int32, (t, hd), 0).astype(F32)
    own = (lane < DIFF_HEAD_DIM, lane >= DIFF_HEAD_DIM)
    heads = range(DIFF_HEAD_BLOCK)
    c_terms = [[slopes_ref[hg * DIFF_HEAD_BLOCK + hb, r] for r in range(DIFF_SLOPE_TERMS)] for hb in heads]
    chains = [(hb, c) for hb in heads for c in range(2)]

    @pl.when(qi == 0)
    def _():
        def build(j, carry):
            start = pl.multiple_of(j * t, t)
            tile_idx = jnp.asarray(j, F32)
            for hb in heads:
                neg = [-cr for cr in c_terms[hb]]
                feat = _lane_features(lane, [rowf, tile_idx, neg, [cr * t for cr in neg]])
                ka_ref[hb, pl.ds(start, t), 0:hd] = k_ref[pl.ds(start, t), hb * hd:(hb + 1) * hd]
                ka_ref[hb, pl.ds(start, t), hd:2 * hd] = feat.astype(BF16)
            return carry

        lax.fori_loop(0, n_tiles, build, 0)
        for hb in heads:
            qf_ref[hb] = _lane_features(lane, [c_terms[hb], [cr * t for cr in c_terms[hb]], rowf, 0.0])

    qa = {}
    tile_lanes = (lane >= 3 * DIFF_SLOPE_TERMS) & (lane < 4 * DIFF_SLOPE_TERMS)
    for hb in heads:
        q = q_ref[:, hb * hd:(hb + 1) * hd]
        feat = jnp.where(tile_lanes, jnp.asarray(qi, F32), qf_ref[hb])
        for c in range(2):
            qa[hb, c] = jnp.concatenate([jnp.where(own[c], q, jnp.zeros_like(q)), feat.astype(BF16)], axis=1)
    acc_ref[...] = jnp.zeros_like(acc_ref)
    m_ref[...] = jnp.full(m_ref.shape, -jnp.inf, F32)

    def tile(kj, masked):
        start = pl.multiple_of(kj * t, t)
        scores = {}
        for hb, c in chains:
            s = _dot_nt(ka_ref[hb, pl.ds(start, t), :], qa[hb, c])
            if masked:
                key = lax.broadcasted_iota(jnp.int32, (t, t), 0)
                qry = lax.broadcasted_iota(jnp.int32, (t, t), 1)
                s = jnp.where(key <= qry, s, -jnp.inf)
            scores[hb, c] = s
        probs = {}
        for hb, c in chains:
            s = scores[hb, c]
            m_prev = m_ref[hb, c]
            m_new = jnp.maximum(m_prev, jnp.max(s, axis=0, keepdims=True))
            alpha = jnp.exp2(m_prev - m_new)
            p = jnp.exp2(s - m_new)
            m_ref[hb, c] = m_new
            probs[hb, c] = (alpha, p.astype(BF16))
        for hb, c in chains:
            alpha, p = probs[hb, c]
            acc_ref[hb, c] = alpha * acc_ref[hb, c] + _dot(vt_ref[hb, kj], p)

    def body(kj, carry):
        tile(kj, masked=False)
        return carry

    lax.fori_loop(0, qi, body, 0)
    tile(qi, masked=True)

    lp = lam_ref[...]
    lam = (jnp.exp(jnp.sum(lp[0:1] * lp[1:2], axis=-1, keepdims=True))
           - jnp.exp(jnp.sum(lp[2:3] * lp[3:4], axis=-1, keepdims=True)) + lam_init)
    for hb in range(DIFF_HEAD_BLOCK):
        a0, a1 = acc_ref[hb, 0], acc_ref[hb, 1]
        o_t = (a0[:hd] * (1.0 / a0[hd:hd + 1]) - lam * (a1[:hd] * (1.0 / a1[hd:hd + 1])))
        o_ref[:, hb * hd:(hb + 1) * hd] = (_rms(o_t.T, ng_ref[...], SUBLN_EPS) * (1.0 - lam_init)).astype(o_ref.dtype)


def _diff_mix(q, k, v_t, lam_p, norm_g, layer_idx):
    B, S, _ = q.shape
    t = ATTN_TILE
    n_tiles = S // t
    hd = 2 * DIFF_HEAD_DIM
    hb = DIFF_HEAD_BLOCK
    lam_init = 0.8 - 0.6 * math.exp(-0.3 * layer_idx)
    slopes = jnp.asarray([2.0 ** (-8.0 * (h + 1) / DIFF_HEADS) for h in range(DIFF_HEADS)], dtype=F32)
    rest = slopes * math.log2(math.e)
    terms = []
    for _ in range(DIFF_SLOPE_TERMS):
        terms.append(rest.astype(BF16).astype(F32))
        rest = rest - terms[-1]
    slopes = jnp.stack(terms, axis=1)
    qspec = pl.BlockSpec((None, t, hb * hd), lambda b, h, i: (b, i, h))
    return pl.pallas_call(
        functools.partial(_diff_kernel, t=t, n_tiles=n_tiles, lam_init=lam_init),
        grid=(B, DIFF_HEADS // hb, n_tiles),
        in_specs=[pl.BlockSpec(memory_space=pltpu.SMEM), _resident(lam_p.shape), qspec,
                  pl.BlockSpec((None, S, hb * hd), lambda b, h, i: (b, 0, h), pipeline_mode=pl.Buffered(1)),
                  pl.BlockSpec((None, hb, n_tiles, hd + DIFF_AUX_ROWS, t), lambda b, h, i: (b, h, 0, 0, 0),
                               pipeline_mode=pl.Buffered(1)),
                  _resident((1, hd))],
        out_specs=qspec,
        out_shape=jax.ShapeDtypeStruct((B, S, D_MODEL), BF16),
        scratch_shapes=[pltpu.VMEM((hb, S, 2 * hd), BF16), pltpu.VMEM((hb, t, hd), F32),
                        pltpu.VMEM((hb, 2, hd + DIFF_AUX_ROWS, t), F32),
                        pltpu.VMEM((hb, 2, 1, t), F32)],
        compiler_params=_params("parallel", "parallel", "arbitrary"),
        name="diff_attn",
    )(slopes, lam_p, q, k, v_t, norm_g.reshape(1, hd))


def _mlstm_kernel(qk_ref, v_ref, og_ref, gc_ref, gr_ref, cw_ref, gbc_ref, gbr_ref, ng_ref, o_ref,
                  ext_ref, c_ref, n_ref, m_ref, *, L):
    c = pl.program_id(1)
    W = 2 * D_MODEL

    @pl.when(c == 0)
    def _():
        ext_ref[0:CONV_HALO, :] = jnp.zeros((CONV_HALO, W), F32)
        c_ref[...] = jnp.zeros_like(c_ref)
        n_ref[...] = jnp.zeros_like(n_ref)
        m_ref[...] = jnp.zeros_like(m_ref)

    ext_ref[CONV_HALO:CONV_HALO + L, :] = qk_ref[...]
    conv = cw_ref[MLSTM_CONV - 1:MLSTM_CONV, :] * qk_ref[...]
    for j in range(1, MLSTM_CONV):
        conv = conv + (cw_ref[MLSTM_CONV - 1 - j:MLSTM_CONV - j, :]
                       * ext_ref[CONV_HALO - j:CONV_HALO - j + L, :])
    ext_ref[0:CONV_HALO, :] = ext_ref[L:L + CONV_HALO, :]
    qk = conv * _sigmoid(conv)

    gc = gc_ref[...] + gbc_ref[...]
    gr = gr_ref[...] + gbr_ref[...]
    b_c = _cumsum_rows(_log_sigmoid(gc))
    b_r = _cumsum_lanes(_log_sigmoid(gr))
    row = lax.broadcasted_iota(jnp.int32, (L, L), 0)
    col = lax.broadcasted_iota(jnp.int32, (L, L), 1)
    causal = col <= row

    for hd in range(MLSTM_HEADS):
        sl = slice(hd * MLSTM_HEAD_DIM, (hd + 1) * MLSTM_HEAD_DIM)
        q32 = qk[:, sl] * (MLSTM_HEAD_DIM ** -0.5)
        k32 = qk[:, D_MODEL + hd * MLSTM_HEAD_DIM:D_MODEL + (hd + 1) * MLSTM_HEAD_DIM]
        vh = v_ref[:, sl]
        fi = MLSTM_HEADS + hd
        bcol, icol = b_c[:, fi:fi + 1], gc[:, hd:hd + 1]
        brow, irow = b_r[fi:fi + 1, :], gr[hd:hd + 1, :]
        b_last = brow[:, L - 1:L]
        m = m_ref[hd:hd + 1, 0:1]
        cmat = c_ref[hd]
        nrow = n_ref[hd]

        dmat = jnp.where(causal, bcol - brow + irow, -jnp.inf)
        inter = bcol + m
        m_t = jnp.maximum(inter, jnp.max(dmat, axis=-1, keepdims=True))
        dec = jnp.exp(inter - m_t)
        qb = q32.astype(BF16)
        k_t = k32.T
        sqk = _dot(qb, k_t.astype(BF16)) * jnp.exp(dmat - m_t)
        num = dec * _dot(qb, cmat.astype(BF16)) + _dot(sqk.astype(BF16), vh)
        den = dec * jnp.sum(q32 * nrow, axis=-1, keepdims=True) + jnp.sum(sqk, axis=-1, keepdims=True)
        hc = num * (1.0 / jnp.maximum(jnp.abs(den), jnp.exp(-m_t)))

        gs_r = b_last - brow + irow
        gs_c = b_last - bcol + icol
        m_new = jnp.maximum(b_last + m, jnp.max(gs_r, axis=-1, keepdims=True))
        carry_dec = jnp.exp(b_last + m - m_new)
        c_ref[hd] = carry_dec * cmat + _dot((k_t * jnp.exp(gs_r - m_new)).astype(BF16), vh)
        n_ref[hd] = carry_dec * nrow + jnp.sum(k32 * jnp.exp(gs_c - m_new), axis=0, keepdims=True)
        m_ref[hd:hd + 1, :] = jnp.broadcast_to(m_new, (1, LANES))

        o_ref[:, sl] = (_rms(hc, ng_ref[:, sl], NORM_EPS) * _sigmoid(og_ref[:, sl])).astype(o_ref.dtype)


def _mlstm_mix(qk, v, og, gates, conv_w, gate_b, norm_g):
    B, S, _ = qk.shape
    L = MLSTM_CHUNK
    H = MLSTM_HEADS
    gates_r = jnp.transpose(gates[..., :2 * H], (0, 2, 1))
    gb = gate_b.reshape(2 * H)
    gb_c = jnp.pad(gb, (0, LANES - 2 * H)).reshape(1, LANES)
    gb_r = gb.reshape(2 * H, 1)
    tok = lambda n: pl.BlockSpec((None, L, n), lambda b, c: (b, c, 0))
    return pl.pallas_call(
        functools.partial(_mlstm_kernel, L=L),
        grid=(B, S // L),
        in_specs=[tok(2 * D_MODEL), tok(D_MODEL), tok(D_MODEL), tok(LANES),
                  pl.BlockSpec((None, 2 * H, L), lambda b, c: (b, 0, c)),
                  _resident(conv_w.shape), _resident((1, LANES)), _resident((2 * H, 1)), _resident((1, D_MODEL))],
        out_specs=tok(D_MODEL),
        out_shape=jax.ShapeDtypeStruct((B, S, D_MODEL), BF16),
        scratch_shapes=[pltpu.VMEM((L + CONV_HALO, 2 * D_MODEL), F32),
                        pltpu.VMEM((H, MLSTM_HEAD_DIM, MLSTM_HEAD_DIM), F32),
                        pltpu.VMEM((H, 1, MLSTM_HEAD_DIM), F32),
                        pltpu.VMEM((8, LANES), F32)],
        compiler_params=_params("parallel", "arbitrary"),
        name="mlstm",
    )(qk, v, og, gates, gates_r, conv_w, gb_c, gb_r, norm_g.reshape(1, D_MODEL))


def _gla_kernel(q_ref, k_ref, v_ref, g_ref, z_ref, w2_ref, gb_ref, ng_ref, o_ref, st_ref, *, L):
    c = pl.program_id(1)

    @pl.when(c == 0)
    def _():
        st_ref[...] = jnp.zeros_like(st_ref)

    log_a = _log_sigmoid(_dot(z_ref[...], w2_ref[...]) + gb_ref[...]) * (1.0 / GLA_TAU)
    b_all = _cumsum_rows(log_a) * LOG2E
    sub = GLA_SUBCHUNK
    row = lax.broadcasted_iota(jnp.int32, (sub, L), 0)
    col = lax.broadcasted_iota(jnp.int32, (sub, L), 1)
    ones = jnp.ones((GLA_KEY_DIM, LANES), BF16)

    for hd in range(GLA_HEADS):
        ks = slice(hd * GLA_KEY_DIM, (hd + 1) * GLA_KEY_DIM)
        vs = slice(hd * GLA_VALUE_DIM, (hd + 1) * GLA_VALUE_DIM)
        b = b_all[:, ks]
        vh = v_ref[:, vs]
        st = st_ref[hd]
        q = q_ref[:, ks] * (GLA_KEY_DIM ** -0.5)
        k = k_ref[:, ks]
        b_t = b.T
        k_t = k.T
        b_last = b_t[:, L - 1:L]
        blocks = []
        for i in range(L // sub):
            r0 = i * sub
            q_i, k_i, b_i = q[r0:r0 + sub], k[r0:r0 + sub], b[r0:r0 + sub]
            terms, offs = [], []
            for s in range(sub):
                t0 = s // SUBLANES * SUBLANES
                offs.append(sum(x.shape[0] for x in terms))
                terms.append(q_i[t0:] * k_i[s:s + 1] * jnp.exp2(b_i[t0:] - b_i[s:s + 1]))
            sums = _dot(jnp.concatenate(terms, axis=0).astype(BF16), ones)
            groups = []
            for t0 in range(0, sub, SUBLANES):
                a_g = jnp.zeros((SUBLANES, L), F32)
                for s in range(min(sub, t0 + SUBLANES)):
                    lo = offs[s] + t0 - s // SUBLANES * SUBLANES
                    a_g = jnp.where(col[:SUBLANES] == r0 + s, sums[lo:lo + SUBLANES], a_g)
                groups.append(a_g)
            a_i = jnp.where(col <= r0 + row, jnp.concatenate(groups, axis=0), 0.0)
            if i > 0:
                ref_row, ref_col = b[r0 - 1:r0], b_t[:, r0 - 1:r0]
                q_s = (q_i * jnp.exp2(b_i - ref_row)).astype(BF16)
                k_s = (k_t * jnp.exp2(ref_col - b_t)).astype(BF16)
                a_i = jnp.where(col < r0, _dot(q_s, k_s), a_i)
            blocks.append(a_i)
        amat = jnp.concatenate(blocks, axis=0)
        o = _dot((q * jnp.exp2(b)).astype(BF16), st.astype(BF16)) + _dot(amat.astype(BF16), vh)
        st_ref[hd] = jnp.exp2(b_last) * st + _dot((k_t * jnp.exp2(b_last - b_t)).astype(BF16), vh)
        gate = g_ref[:, vs]
        o_ref[:, vs] = (_rms(o, ng_ref[:, vs], NORM_EPS) * (gate * _sigmoid(gate))).astype(o_ref.dtype)


def _gla_mix(q, k, v, g, z, gate_w2, gate_b, norm_g):
    B, S, _ = q.shape
    L = GLA_CHUNK
    kw =GLA_HEADS * GLA_KEY_DIM
    w2 = jnp.pad(gate_w2, ((0, LANES - GLA_GATE_RANK), (0, 0))).astype(BF16)
    tok = lambda n: pl.BlockSpec((None, L, n), lambda b, c: (b, c, 0))
    return pl.pallas_call(
        functools.partial(_gla_kernel, L=L),
        grid=(B, S // L),
        in_specs=[tok(kw), tok(kw), tok(D_MODEL), tok(D_MODEL), tok(LANES),
                  _resident(w2.shape), _resident((1, kw)), _resident((1, D_MODEL))],
        out_specs=tok(D_MODEL),
        out_shape=jax.ShapeDtypeStruct((B, S, D_MODEL), BF16),
        scratch_shapes=[pltpu.VMEM((GLA_HEADS, GLA_KEY_DIM, GLA_VALUE_DIM), F32)],
        compiler_params=_params("parallel", "arbitrary"),
        name="gla",
    )(q, k, v, g, z, w2, gate_b.reshape(1, kw), norm_g.reshape(1, D_MODEL))


def kernel(x, mem, norm_g, ffn_w_up, ffn_w_down, mem_norm_g, mem_w_kv, pool_w_in, pool_w_group, pool_scale, pool_w_out, diff_w_in, diff_lambda, diff_norm_g, diff_w_out, mlstm_w_in, mlstm_conv_w, mlstm_gate_b, mlstm_norm_g, mlstm_w_out, gla_w_in, gla_gate_w2, gla_gate_b, gla_norm_g, gla_w_out, final_norm_g):
    B, S, D = x.shape
    M = mem.shape[1]
    T = B * S
    depth = norm_g.shape[0]
    n_mixers = 4
    mem2 = mem.reshape(B * M, D)
    x2 = x.reshape(T, D)
    norm_g4 = norm_g.reshape(depth, 3, 1, D)
    w_up, w_down = ffn_w_up[0, 0].astype(BF16), ffn_w_down[0, 0].astype(BF16)

    def seq(a):
        return a.reshape(B, S, a.shape[-1])

    for i in range(depth):
        kind, j = i % n_mixers, i // n_mixers
        k_scale = X_HEAD_DIM ** -0.5 * LOG2E
        mem_k, mem_v = _norm_proj(mem2, mem_norm_g, [(mem_w_kv[i][:, :X_WIDTH] * k_scale, BF16),
                                                     (mem_w_kv[i][:, X_WIDTH:], BF16)])
        mem_k = mem_k.reshape(B, M, X_WIDTH)
        mem_v = mem_v.reshape(B, M, X_WIDTH)
        x2, w_up, w_down = _ffn(x2, _ffn_weights(norm_g4, w_up, w_down, i, 0), (ffn_w_up, ffn_w_down, i, 1))
        if kind == 0:
            w = pool_w_in[j]
            u, xq = _norm_proj(x2, norm_g[i, 1], [(w[:, :D], F32), (w[:, D:], BF16)])
            mix = _pool_mix(seq(u), pool_w_group[j], pool_scale[j])
            w_out = pool_w_out[j]
        elif kind == 1:
            w = diff_w_in[j]
            q_scale = DIFF_HEAD_DIM ** -0.5 * math.log2(math.e)
            q, k, v_t, xq = _norm_proj(x2, norm_g[i, 1], [
                (w[:, :D] * q_scale, BF16), (w[:, D:2 * D], BF16), (w[:, 2 * D:3 * D], BF16, "values_t"),
                (w[:, 3 * D:], BF16)], seq_len=S)
            mix = _diff_mix(seq(q), seq(k), v_t, diff_lambda[j], diff_norm_g[j], i)
            w_out = diff_w_out[j]
        elif kind == 2:
            w = mlstm_w_in[j]
            ng = 2 * MLSTM_HEADS
            qk, v, og, gates, xq = _norm_proj(x2, norm_g[i, 1], [
                (w[:, :2 * D], F32), (w[:, 2 * D:3 * D], BF16), (w[:, 3 * D:4 * D], F32),
                (w[:, 4 * D:4 * D + ng], F32), (w[:, 4 * D + ng:], BF16)])
            mix = _mlstm_mix(seq(qk), seq(v), seq(og), seq(gates), mlstm_conv_w[j], mlstm_gate_b[j], mlstm_norm_g[j])
            w_out = mlstm_w_out[j]
        else:
            w = gla_w_in[j]
            kw = GLA_HEADS * GLA_KEY_DIM
            o0 = 2 * kw + 2 * D
            q, k, v, g, z, xq = _norm_proj(x2, norm_g[i, 1], [
                (w[:, :kw], F32), (w[:, kw:2 * kw], F32), (w[:, 2 * kw:2 * kw + D], BF16),
                (w[:, 2 * kw + D:o0], F32), (w[:, o0:o0 + GLA_GATE_RANK], BF16), (w[:, o0 + GLA_GATE_RANK:], BF16)])
            mix = _gla_mix(seq(q), seq(k), seq(v), seq(g), seq(z), gla_gate_w2[j], gla_gate_b[j], gla_norm_g[j])
            w_out = gla_w_out[j]
        last = i == depth - 1
        x3, w_up, w_down = _attn_out_ffn(seq(x2), mix, seq(xq), mem_k, mem_v, w_out,
                                         _ffn_weights(norm_g4, w_up, w_down, i, 2),
                                         final_g=final_norm_g if last else None,
                                         next_cast=None if last else (ffn_w_up, ffn_w_down, i + 1, 0))
        x2 = x3.reshape(T, D)
    return x2.reshape(B, S, D)
```

```python
import functools
import math

import jax
import jax.numpy as jnp
from jax import lax
from jax.experimental import pallas as pl
from jax.experimental.pallas import tpu as pltpu

F32 = jnp.float32
BF16 = jnp.bfloat16

D_MODEL = 1024
D_FF = 2816
X_HEADS = 4
X_HEAD_DIM = 128
X_WIDTH = X_HEADS * X_HEAD_DIM
POOL_WINDOWS = (2, 4, 8, 16)
POOL_GROUP_DIM = D_MODEL // len(POOL_WINDOWS)
POOL_HALO = 16
DIFF_HEAD_DIM = 64
DIFF_HEADS = 8
DIFF_HEAD_BLOCK = 8
MLSTM_HEADS = 4
MLSTM_HEAD_DIM = 256
MLSTM_CONV = 4
CONV_HALO = 8
GLA_HEADS = 4
GLA_KEY_DIM = 128
GLA_VALUE_DIM = 256
GLA_GATE_RANK = 16
GLA_TAU = 16.0
GLA_SUBCHUNK = 16
LOG2E = math.log2(math.e)
NORM_EPS = 1e-6
SUBLN_EPS = 1e-5
LANES = 128
SUBLANES = 8

VMEM_LIMIT = 56 * 1024 * 1024
TOKEN_TILE = 512
FF_CHUNKS = (0, 1536, D_FF)
FFN_TOKEN_TILE = 1024
FFN_WIDE_CHUNKS = (0, 768, 1536, 2304, D_FF)
ATTN_TILE = 256
MLSTM_CHUNK = 256
GLA_CHUNK = 128


def _params(*sem):
    return pltpu.CompilerParams(dimension_semantics=sem, vmem_limit_bytes=VMEM_LIMIT)


def _resident(shape):
    nd = len(shape)
    return pl.BlockSpec(shape, lambda *_: (0,) * nd, pipeline_mode=pl.Buffered(1))


def _rms(x, g, eps):
    return x * lax.rsqrt(jnp.mean(x * x, axis=-1, keepdims=True) + eps) * g


def _sigmoid(x):
    return 1.0 / (1.0 + jnp.exp(-x))


def _log_sigmoid(x):
    return jnp.minimum(x, 0.0) - jnp.log(1.0 + jnp.exp(-jnp.abs(x)))


def _dot(a, b):
    return jnp.dot(a, b, preferred_element_type=F32)


def _dot_nt(a, b):
    return lax.dot_general(a, b, (((1,), (1,)), ((), ())), preferred_element_type=F32)


def _split3(x):
    hi = x.astype(BF16)
    r1 = x - hi.astype(F32)
    mid = r1.astype(BF16)
    lo = (r1 - mid.astype(F32)).astype(BF16)
    return hi, mid, lo


def _tri(n, upper):
    r = lax.broadcasted_iota(jnp.int32, (n, n), 0)
    c = lax.broadcasted_iota(jnp.int32, (n, n), 1)
    keep = (r <= c) if upper else (c <= r)
    return jnp.where(keep, 1.0, 0.0).astype(BF16)


def _cumsum_rows(x):
    tri = _tri(x.shape[0], upper=False)
    hi, mid, lo = _split3(x)
    return _dot(tri, hi) + _dot(tri, mid) + _dot(tri, lo)


def _cumsum_lanes(x):
    tri = _tri(x.shape[1], upper=True)
    hi, mid, lo = _split3(x)
    return _dot(hi, tri) + _dot(mid, tri) + _dot(lo, tri)


def _ffn_half_step(x, g_ref, wg_ref, wu_ref, wd_ref, chunks=FF_CHUNKS):
    h = _rms(x, g_ref[...], NORM_EPS).astype(BF16)
    acc = jnp.zeros_like(x)
    for lo, hi in zip(chunks[:-1], chunks[1:]):
        sl = slice(lo, hi)
        gate = _dot(h, wg_ref[:, sl])
        up = _dot(h, wu_ref[:, sl])
        act = (gate * _sigmoid(gate) * up).astype(BF16)
        acc = acc + _dot(act, wd_ref[sl, :])
    return x + 0.5 * acc


def _stacked(shape, lead, tail=None):
    tail = tail or (0,) * len(shape)
    return pl.BlockSpec((None,) * len(lead) + shape, lambda *_: lead + tail, pipeline_mode=pl.Buffered(1))


def _ffn_weights(norm_g, w_up, w_down, layer, norm_slot):
    args = [norm_g, w_up, w_up, w_down]
    specs = [_stacked((1, D_MODEL), (layer, norm_slot)), _stacked((D_MODEL, D_FF), (), (0, 0)),
             _stacked((D_MODEL, D_FF), (), (0, 1)), _stacked((D_FF, D_MODEL), ())]
    return args, specs


def _next_weights_cast(w_up_all, w_down_all, layer, slot, step_of, n_steps):
    args, in_specs, out_specs, out_shapes = [w_up_all, w_down_all], [], [], []
    for w in args:
        n_rows, n_cols = w.shape[2:]
        n_slabs = max(s for s in range(1, n_steps + 1) if n_rows % (16 * s) == 0)
        slab = lambda *idx, n_slabs=n_slabs: jnp.minimum(step_of(*idx), n_slabs - 1)
        in_specs.append(pl.BlockSpec((None, None, n_rows // n_slabs, n_cols),
                                     lambda *idx, slab=slab: (layer, slot, slab(*idx), 0)))
        out_specs.append(pl.BlockSpec((n_rows // n_slabs, n_cols), lambda *idx, slab=slab: (slab(*idx), 0)))
        out_shapes.append(jax.ShapeDtypeStruct((n_rows, n_cols), BF16))
    return args, in_specs, out_specs, out_shapes


def _ffn_kernel(x_ref, g_ref, wg_ref, wu_ref, wd_ref, nu_ref, nd_ref, o_ref, nu_out_ref, nd_out_ref):
    o_ref[...] = _ffn_half_step(x_ref[...], g_ref, wg_ref, wu_ref, wd_ref, FFN_WIDE_CHUNKS)
    nu_out_ref[...] = nu_ref[...].astype(BF16)
    nd_out_ref[...] = nd_ref[...].astype(BF16)


def _ffn(x, ffn_weights, next_cast):
    T = x.shape[0]
    w_args, w_specs = ffn_weights
    n_steps = T // FFN_TOKEN_TILE
    c_args, c_in, c_out, c_shapes = _next_weights_cast(*next_cast, step_of=lambda i: i, n_steps=n_steps)
    tok = pl.BlockSpec((FFN_TOKEN_TILE, D_MODEL), lambda i: (i, 0))
    y, w_up, w_down = pl.pallas_call(
        _ffn_kernel,
        grid=(n_steps,),
        in_specs=[tok] + w_specs + c_in,
        out_specs=[tok] + c_out,
        out_shape=[jax.ShapeDtypeStruct((T, D_MODEL), F32)] + c_shapes,
        compiler_params=_params("parallel"),
        name="ffn",
    )(x, *w_args, *c_args)
    return y, w_up, w_down


def _norm_proj_kernel(x_ref, g_ref, *refs, n_out):
    h = _rms(x_ref[...], g_ref[...], NORM_EPS).astype(BF16)
    for w_ref, o_ref in zip(refs[:n_out], refs[n_out:]):
        y = _dot(h, w_ref[...])
        if len(o_ref.shape) == 2:
            o_ref[...] = y.astype(o_ref.dtype)
            continue
        n_heads, n_sub, rows, t = o_ref.shape
        hd = rows - DIFF_AUX_ROWS
        aux = jnp.where(lax.broadcasted_iota(jnp.int32, (DIFF_AUX_ROWS, t), 0) == 0, 1.0, 0.0).astype(o_ref.dtype)
        for head in range(n_heads):
            for s in range(n_sub):
                o_ref[head, s, 0:hd, :] = y[s * t:(s + 1) * t, head * hd:(head + 1) * hd].T.astype(o_ref.dtype)
                o_ref[head, s, hd:rows, :] = aux


def _norm_proj(x, g, pieces, seq_len=None):
    T = x.shape[0]
    ws, out_specs, out_shapes = [], [], []
    for w, dt, *layout in pieces:
        n = w.shape[1]
        n_pad = -(-n // LANES) * LANES
        if n_pad != n:
            w = jnp.pad(w, ((0, 0), (0, n_pad - n)))
        ws.append(w.astype(BF16))
        if layout:
            hd, t, per_seq = 2 * DIFF_HEAD_DIM, ATTN_TILE, seq_len // TOKEN_TILE
            blk = (None, n // hd, TOKEN_TILE // t, hd + DIFF_AUX_ROWS, t)
            out_specs.append(pl.BlockSpec(blk, lambda i: (i // per_seq, 0, i % per_seq, 0, 0)))
            out_shapes.append(jax.ShapeDtypeStruct((T // seq_len, n // hd, seq_len // t) + blk[3:], dt))
            continue
        out_specs.append(pl.BlockSpec((TOKEN_TILE, n_pad), lambda i: (i, 0)))
        out_shapes.append(jax.ShapeDtypeStruct((T, n_pad), dt))
    in_specs = [pl.BlockSpec((TOKEN_TILE, D_MODEL), lambda i: (i, 0)), _resident((1, D_MODEL))]
    in_specs += [_resident(w.shape) for w in ws]
    return pl.pallas_call(
        functools.partial(_norm_proj_kernel, n_out=len(ws)),
        grid=(T // TOKEN_TILE,),
        in_specs=in_specs,
        out_specs=out_specs,
        out_shape=out_shapes,
        compiler_params=_params("parallel"),
        name="norm_proj",
    )(x, g.reshape(1, D_MODEL), *ws)


def _attn_out_kernel(x_ref, mix_ref, xq_ref, mk_ref, mv_ref, w1_ref, w2_ref, g_ref, wg_ref, wu_ref, wd_ref,
                     *rest, final):
    xq = xq_ref[...]
    mk = mk_ref[...]
    mv = mv_ref[...]
    outs = []
    for h in range(X_HEADS):
        sl = slice(h * X_HEAD_DIM, (h + 1) * X_HEAD_DIM)
        s = _dot_nt(xq[:, sl], mk[:, sl])
        e = jnp.exp2(s - jnp.max(s, axis=-1, keepdims=True))
        o = _dot(e.astype(BF16), mv[:, sl]) * (1.0 / jnp.sum(e, axis=-1, keepdims=True))
        outs.append(o.astype(BF16))
    xo = jnp.concatenate(outs, axis=-1)
    x = x_ref[...] + _dot(mix_ref[...], w1_ref[...]) + _dot(xo, w2_ref[...])
    y = _ffn_half_step(x, g_ref, wg_ref, wu_ref, wd_ref)
    if final:
        fg_ref, o_ref = rest
        o_ref[...] = _rms(y, fg_ref[...], NORM_EPS)
    else:
        nu_ref, nd_ref, o_ref, nu_out_ref, nd_out_ref = rest
        o_ref[...] = y
        nu_out_ref[...] = nu_ref[...].astype(BF16)
        nd_out_ref[...] = nd_ref[...].astype(BF16)


def _attn_out_ffn(x, mix, xq, mem_k, mem_v, w_out, ffn_weights, final_g=None, next_cast=None):
    B, S, _ = x.shape
    M = mem_k.shape[1]
    w1 = w_out[:D_MODEL].astype(BF16)
    w2 = w_out[D_MODEL:].astype(BF16)
    w_args, w_specs = ffn_weights
    tok = lambda n: pl.BlockSpec((None, TOKEN_TILE, n), lambda b, i: (b, i, 0))
    mem = pl.BlockSpec((None, M, X_WIDTH), lambda b, i: (b, 0, 0))
    out_specs, out_shapes = [tok(D_MODEL)], [jax.ShapeDtypeStruct((B, S, D_MODEL), F32)]
    per_seq = S // TOKEN_TILE
    if final_g is not None:
        w_args = w_args + [final_g.reshape(1, D_MODEL)]
        w_specs = w_specs + [_resident((1, D_MODEL))]
    else:
        c_args, c_in, c_out, c_shapes = _next_weights_cast(*next_cast, step_of=lambda b, i: b * per_seq + i,
                                                           n_steps=B * per_seq)
        w_args, w_specs = w_args + c_args, w_specs + c_in
        out_specs, out_shapes = out_specs + c_out, out_shapes + c_shapes
    outs = pl.pallas_call(
        functools.partial(_attn_out_kernel, final=final_g is not None),
        grid=(B, per_seq),
        in_specs=[tok(D_MODEL), tok(D_MODEL), tok(X_WIDTH), mem, mem, _resident(w1.shape), _resident(w2.shape)]
        + w_specs,
        out_specs=out_specs,
        out_shape=out_shapes,
        compiler_params=_params("parallel", "parallel"),
        name="attn_out_ffn",
    )(x, mix, xq, mem_k, mem_v, w1, w2, *w_args)
    if final_g is not None:
        return outs[0], None, None
    return outs[0], outs[1], outs[2]


def _pool_kernel(x_ref, g_ref, wu_ref, wq_ref, wg_ref, sc_ref, o_ref, xq_ref, ext_ref, a_ref, b_ref, *, ts):
    j = pl.program_id(1)
    H = POOL_HALO

    @pl.when(j == 0)
    def _():
        ext_ref[0:2 * H, :] = jnp.zeros((2 * H, D_MODEL), F32)
        a_ref[0:H, :] = jnp.zeros((H, POOL_GROUP_DIM), F32)
        b_ref[0:H, :] = jnp.zeros((H, POOL_GROUP_DIM), F32)

    h = _rms(x_ref[...], g_ref[...], NORM_EPS).astype(BF16)
    xq_ref[...] = _dot(h, wq_ref[...]).astype(xq_ref.dtype)
    ext_ref[2 * H:2 * H + ts, :] = _dot(h, wu_ref[...])
    pos = j * ts + lax.broadcasted_iota(jnp.int32, (ts, 1), 0)
    n = ts + H
    for g, w in enumerate(POOL_WINDOWS):
        sl = slice(g * POOL_GROUP_DIM, (g + 1) * POOL_GROUP_DIM)
        src, cols, d, bufs = ext_ref, sl, 1, [a_ref, b_ref]
        while d < w:
            dst = bufs.pop(0)
            dst[H:H + n, :] = src[H:H + n, cols] + src[H - d:H - d + n, cols]
            bufs.append(dst)
            src, cols, d = dst, slice(None), 2 * d
        inv = 1.0 / jnp.minimum(pos + 1, w).astype(F32)
        pooled = (src[2 * H:2 * H + ts, cols] * inv - ext_ref[2 * H:2 * H + ts, sl]).astype(BF16)
        o_ref[:, sl] = (_dot(pooled, wg_ref[g]) * sc_ref[:, sl]).astype(o_ref.dtype)
    ext_ref[H:2 * H, :] = ext_ref[ts + H:ts + 2 * H, :]


def _pool_mix(x, g, w_in, w_group, scale):
    B, S, _ = x.shape
    ts = TOKEN_TILE
    tok = lambda n: pl.BlockSpec((None, ts, n), lambda b, j: (b, j, 0))
    return pl.pallas_call(
        functools.partial(_pool_kernel, ts=ts),
        grid=(B, S // ts),
        in_specs=[tok(D_MODEL), _resident((1, D_MODEL)), _resident((D_MODEL, D_MODEL)), _resident((D_MODEL, X_WIDTH)),
                  _resident(w_group.shape), _resident((1, D_MODEL))],
        out_specs=[tok(D_MODEL), tok(X_WIDTH)],
        out_shape=[jax.ShapeDtypeStruct((B, S, D_MODEL), BF16), jax.ShapeDtypeStruct((B, S, X_WIDTH), BF16)],
        scratch_shapes=[pltpu.VMEM((ts + 2 * POOL_HALO, D_MODEL), F32),
                        pltpu.VMEM((ts + 2 * POOL_HALO, POOL_GROUP_DIM), F32),
                        pltpu.VMEM((ts + 2 * POOL_HALO, POOL_GROUP_DIM), F32)],
        compiler_params=_params("parallel", "arbitrary"),
        name="pool_mix",
    )(x, g.reshape(1, D_MODEL), w_in[:, :D_MODEL].astype(BF16), w_in[:, D_MODEL:].astype(BF16),
      w_group.astype(BF16), scale.reshape(1, D_MODEL))


DIFF_SLOPE_TERMS = 3
DIFF_AUX_ROWS = 16


def _lane_features(lane, groups):
    out = jnp.zeros(lane.shape, F32)
    for g, v in enumerate(groups):
        for r in range(DIFF_SLOPE_TERMS):
            out = jnp.where(lane == DIFF_SLOPE_TERMS * g + r, v[r] if isinstance(v, list) else v, out)
    return out


def _diff_kernel(slopes_ref, lam_ref, q_ref, k_ref, vt_ref, ng_ref, o_ref,
                 ka_ref, qf_ref, acc_ref, m_ref, *, t, n_tiles, lam_init):
    hg = pl.program_id(1)
    qi = pl.program_id(2)
    hd = 2 * DIFF_HEAD_DIM
    lane = lax.broadcasted_iota(jnp.int32, (t, hd), 1)
    rowf = lax.broadcasted_iota(jnp.int32, (t, hd), 0).astype(F32)
    own = (lane < DIFF_HEAD_DIM, lane >= DIFF_HEAD_DIM)
    heads = range(DIFF_HEAD_BLOCK)
    c_terms = [[slopes_ref[hg * DIFF_HEAD_BLOCK + hb, r] for r in range(DIFF_SLOPE_TERMS)] for hb in heads]
    chains = [(hb, c) for hb in heads for c in range(2)]

    @pl.when(qi == 0)
    def _():
        def build(j, carry):
            start = pl.multiple_of(j * t, t)
            tile_idx = jnp.asarray(j, F32)
            for hb in heads:
                neg = [-cr for cr in c_terms[hb]]
                feat = _lane_features(lane, [rowf, tile_idx, neg, [cr * t for cr in neg]])
                ka_ref[hb, pl.ds(start, t), 0:hd] = k_ref[pl.ds(start, t), hb * hd:(hb + 1) * hd]
                ka_ref[hb, pl.ds(start, t), hd:2 * hd] = feat.astype(BF16)
            return carry

        lax.fori_loop(0, n_tiles, build, 0)
        for hb in heads:
            qf_ref[hb] = _lane_features(lane, [c_terms[hb], [cr * t for cr in c_terms[hb]], rowf, 0.0])

    qa = {}
    tile_lanes = (lane >= 3 * DIFF_SLOPE_TERMS) & (lane < 4 * DIFF_SLOPE_TERMS)
    for hb in heads:
        q = q_ref[:, hb * hd:(hb + 1) * hd]
        feat = jnp.where(tile_lanes, jnp.asarray(qi, F32), qf_ref[hb])
        for c in range(2):
            qa[hb, c] = jnp.concatenate([jnp.where(own[c], q, jnp.zeros_like(q)), feat.astype(BF16)], axis=1)
    acc_ref[...] = jnp.zeros_like(acc_ref)
    m_ref[...] = jnp.full(m_ref.shape, -jnp.inf, F32)

    def tile(kj, masked):
        start = pl.multiple_of(kj * t, t)
        scores = {}
        for hb, c in chains:
            s = _dot_nt(ka_ref[hb, pl.ds(start, t), :], qa[hb, c])
            if masked:
                key = lax.broadcasted_iota(jnp.int32, (t, t), 0)
                qry = lax.broadcasted_iota(jnp.int32, (t, t), 1)
                s = jnp.where(key <= qry, s, -jnp.inf)
            scores[hb, c] = s
        probs = {}
        for hb, c in chains:
            s = scores[hb, c]
            m_prev = m_ref[hb, c]
            m_new = jnp.maximum(m_prev, jnp.max(s, axis=0, keepdims=True))
            alpha = jnp.exp2(m_prev - m_new)
            p = jnp.exp2(s - m_new)
            m_ref[hb, c] = m_new
            probs[hb, c] = (alpha, p.astype(BF16))
        for hb, c in chains:
            alpha, p = probs[hb, c]
            acc_ref[hb, c] = alpha * acc_ref[hb, c] + _dot(vt_ref[hb, kj], p)

    def body(kj, carry):
        tile(kj, masked=False)
        return carry

    lax.fori_loop(0, qi, body, 0)
    tile(qi, masked=True)

    lp = lam_ref[...]
    lam = (jnp.exp(jnp.sum(lp[0:1] * lp[1:2], axis=-1, keepdims=True))
           - jnp.exp(jnp.sum(lp[2:3] * lp[3:4], axis=-1, keepdims=True)) + lam_init)
    for hb in range(DIFF_HEAD_BLOCK):
        a0, a1 = acc_ref[hb, 0], acc_ref[hb, 1]
        o_t = (a0[:hd] * (1.0 / a0[hd:hd + 1]) - lam * (a1[:hd] * (1.0 / a1[hd:hd + 1])))
        o_ref[:, hb * hd:(hb + 1) * hd] = (_rms(o_t.T, ng_ref[...], SUBLN_EPS) * (1.0 - lam_init)).astype(o_ref.dtype)


def _diff_mix(q, k, v_t, lam_p, norm_g, layer_idx):
    B, S, _ = q.shape
    t = ATTN_TILE
    n_tiles = S // t
    hd = 2 * DIFF_HEAD_DIM
    hb = DIFF_HEAD_BLOCK
    lam_init = 0.8 - 0.6 * math.exp(-0.3 * layer_idx)
    slopes = jnp.asarray([2.0 ** (-8.0 * (h + 1) / DIFF_HEADS) for h in range(DIFF_HEADS)], dtype=F32)
    rest = slopes * math.log2(math.e)
    terms = []
    for _ in range(DIFF_SLOPE_TERMS):
        terms.append(rest.astype(BF16).astype(F32))
        rest = rest - terms[-1]
    slopes = jnp.stack(terms, axis=1)
    qspec = pl.BlockSpec((None, t, hb * hd), lambda b, h, i: (b, i, h))
    return pl.pallas_call(
        functools.partial(_diff_kernel, t=t, n_tiles=n_tiles, lam_init=lam_init),
        grid=(B, DIFF_HEADS // hb, n_tiles),
        in_specs=[pl.BlockSpec(memory_space=pltpu.SMEM), _resident(lam_p.shape), qspec,
                  pl.BlockSpec((None, S, hb * hd), lambda b, h, i: (b, 0, h), pipeline_mode=pl.Buffered(1)),
                  pl.BlockSpec((None, hb, n_tiles, hd + DIFF_AUX_ROWS, t), lambda b, h, i: (b, h, 0, 0, 0),
                               pipeline_mode=pl.Buffered(1)),
                  _resident((1, hd))],
        out_specs=qspec,
        out_shape=jax.ShapeDtypeStruct((B, S, D_MODEL), BF16),
        scratch_shapes=[pltpu.VMEM((hb, S, 2 * hd), BF16), pltpu.VMEM((hb, t, hd), F32),
                        pltpu.VMEM((hb, 2, hd + DIFF_AUX_ROWS, t), F32),
                        pltpu.VMEM((hb, 2, 1, t), F32)],
        compiler_params=_params("parallel", "parallel", "arbitrary"),
        name="diff_attn",
    )(slopes, lam_p, q, k, v_t, norm_g.reshape(1, hd))


def _mlstm_kernel(qk_ref, v_ref, og_ref, gc_ref, gr_ref, cw_ref, gbc_ref, gbr_ref, ng_ref, o_ref,
                  ext_ref, c_ref, n_ref, m_ref, *, L):
    c = pl.program_id(1)
    W = 2 * D_MODEL

    @pl.when(c == 0)
    def _():
        ext_ref[0:CONV_HALO, :] = jnp.zeros((CONV_HALO, W), F32)
        c_ref[...] = jnp.zeros_like(c_ref)
        n_ref[...] = jnp.zeros_like(n_ref)
        m_ref[...] = jnp.zeros_like(m_ref)

    ext_ref[CONV_HALO:CONV_HALO + L, :] = qk_ref[...]
    conv = cw_ref[MLSTM_CONV - 1:MLSTM_CONV, :] * qk_ref[...]
    for j in range(1, MLSTM_CONV):
        conv = conv + (cw_ref[MLSTM_CONV - 1 - j:MLSTM_CONV - j, :]
                       * ext_ref[CONV_HALO - j:CONV_HALO - j + L, :])
    ext_ref[0:CONV_HALO, :] = ext_ref[L:L + CONV_HALO, :]
    qk = conv * _sigmoid(conv)

    gc = gc_ref[...] + gbc_ref[...]
    gr = gr_ref[...] + gbr_ref[...]
    b_c = _cumsum_rows(_log_sigmoid(gc))
    b_r = _cumsum_lanes(_log_sigmoid(gr))
    row = lax.broadcasted_iota(jnp.int32, (L, L), 0)
    col = lax.broadcasted_iota(jnp.int32, (L, L), 1)
    causal = col <= row

    for hd in range(MLSTM_HEADS):
        sl = slice(hd * MLSTM_HEAD_DIM, (hd + 1) * MLSTM_HEAD_DIM)
        q32 = qk[:, sl] * (MLSTM_HEAD_DIM ** -0.5)
        k32 = qk[:, D_MODEL + hd * MLSTM_HEAD_DIM:D_MODEL + (hd + 1) * MLSTM_HEAD_DIM]
        vh = v_ref[:, sl]
        fi = MLSTM_HEADS + hd
        bcol, icol = b_c[:, fi:fi + 1], gc[:, hd:hd + 1]
        brow, irow = b_r[fi:fi + 1, :], gr[hd:hd + 1, :]
        b_last = brow[:, L - 1:L]
        m = m_ref[hd:hd + 1, 0:1]
        cmat = c_ref[hd]
        nrow = n_ref[hd]

        dmat = jnp.where(causal, bcol - brow + irow, -jnp.inf)
        inter = bcol + m
        m_t = jnp.maximum(inter, jnp.max(dmat, axis=-1, keepdims=True))
        dec = jnp.exp(inter - m_t)
        qb = q32.astype(BF16)
        k_t = k32.T
        sqk = _dot(qb, k_t.astype(BF16)) * jnp.exp(dmat - m_t)
        num = dec * _dot(qb, cmat.astype(BF16)) + _dot(sqk.astype(BF16), vh)
        den = dec * jnp.sum(q32 * nrow, axis=-1, keepdims=True) + jnp.sum(sqk, axis=-1, keepdims=True)
        hc = num * (1.0 / jnp.maximum(jnp.abs(den), jnp.exp(-m_t)))

        gs_r = b_last - brow + irow
        gs_c = b_last - bcol + icol
        m_new = jnp.maximum(b_last + m, jnp.max(gs_r, axis=-1, keepdims=True))
        carry_dec = jnp.exp(b_last + m - m_new)
        c_ref[hd] = carry_dec * cmat + _dot((k_t * jnp.exp(gs_r - m_new)).astype(BF16), vh)
        n_ref[hd] = carry_dec * nrow + jnp.sum(k32 * jnp.exp(gs_c - m_new), axis=0, keepdims=True)
        m_ref[hd:hd + 1, :] = jnp.broadcast_to(m_new, (1, LANES))

        o_ref[:, sl] = (_rms(hc, ng_ref[:, sl], NORM_EPS) * _sigmoid(og_ref[:, sl])).astype(o_ref.dtype)


def _mlstm_mix(qk, v, og, gates, conv_w, gate_b, norm_g):
    B, S, _ = qk.shape
    L = MLSTM_CHUNK
    H = MLSTM_HEADS
    gates_r = jnp.transpose(gates[..., :2 * H], (0, 2, 1))
    gb = gate_b.reshape(2 * H)
    gb_c = jnp.pad(gb, (0, LANES - 2 * H)).reshape(1, LANES)
    gb_r = gb.reshape(2 * H, 1)
    tok = lambda n: pl.BlockSpec((None, L, n), lambda b, c: (b, c, 0))
    return pl.pallas_call(
        functools.partial(_mlstm_kernel, L=L),
        grid=(B, S // L),
        in_specs=[tok(2 * D_MODEL), tok(D_MODEL), tok(D_MODEL), tok(LANES),
                  pl.BlockSpec((None, 2 * H, L), lambda b, c: (b, 0, c)),
                  _resident(conv_w.shape), _resident((1, LANES)), _resident((2 * H, 1)), _resident((1, D_MODEL))],
        out_specs=tok(D_MODEL),
        out_shape=jax.ShapeDtypeStruct((B, S, D_MODEL), BF16),
        scratch_shapes=[pltpu.VMEM((L + CONV_HALO, 2 * D_MODEL), F32),
                        pltpu.VMEM((H, MLSTM_HEAD_DIM, MLSTM_HEAD_DIM), F32),
                        pltpu.VMEM((H, 1, MLSTM_HEAD_DIM), F32),
                        pltpu.VMEM((8, LANES), F32)],
        compiler_params=_params("parallel", "arbitrary"),
        name="mlstm",
    )(qk, v, og, gates, gates_r, conv_w, gb_c, gb_r, norm_g.reshape(1, D_MODEL))


def _gla_kernel(q_ref, k_ref, v_ref, g_ref, z_ref, w2_ref, gb_ref, ng_ref, o_ref, st_ref, *, L):
    c = pl.program_id(1)

    @pl.when(c == 0)
    def _():
        st_ref[...] = jnp.zeros_like(st_ref)

    log_a = _log_sigmoid(_dot(z_ref[...], w2_ref[...]) + gb_ref[...]) * (1.0 / GLA_TAU)
    b_all = _cumsum_rows(log_a) * LOG2E
    sub = GLA_SUBCHUNK
    row = lax.broadcasted_iota(jnp.int32, (sub, L), 0)
    col = lax.broadcasted_iota(jnp.int32, (sub, L), 1)
    ones = jnp.ones((GLA_KEY_DIM, LANES), BF16)

    for hd in range(GLA_HEADS):
        ks = slice(hd * GLA_KEY_DIM, (hd + 1) * GLA_KEY_DIM)
        vs = slice(hd * GLA_VALUE_DIM, (hd + 1) * GLA_VALUE_DIM)
        b = b_all[:, ks]
        vh = v_ref[:, vs]
        st = st_ref[hd]
        q = q_ref[:, ks] * (GLA_KEY_DIM ** -0.5)
        k = k_ref[:, ks]
        b_t = b.T
        k_t = k.T
        b_last = b_t[:, L - 1:L]
        blocks = []
        for i in range(L // sub):
            r0 = i * sub
            q_i, k_i, b_i = q[r0:r0 + sub], k[r0:r0 + sub], b[r0:r0 + sub]
            terms, offs = [], []
            for s in range(sub):
                t0 = s // SUBLANES * SUBLANES
                offs.append(sum(x.shape[0] for x in terms))
                terms.append(q_i[t0:] * k_i[s:s + 1] * jnp.exp2(b_i[t0:] - b_i[s:s + 1]))
            sums = _dot(jnp.concatenate(terms, axis=0).astype(BF16), ones)
            groups = []
            for t0 in range(0, sub, SUBLANES):
                a_g = jnp.zeros((SUBLANES, L), F32)
                for s in range(min(sub, t0 + SUBLANES)):
                    lo = offs[s] + t0 - s // SUBLANES * SUBLANES
                    a_g = jnp.where(col[:SUBLANES] == r0 + s, sums[lo:lo + SUBLANES], a_g)
                groups.append(a_g)
            a_i = jnp.where(col <= r0 + row, jnp.concatenate(groups, axis=0), 0.0)
            if i > 0:
                ref_row, ref_col = b[r0 - 1:r0], b_t[:, r0 - 1:r0]
                q_s = (q_i * jnp.exp2(b_i - ref_row)).astype(BF16)
                k_s = (k_t * jnp.exp2(ref_col - b_t)).astype(BF16)
                a_i = jnp.where(col < r0, _dot(q_s, k_s), a_i)
            blocks.append(a_i)
        amat = jnp.concatenate(blocks, axis=0)
        o = _dot((q * jnp.exp2(b)).astype(BF16), st.astype(BF16)) + _dot(amat.astype(BF16), vh)
        st_ref[hd] = jnp.exp2(b_last) * st + _dot((k_t * jnp.exp2(b_last - b_t)).astype(BF16), vh)
        gate = g_ref[:, vs]
        o_ref[:, vs] = (_rms(o, ng_ref[:, vs], NORM_EPS) * (gate * _sigmoid(gate))).astype(o_ref.dtype)


def _gla_mix(q, k, v, g, z, gate_w2, gate_b, norm_g):
    B, S, _ = q.shape
    L = GLA_CHUNK
    kw =GLA_HEADS * GLA_KEY_DIM
    w2 = jnp.pad(gate_w2, ((0, LANES - GLA_GATE_RANK), (0, 0))).astype(BF16)
    tok = lambda n: pl.BlockSpec((None, L, n), lambda b, c: (b, c, 0))
    return pl.pallas_call(
        functools.partial(_gla_kernel, L=L),
        grid=(B, S // L),
        in_specs=[tok(kw), tok(kw), tok(D_MODEL), tok(D_MODEL), tok(LANES),
                  _resident(w2.shape), _resident((1, kw)), _resident((1, D_MODEL))],
        out_specs=tok(D_MODEL),
        out_shape=jax.ShapeDtypeStruct((B, S, D_MODEL), BF16),
        scratch_shapes=[pltpu.VMEM((GLA_HEADS, GLA_KEY_DIM, GLA_VALUE_DIM), F32)],
        compiler_params=_params("parallel", "arbitrary"),
        name="gla",
    )(q, k, v, g, z, w2, gate_b.reshape(1, kw), norm_g.reshape(1, D_MODEL))


def kernel(x, mem, norm_g, ffn_w_up, ffn_w_down, mem_norm_g, mem_w_kv, pool_w_in, pool_w_group, pool_scale, pool_w_out, diff_w_in, diff_lambda, diff_norm_g, diff_w_out, mlstm_w_in, mlstm_conv_w, mlstm_gate_b, mlstm_norm_g, mlstm_w_out, gla_w_in, gla_gate_w2, gla_gate_b, gla_norm_g, gla_w_out, final_norm_g):
    B, S, D = x.shape
    M = mem.shape[1]
    T = B * S
    depth = norm_g.shape[0]
    n_mixers = 4
    mem2 = mem.reshape(B * M, D)
    x2 = x.reshape(T, D)
    norm_g4 = norm_g.reshape(depth, 3, 1, D)
    w_up, w_down = ffn_w_up[0, 0].astype(BF16), ffn_w_down[0, 0].astype(BF16)

    def seq(a):
        return a.reshape(B, S, a.shape[-1])

    for i in range(depth):
        kind, j = i % n_mixers, i // n_mixers
        k_scale = X_HEAD_DIM ** -0.5 * LOG2E
        mem_k, mem_v = _norm_proj(mem2, mem_norm_g, [(mem_w_kv[i][:, :X_WIDTH] * k_scale, BF16),
                                                     (mem_w_kv[i][:, X_WIDTH:], BF16)])
        mem_k = mem_k.reshape(B, M, X_WIDTH)
        mem_v = mem_v.reshape(B, M, X_WIDTH)
        x2, w_up, w_down = _ffn(x2, _ffn_weights(norm_g4, w_up, w_down, i, 0), (ffn_w_up, ffn_w_down, i, 1))
        if kind == 0:
            mix, xq = _pool_mix(seq(x2), norm_g[i, 1], pool_w_in[j], pool_w_group[j], pool_scale[j])
            w_out = pool_w_out[j]
        elif kind == 1:
            w = diff_w_in[j]
            q_scale = DIFF_HEAD_DIM ** -0.5 * math.log2(math.e)
            q, k, v_t, xq = _norm_proj(x2, norm_g[i, 1], [
                (w[:, :D] * q_scale, BF16), (w[:, D:2 * D], BF16), (w[:, 2 * D:3 * D], BF16, "values_t"),
                (w[:, 3 * D:], BF16)], seq_len=S)
            mix = _diff_mix(seq(q), seq(k), v_t, diff_lambda[j], diff_norm_g[j], i)
            w_out = diff_w_out[j]
        elif kind == 2:
            w = mlstm_w_in[j]
            ng = 2 * MLSTM_HEADS
            qk, v, og, gates, xq = _norm_proj(x2, norm_g[i, 1], [
                (w[:, :2 * D], F32), (w[:, 2 * D:3 * D], BF16), (w[:, 3 * D:4 * D], F32),
                (w[:, 4 * D:4 * D + ng], F32), (w[:, 4 * D + ng:], BF16)])
            mix = _mlstm_mix(seq(qk), seq(v), seq(og), seq(gates), mlstm_conv_w[j], mlstm_gate_b[j], mlstm_norm_g[j])
            w_out = mlstm_w_out[j]
        else:
            w = gla_w_in[j]
            kw = GLA_HEADS * GLA_KEY_DIM
            o0 = 2 * kw + 2 * D
            q, k, v, g, z, xq = _norm_proj(x2, norm_g[i, 1], [
                (w[:, :kw], F32), (w[:, kw:2 * kw], F32), (w[:, 2 * kw:2 * kw + D], BF16),
                (w[:, 2 * kw + D:o0], F32), (w[:, o0:o0 + GLA_GATE_RANK], BF16), (w[:, o0 + GLA_GATE_RANK:], BF16)])
            mix = _gla_mix(seq(q), seq(k), seq(v), seq(g), seq(z), gla_gate_w2[j], gla_gate_b[j], gla_norm_g[j])
            w_out = gla_w_out[j]
        last = i == depth - 1
        x3, w_up, w_down = _attn_out_ffn(seq(x2), mix, seq(xq), mem_k, mem_v, w_out,
                                         _ffn_weights(norm_g4, w_up, w_down, i, 2),
                                         final_g=final_norm_g if last else None,
                                         next_cast=None if last else (ffn_w_up, ffn_w_down, i + 1, 0))
        x2 = x3.reshape(T, D)
    return x2.reshape(B, S, D)
```

```python
import functools
import math

import jax
import jax.numpy as jnp
from jax import lax
from jax.experimental import pallas as pl
from jax.experimental.pallas import tpu as pltpu

F32 = jnp.float32
BF16 = jnp.bfloat16

D_MODEL = 1024
D_FF = 2816
X_HEADS = 4
X_HEAD_DIM = 128
X_WIDTH = X_HEADS * X_HEAD_DIM
POOL_WINDOWS = (2, 4, 8, 16)
POOL_GROUP_DIM = D_MODEL // len(POOL_WINDOWS)
POOL_HALO = 16
DIFF_HEAD_DIM = 64
DIFF_HEADS = 8
DIFF_HEAD_BLOCK = 8
MLSTM_HEADS = 4
MLSTM_HEAD_DIM = 256
MLSTM_CONV = 4
CONV_HALO = 8
GLA_HEADS = 4
GLA_KEY_DIM = 128
GLA_VALUE_DIM = 256
GLA_GATE_RANK = 16
GLA_TAU = 16.0
GLA_SUBCHUNK = 16
LOG2E = math.log2(math.e)
NORM_EPS = 1e-6
SUBLN_EPS = 1e-5
LANES = 128
SUBLANES = 8

VMEM_LIMIT = 56 * 1024 * 1024
TOKEN_TILE = 512
FF_CHUNKS = (0, 1536, D_FF)
FFN_TOKEN_TILE = 1024
FFN_WIDE_CHUNKS = (0, 768, 1536, 2304, D_FF)
ATTN_TILE = 256
MLSTM_CHUNK = 256
GLA_CHUNK = 128


def _params(*sem):
    return pltpu.CompilerParams(dimension_semantics=sem, vmem_limit_bytes=VMEM_LIMIT)


def _resident(shape):
    nd = len(shape)
    return pl.BlockSpec(shape, lambda *_: (0,) * nd, pipeline_mode=pl.Buffered(1))


def _rms(x, g, eps):
    return x * lax.rsqrt(jnp.mean(x * x, axis=-1, keepdims=True) + eps) * g


def _sigmoid(x):
    return 1.0 / (1.0 + jnp.exp(-x))


def _log_sigmoid(x):
    return jnp.minimum(x, 0.0) - jnp.log(1.0 + jnp.exp(-jnp.abs(x)))


def _dot(a, b):
    return jnp.dot(a, b, preferred_element_type=F32)


def _dot_nt(a, b):
    return lax.dot_general(a, b, (((1,), (1,)), ((), ())), preferred_element_type=F32)


def _split3(x):
    hi = x.astype(BF16)
    r1 = x - hi.astype(F32)
    mid = r1.astype(BF16)
    lo = (r1 - mid.astype(F32)).astype(BF16)
    return hi, mid, lo


def _tri(n, upper):
    r = lax.broadcasted_iota(jnp.int32, (n, n), 0)
    c = lax.broadcasted_iota(jnp.int32, (n, n), 1)
    keep = (r <= c) if upper else (c <= r)
    return jnp.where(keep, 1.0, 0.0).astype(BF16)


def _cumsum_rows(x):
    tri = _tri(x.shape[0], upper=False)
    hi, mid, lo = _split3(x)
    return _dot(tri, hi) + _dot(tri, mid) + _dot(tri, lo)


def _cumsum_lanes(x):
    tri = _tri(x.shape[1], upper=True)
    hi, mid, lo = _split3(x)
    return _dot(hi, tri) + _dot(mid, tri) + _dot(lo, tri)


def _ffn_half_step(x, g_ref, wg_ref, wu_ref, wd_ref, chunks=FF_CHUNKS):
    h = _rms(x, g_ref[...], NORM_EPS).astype(BF16)
    acc = jnp.zeros_like(x)
    for lo, hi in zip(chunks[:-1], chunks[1:]):
        sl = slice(lo, hi)
        gate = _dot(h, wg_ref[:, sl])
        up = _dot(h, wu_ref[:, sl])
        act = (gate * _sigmoid(gate) * up).astype(BF16)
        acc = acc + _dot(act, wd_ref[sl, :])
    return x + 0.5 * acc


def _stacked(shape, lead, tail=None):
    tail = tail or (0,) * len(shape)
    return pl.BlockSpec((None,) * len(lead) + shape, lambda *_: lead + tail, pipeline_mode=pl.Buffered(1))


def _ffn_weights(norm_g, w_up, w_down, layer, norm_slot):
    args = [norm_g, w_up, w_up, w_down]
    specs = [_stacked((1, D_MODEL), (layer, norm_slot)), _stacked((D_MODEL, D_FF), (), (0, 0)),
             _stacked((D_MODEL, D_FF), (), (0, 1)), _stacked((D_FF, D_MODEL), ())]
    return args, specs


def _next_weights_cast(w_up_all, w_down_all, layer, slot, step_of, n_steps):
    args, in_specs, out_specs, out_shapes = [w_up_all, w_down_all], [], [], []
    for w in args:
        n_rows, n_cols = w.shape[2:]
        n_slabs = max(s for s in range(1, n_steps + 1) if n_rows % (16 * s) == 0)
        slab = lambda *idx, n_slabs=n_slabs: jnp.minimum(step_of(*idx), n_slabs - 1)
        in_specs.append(pl.BlockSpec((None, None, n_rows // n_slabs, n_cols),
                                     lambda *idx, slab=slab: (layer, slot, slab(*idx), 0)))
        out_specs.append(pl.BlockSpec((n_rows // n_slabs, n_cols), lambda *idx, slab=slab: (slab(*idx), 0)))
        out_shapes.append(jax.ShapeDtypeStruct((n_rows, n_cols), BF16))
    return args, in_specs, out_specs, out_shapes


def _ffn_kernel(x_ref, g_ref, wg_ref, wu_ref, wd_ref, nu_ref, nd_ref, o_ref, nu_out_ref, nd_out_ref):
    o_ref[...] = _ffn_half_step(x_ref[...], g_ref, wg_ref, wu_ref, wd_ref, FFN_WIDE_CHUNKS)
    nu_out_ref[...] = nu_ref[...].astype(BF16)
    nd_out_ref[...] = nd_ref[...].astype(BF16)


def _ffn(x, ffn_weights, next_cast):
    T = x.shape[0]
    w_args, w_specs = ffn_weights
    n_steps = T // FFN_TOKEN_TILE
    c_args, c_in, c_out, c_shapes = _next_weights_cast(*next_cast, step_of=lambda i: i, n_steps=n_steps)
    tok = pl.BlockSpec((FFN_TOKEN_TILE, D_MODEL), lambda i: (i, 0))
    y, w_up, w_down = pl.pallas_call(
        _ffn_kernel,
        grid=(n_steps,),
        in_specs=[tok] + w_specs + c_in,
        out_specs=[tok] + c_out,
        out_shape=[jax.ShapeDtypeStruct((T, D_MODEL), F32)] + c_shapes,
        compiler_params=_params("parallel"),
        name="ffn",
    )(x, *w_args, *c_args)
    return y, w_up, w_down


def _norm_proj_kernel(x_ref, g_ref, *refs, n_out):
    h = _rms(x_ref[...], g_ref[...], NORM_EPS).astype(BF16)
    for w_ref, o_ref in zip(refs[:n_out], refs[n_out:]):
        y = _dot(h, w_ref[...])
        if len(o_ref.shape) == 2:
            o_ref[...] = y.astype(o_ref.dtype)
            continue
        n_heads, n_sub, rows, t = o_ref.shape
        hd = rows - DIFF_AUX_ROWS
        aux = jnp.where(lax.broadcasted_iota(jnp.int32, (DIFF_AUX_ROWS, t), 0) == 0, 1.0, 0.0).astype(o_ref.dtype)
        for head in range(n_heads):
            for s in range(n_sub):
                o_ref[head, s, 0:hd, :] = y[s * t:(s + 1) * t, head * hd:(head + 1) * hd].T.astype(o_ref.dtype)
                o_ref[head, s, hd:rows, :] = aux


def _norm_proj(x, g, pieces, seq_len=None):
    T = x.shape[0]
    ws, out_specs, out_shapes = [], [], []
    for w, dt, *layout in pieces:
        n = w.shape[1]
        n_pad = -(-n // LANES) * LANES
        if n_pad != n:
            w = jnp.pad(w, ((0, 0), (0, n_pad - n)))
        ws.append(w.astype(BF16))
        if layout:
            hd, t, per_seq = 2 * DIFF_HEAD_DIM, ATTN_TILE, seq_len // TOKEN_TILE
            blk = (None, n // hd, TOKEN_TILE // t, hd + DIFF_AUX_ROWS, t)
            out_specs.append(pl.BlockSpec(blk, lambda i: (i // per_seq, 0, i % per_seq, 0, 0)))
            out_shapes.append(jax.ShapeDtypeStruct((T // seq_len, n // hd, seq_len // t) + blk[3:], dt))
            continue
        out_specs.append(pl.BlockSpec((TOKEN_TILE, n_pad), lambda i: (i, 0)))
        out_shapes.append(jax.ShapeDtypeStruct((T, n_pad), dt))
    in_specs = [pl.BlockSpec((TOKEN_TILE, D_MODEL), lambda i: (i, 0)), _resident((1, D_MODEL))]
    in_specs += [_resident(w.shape) for w in ws]
    return pl.pallas_call(
        functools.partial(_norm_proj_kernel, n_out=len(ws)),
        grid=(T // TOKEN_TILE,),
        in_specs=in_specs,
        out_specs=out_specs,
        out_shape=out_shapes,
        compiler_params=_params("parallel"),
        name="norm_proj",
    )(x, g.reshape(1, D_MODEL), *ws)


def _attn_out_kernel(x_ref, mix_ref, xq_ref, mk_ref, mv_ref, w1_ref, w2_ref, g_ref, wg_ref, wu_ref, wd_ref,
                     *rest, final):
    xq = xq_ref[...]
    mk = mk_ref[...]
    mv = mv_ref[...]
    outs = []
    for h in range(X_HEADS):
        sl = slice(h * X_HEAD_DIM, (h + 1) * X_HEAD_DIM)
        s = _dot_nt(xq[:, sl], mk[:, sl])
        e = jnp.exp2(s - jnp.max(s, axis=-1, keepdims=True))
        o = _dot(e.astype(BF16), mv[:, sl]) * (1.0 / jnp.sum(e, axis=-1, keepdims=True))
        outs.append(o.astype(BF16))
    xo = jnp.concatenate(outs, axis=-1)
    x = x_ref[...] + _dot(mix_ref[...], w1_ref[...]) + _dot(xo, w2_ref[...])
    y = _ffn_half_step(x, g_ref, wg_ref, wu_ref, wd_ref)
    if final:
        fg_ref, o_ref = rest
        o_ref[...] = _rms(y, fg_ref[...], NORM_EPS)
    else:
        nu_ref, nd_ref, o_ref, nu_out_ref, nd_out_ref = rest
        o_ref[...] = y
        nu_out_ref[...] = nu_ref[...].astype(BF16)
        nd_out_ref[...] = nd_ref[...].astype(BF16)


def _attn_out_ffn(x, mix, xq, mem_k, mem_v, w_out, ffn_weights, final_g=None, next_cast=None):
    B, S, _ = x.shape
    M = mem_k.shape[1]
    w1 = w_out[:D_MODEL].astype(BF16)
    w2 = w_out[D_MODEL:].astype(BF16)
    w_args, w_specs = ffn_weights
    tok = lambda n: pl.BlockSpec((None, TOKEN_TILE, n), lambda b, i: (b, i, 0))
    mem = pl.BlockSpec((None, M, X_WIDTH), lambda b, i: (b, 0, 0))
    out_specs, out_shapes = [tok(D_MODEL)], [jax.ShapeDtypeStruct((B, S, D_MODEL), F32)]
    per_seq = S // TOKEN_TILE
    if final_g is not None:
        w_args = w_args + [final_g.reshape(1, D_MODEL)]
        w_specs = w_specs + [_resident((1, D_MODEL))]
    else:
        c_args, c_in, c_out, c_shapes = _next_weights_cast(*next_cast, step_of=lambda b, i: b * per_seq + i,
                                                           n_steps=B * per_seq)
        w_args, w_specs = w_args + c_args, w_specs + c_in
        out_specs, out_shapes = out_specs + c_out, out_shapes + c_shapes
    outs = pl.pallas_call(
        functools.partial(_attn_out_kernel, final=final_g is not None),
        grid=(B, per_seq),
        in_specs=[tok(D_MODEL), tok(D_MODEL), tok(X_WIDTH), mem, mem, _resident(w1.shape), _resident(w2.shape)]
        + w_specs,
        out_specs=out_specs,
        out_shape=out_shapes,
        compiler_params=_params("parallel", "parallel"),
        name="attn_out_ffn",
    )(x, mix, xq, mem_k, mem_v, w1, w2, *w_args)
    if final_g is not None:
        return outs[0], None, None
    return outs[0], outs[1], outs[2]


def _pool_kernel(x_ref, g_ref, wu_ref, wq_ref, wg_ref, sc_ref, o_ref, xq_ref, ext_ref, a_ref, b_ref, *, ts):
    j = pl.program_id(1)
    H = POOL_HALO

    @pl.when(j == 0)
    def _():
        ext_ref[0:2 * H, :] = jnp.zeros((2 * H, D_MODEL), F32)
        a_ref[0:H, :] = jnp.zeros((H, POOL_GROUP_DIM), F32)
        b_ref[0:H, :] = jnp.zeros((H, POOL_GROUP_DIM), F32)

    h = _rms(x_ref[...], g_ref[...], NORM_EPS).astype(BF16)
    xq_ref[...] = _dot(h, wq_ref[...]).astype(xq_ref.dtype)
    ext_ref[2 * H:2 * H + ts, :] = _dot(h, wu_ref[...])
    pos = j * ts + lax.broadcasted_iota(jnp.int32, (ts, 1), 0)
    n = ts + H
    for g, w in enumerate(POOL_WINDOWS):
        sl = slice(g * POOL_GROUP_DIM, (g + 1) * POOL_GROUP_DIM)
        src, cols, d, bufs = ext_ref, sl, 1, [a_ref, b_ref]
        while d < w:
            dst = bufs.pop(0)
            dst[H:H + n, :] = src[H:H + n, cols] + src[H - d:H - d + n, cols]
            bufs.append(dst)
            src, cols, d = dst, slice(None), 2 * d
        inv = 1.0 / jnp.minimum(pos + 1, w).astype(F32)
        pooled = (src[2 * H:2 * H + ts, cols] * inv - ext_ref[2 * H:2 * H + ts, sl]).astype(BF16)
        o_ref[:, sl] = (_dot(pooled, wg_ref[g]) * sc_ref[:, sl]).astype(o_ref.dtype)
    ext_ref[H:2 * H, :] = ext_ref[ts + H:ts + 2 * H, :]


def _pool_mix(x, g, w_in, w_group, scale):
    B, S, _ = x.shape
    ts = TOKEN_TILE
    tok = lambda n: pl.BlockSpec((None, ts, n), lambda b, j: (b, j, 0))
    return pl.pallas_call(
        functools.partial(_pool_kernel, ts=ts),
        grid=(B, S // ts),
        in_specs=[tok(D_MODEL), _resident((1, D_MODEL)), _resident((D_MODEL, D_MODEL)), _resident((D_MODEL, X_WIDTH)),
                  _resident(w_group.shape), _resident((1, D_MODEL))],
        out_specs=[tok(D_MODEL), tok(X_WIDTH)],
        out_shape=[jax.ShapeDtypeStruct((B, S, D_MODEL), BF16), jax.ShapeDtypeStruct((B, S, X_WIDTH), BF16)],
        scratch_shapes=[pltpu.VMEM((ts + 2 * POOL_HALO, D_MODEL), F32),
                        pltpu.VMEM((ts + 2 * POOL_HALO, POOL_GROUP_DIM), F32),
                        pltpu.VMEM((ts + 2 * POOL_HALO, POOL_GROUP_DIM), F32)],
        compiler_params=_params("parallel", "arbitrary"),
        name="pool_mix",
    )(x, g.reshape(1, D_MODEL), w_in[:, :D_MODEL].astype(BF16), w_in[:, D_MODEL:].astype(BF16),
      w_group.astype(BF16), scale.reshape(1, D_MODEL))


DIFF_SLOPE_TERMS = 3
DIFF_AUX_ROWS = 16


def _lane_features(lane, groups):
    out = jnp.zeros(lane.shape, F32)
    for g, v in enumerate(groups):
        for r in range(DIFF_SLOPE_TERMS):
            out = jnp.where(lane == DIFF_SLOPE_TERMS * g + r, v[r] if isinstance(v, list) else v, out)
    return out


def _diff_kernel(slopes_ref, lam_ref, q_ref, k_ref, vt_ref, ng_ref, o_ref,
                 ka_ref, qf_ref, acc_ref, m_ref, *, t, n_tiles, lam_init):
    hg = pl.program_id(1)
    qi = pl.program_id(2)
    hd = 2 * DIFF_HEAD_DIM
    lane = lax.broadcasted_iota(jnp.int32, (t, hd), 1)
    rowf = lax.broadcasted_iota(jnp.int32, (t, hd), 0).astype(F32)
    own = (lane < DIFF_HEAD_DIM, lane >= DIFF_HEAD_DIM)
    heads = range(DIFF_HEAD_BLOCK)
    c_terms = [[slopes_ref[hg * DIFF_HEAD_BLOCK + hb, r] for r in range(DIFF_SLOPE_TERMS)] for hb in heads]
    chains = [(hb, c) for hb in heads for c in range(2)]

    @pl.when(qi == 0)
    def _():
        key_feat = []
        for hb in heads:
            neg = [-cr for cr in c_terms[hb]]
            key_feat.append(_lane_features(lane, [rowf, 0.0, neg, [cr * t for cr in neg]]))
        key_tile_lanes = (lane >= DIFF_SLOPE_TERMS) & (lane < 2 * DIFF_SLOPE_TERMS)

        def build(j, carry):
            start = pl.multiple_of(j * t, t)
            for hb in heads:
                feat = jnp.where(key_tile_lanes, jnp.asarray(j, F32), key_feat[hb])
                ka_ref[hb, pl.ds(start, t), 0:hd] = k_ref[pl.ds(start, t), hb * hd:(hb + 1) * hd]
                ka_ref[hb, pl.ds(start, t), hd:2 * hd] = feat.astype(BF16)
            return carry

        lax.fori_loop(0, n_tiles, build, 0)
        for hb in heads:
            qf_ref[hb] = _lane_features(lane, [c_terms[hb], [cr * t for cr in c_terms[hb]], rowf, 0.0])

    qa = {}
    tile_lanes = (lane >= 3 * DIFF_SLOPE_TERMS) & (lane < 4 * DIFF_SLOPE_TERMS)
    for hb in heads:
        q = q_ref[:, hb * hd:(hb + 1) * hd]
        feat = jnp.where(tile_lanes, jnp.asarray(qi, F32), qf_ref[hb])
        for c in range(2):
            qa[hb, c] = jnp.concatenate([jnp.where(own[c], q, jnp.zeros_like(q)), feat.astype(BF16)], axis=1)
    acc_ref[...] = jnp.zeros_like(acc_ref)
    m_ref[...] = jnp.full(m_ref.shape, -jnp.inf, F32)

    def tile(kj, masked):
        start = pl.multiple_of(kj * t, t)
        scores = {}
        for hb, c in chains:
            s = _dot_nt(ka_ref[hb, pl.ds(start, t), :], qa[hb, c])
            if masked:
                key = lax.broadcasted_iota(jnp.int32, (t, t), 0)
                qry = lax.broadcasted_iota(jnp.int32, (t, t), 1)
                s = jnp.where(key <= qry, s, -jnp.inf)
            scores[hb, c] = s
        probs = {}
        for hb, c in chains:
            s = scores[hb, c]
            m_prev = m_ref[hb, c]
            m_new = jnp.maximum(m_prev, jnp.max(s, axis=0, keepdims=True))
            alpha = jnp.exp2(m_prev - m_new)
            p = jnp.exp2(s - m_new)
            m_ref[hb, c] = m_new
            probs[hb, c] = (alpha, p.astype(BF16))
        for hb, c in chains:
            alpha, p = probs[hb, c]
            acc_ref[hb, c] = alpha * acc_ref[hb, c] + _dot(vt_ref[hb, kj], p)

    def body(kj, carry):
        tile(kj, masked=False)
        return carry

    lax.fori_loop(0, qi, body, 0)
    tile(qi, masked=True)

    lp = lam_ref[...]
    lam = (jnp.exp(jnp.sum(lp[0:1] * lp[1:2], axis=-1, keepdims=True))
           - jnp.exp(jnp.sum(lp[2:3] * lp[3:4], axis=-1, keepdims=True)) + lam_init)
    for hb in range(DIFF_HEAD_BLOCK):
        a0, a1 = acc_ref[hb, 0], acc_ref[hb, 1]
        o_t = (a0[:hd] * (1.0 / a0[hd:hd + 1]) - lam * (a1[:hd] * (1.0 / a1[hd:hd + 1])))
        o_ref[:, hb * hd:(hb + 1) * hd] = (_rms(o_t.T, ng_ref[...], SUBLN_EPS) * (1.0 - lam_init)).astype(o_ref.dtype)


def _diff_mix(q, k, v_t, lam_p, norm_g, layer_idx):
    B, S, _ = q.shape
    t = ATTN_TILE
    n_tiles = S // t
    hd = 2 * DIFF_HEAD_DIM
    hb = DIFF_HEAD_BLOCK
    lam_init = 0.8 - 0.6 * math.exp(-0.3 * layer_idx)
    slopes = jnp.asarray([2.0 ** (-8.0 * (h + 1) / DIFF_HEADS) for h in range(DIFF_HEADS)], dtype=F32)
    rest = slopes * math.log2(math.e)
    terms = []
    for _ in range(DIFF_SLOPE_TERMS):
        terms.append(rest.astype(BF16).astype(F32))
        rest = rest - terms[-1]
    slopes = jnp.stack(terms, axis=1)
    qspec = pl.BlockSpec((None, t, hb * hd), lambda b, h, i: (b, i, h))
    return pl.pallas_call(
        functools.partial(_diff_kernel, t=t, n_tiles=n_tiles, lam_init=lam_init),
        grid=(B, DIFF_HEADS // hb, n_tiles),
        in_specs=[pl.BlockSpec(memory_space=pltpu.SMEM), _resident(lam_p.shape), qspec,
                  pl.BlockSpec((None, S, hb * hd), lambda b, h, i: (b, 0, h), pipeline_mode=pl.Buffered(1)),
                  pl.BlockSpec((None, hb, n_tiles, hd + DIFF_AUX_ROWS, t), lambda b, h, i: (b, h, 0, 0, 0),
                               pipeline_mode=pl.Buffered(1)),
                  _resident((1, hd))],
        out_specs=qspec,
        out_shape=jax.ShapeDtypeStruct((B, S, D_MODEL), BF16),
        scratch_shapes=[pltpu.VMEM((hb, S, 2 * hd), BF16), pltpu.VMEM((hb, t, hd), F32),
                        pltpu.VMEM((hb, 2, hd + DIFF_AUX_ROWS, t), F32),
                        pltpu.VMEM((hb, 2, 1, t), F32)],
        compiler_params=_params("parallel", "parallel", "arbitrary"),
        name="diff_attn",
    )(slopes, lam_p, q, k, v_t, norm_g.reshape(1, hd))


def _mlstm_kernel(qk_ref, v_ref, og_ref, gc_ref, gr_ref, cw_ref, gbc_ref, gbr_ref, ng_ref, o_ref,
                  ext_ref, c_ref, n_ref, m_ref, *, L):
    c = pl.program_id(1)
    W = 2 * D_MODEL

    @pl.when(c == 0)
    def _():
        ext_ref[0:CONV_HALO, :] = jnp.zeros((CONV_HALO, W), F32)
        c_ref[...] = jnp.zeros_like(c_ref)
        n_ref[...] = jnp.zeros_like(n_ref)
        m_ref[...] = jnp.zeros_like(m_ref)

    ext_ref[CONV_HALO:CONV_HALO + L, :] = qk_ref[...]
    conv = cw_ref[MLSTM_CONV - 1:MLSTM_CONV, :] * qk_ref[...]
    for j in range(1, MLSTM_CONV):
        conv = conv + (cw_ref[MLSTM_CONV - 1 - j:MLSTM_CONV - j, :]
                       * ext_ref[CONV_HALO - j:CONV_HALO - j + L, :])
    ext_ref[0:CONV_HALO, :] = ext_ref[L:L + CONV_HALO, :]
    qk = conv * _sigmoid(conv)

    gc = gc_ref[...] + gbc_ref[...]
    gr = gr_ref[...] + gbr_ref[...]
    b_c = _cumsum_rows(_log_sigmoid(gc))
    b_r = _cumsum_lanes(_log_sigmoid(gr))
    row = lax.broadcasted_iota(jnp.int32, (L, L), 0)
    col = lax.broadcasted_iota(jnp.int32, (L, L), 1)
    causal = col <= row

    for hd in range(MLSTM_HEADS):
        sl = slice(hd * MLSTM_HEAD_DIM, (hd + 1) * MLSTM_HEAD_DIM)
        q32 = qk[:, sl] * (MLSTM_HEAD_DIM ** -0.5)
        k32 = qk[:, D_MODEL + hd * MLSTM_HEAD_DIM:D_MODEL + (hd + 1) * MLSTM_HEAD_DIM]
        vh = v_ref[:, sl]
        fi = MLSTM_HEADS + hd
        bcol, icol = b_c[:, fi:fi + 1], gc[:, hd:hd + 1]
        brow, irow = b_r[fi:fi + 1, :], gr[hd:hd + 1, :]
        b_last = brow[:, L - 1:L]
        m = m_ref[hd:hd + 1, 0:1]
        cmat = c_ref[hd]
        nrow = n_ref[hd]

        dmat = jnp.where(causal, bcol - brow + irow, -jnp.inf)
        inter = bcol + m
        m_t = jnp.maximum(inter, jnp.max(dmat, axis=-1, keepdims=True))
        dec = jnp.exp(inter - m_t)
        qb = q32.astype(BF16)
        k_t = k32.T
        sqk = _dot(qb, k_t.astype(BF16)) * jnp.exp(dmat - m_t)
        num = dec * _dot(qb, cmat.astype(BF16)) + _dot(sqk.astype(BF16), vh)
        den = dec * jnp.sum(q32 * nrow, axis=-1, keepdims=True) + jnp.sum(sqk, axis=-1, keepdims=True)
        hc = num * (1.0 / jnp.maximum(jnp.abs(den), jnp.exp(-m_t)))

        gs_r = b_last - brow + irow
        gs_c = b_last - bcol + icol
        m_new = jnp.maximum(b_last + m, jnp.max(gs_r, axis=-1, keepdims=True))
        carry_dec = jnp.exp(b_last + m - m_new)
        c_ref[hd] = carry_dec * cmat + _dot((k_t * jnp.exp(gs_r - m_new)).astype(BF16), vh)
        n_ref[hd] = carry_dec * nrow + jnp.sum(k32 * jnp.exp(gs_c - m_new), axis=0, keepdims=True)
        m_ref[hd:hd + 1, :] = jnp.broadcast_to(m_new, (1, LANES))

        o_ref[:, sl] = (_rms(hc, ng_ref[:, sl], NORM_EPS) * _sigmoid(og_ref[:, sl])).astype(o_ref.dtype)


def _mlstm_mix(qk, v, og, gates, conv_w, gate_b, norm_g):
    B, S, _ = qk.shape
    L = MLSTM_CHUNK
    H = MLSTM_HEADS
    gates_r = jnp.transpose(gates[..., :2 * H], (0, 2, 1))
    gb = gate_b.reshape(2 * H)
    gb_c = jnp.pad(gb, (0, LANES - 2 * H)).reshape(1, LANES)
    gb_r = gb.reshape(2 * H, 1)
    tok = lambda n: pl.BlockSpec((None, L, n), lambda b, c: (b, c, 0))
    return pl.pallas_call(
        functools.partial(_mlstm_kernel, L=L),
        grid=(B, S // L),
        in_specs=[tok(2 * D_MODEL), tok(D_MODEL), tok(D_MODEL), tok(LANES),
                  pl.BlockSpec((None, 2 * H, L), lambda b, c: (b, 0, c)),
                  _resident(conv_w.shape), _resident((1, LANES)), _resident((2 * H, 1)), _resident((1, D_MODEL))],
        out_specs=tok(D_MODEL),
        out_shape=jax.ShapeDtypeStruct((B, S, D_MODEL), BF16),
        scratch_shapes=[pltpu.VMEM((L + CONV_HALO, 2 * D_MODEL), F32),
                        pltpu.VMEM((H, MLSTM_HEAD_DIM, MLSTM_HEAD_DIM), F32),
                        pltpu.VMEM((H, 1, MLSTM_HEAD_DIM), F32),
                        pltpu.VMEM((8, LANES), F32)],
        compiler_params=_params("parallel", "arbitrary"),
        name="mlstm",
    )(qk, v, og, gates, gates_r, conv_w, gb_c, gb_r, norm_g.reshape(1, D_MODEL))


def _gla_kernel(q_ref, k_ref, v_ref, g_ref, z_ref, w2_ref, gb_ref, ng_ref, o_ref, st_ref, *, L):
    c = pl.program_id(1)

    @pl.when(c == 0)
    def _():
        st_ref[...] = jnp.zeros_like(st_ref)

    log_a = _log_sigmoid(_dot(z_ref[...], w2_ref[...]) + gb_ref[...]) * (1.0 / GLA_TAU)
    b_all = _cumsum_rows(log_a) * LOG2E
    sub = GLA_SUBCHUNK
    row = lax.broadcasted_iota(jnp.int32, (sub, L), 0)
    col = lax.broadcasted_iota(jnp.int32, (sub, L), 1)
    ones = jnp.ones((GLA_KEY_DIM, LANES), BF16)

    for hd in range(GLA_HEADS):
        ks = slice(hd * GLA_KEY_DIM, (hd + 1) * GLA_KEY_DIM)
        vs = slice(hd * GLA_VALUE_DIM, (hd + 1) * GLA_VALUE_DIM)
        b = b_all[:, ks]
        vh = v_ref[:, vs]
        st = st_ref[hd]
        q = q_ref[:, ks] * (GLA_KEY_DIM ** -0.5)
        k = k_ref[:, ks]
        b_t = b.T
        k_t = k.T
        b_last = b_t[:, L - 1:L]
        blocks = []
        for i in range(L // sub):
            r0 = i * sub
            q_i, k_i, b_i = q[r0:r0 + sub], k[r0:r0 + sub], b[r0:r0 + sub]
            terms, offs = [], []
            for s in range(sub):
                t0 = s // SUBLANES * SUBLANES
                offs.append(sum(x.shape[0] for x in terms))
                terms.append(q_i[t0:] * k_i[s:s + 1] * jnp.exp2(b_i[t0:] - b_i[s:s + 1]))
            sums = _dot(jnp.concatenate(terms, axis=0).astype(BF16), ones)
            groups = []
            for t0 in range(0, sub, SUBLANES):
                a_g = jnp.zeros((SUBLANES, L), F32)
                for s in range(min(sub, t0 + SUBLANES)):
                    lo = offs[s] + t0 - s // SUBLANES * SUBLANES
                    a_g = jnp.where(col[:SUBLANES] == r0 + s, sums[lo:lo + SUBLANES], a_g)
                groups.append(a_g)
            a_i = jnp.where(col <= r0 + row, jnp.concatenate(groups, axis=0), 0.0)
            if i > 0:
                ref_row, ref_col = b[r0 - 1:r0], b_t[:, r0 - 1:r0]
                q_s = (q_i * jnp.exp2(b_i - ref_row)).astype(BF16)
                k_s = (k_t * jnp.exp2(ref_col - b_t)).astype(BF16)
                a_i = jnp.where(col < r0, _dot(q_s, k_s), a_i)
            blocks.append(a_i)
        amat = jnp.concatenate(blocks, axis=0)
        o = _dot((q * jnp.exp2(b)).astype(BF16), st.astype(BF16)) + _dot(amat.astype(BF16), vh)
        st_ref[hd] = jnp.exp2(b_last) * st + _dot((k_t * jnp.exp2(b_last - b_t)).astype(BF16), vh)
        gate = g_ref[:, vs]
        o_ref[:, vs] = (_rms(o, ng_ref[:, vs], NORM_EPS) * (gate * _sigmoid(gate))).astype(o_ref.dtype)


def _gla_mix(q, k, v, g, z, gate_w2, gate_b, norm_g):
    B, S, _ = q.shape
    L = GLA_CHUNK
    kw =GLA_HEADS * GLA_KEY_DIM
    w2 = jnp.pad(gate_w2, ((0, LANES - GLA_GATE_RANK), (0, 0))).astype(BF16)
    tok = lambda n: pl.BlockSpec((None, L, n), lambda b, c: (b, c, 0))
    return pl.pallas_call(
        functools.partial(_gla_kernel, L=L),
        grid=(B, S // L),
        in_specs=[tok(kw), tok(kw), tok(D_MODEL), tok(D_MODEL), tok(LANES),
                  _resident(w2.shape), _resident((1, kw)), _resident((1, D_MODEL))],
        out_specs=tok(D_MODEL),
        out_shape=jax.ShapeDtypeStruct((B, S, D_MODEL), BF16),
        scratch_shapes=[pltpu.VMEM((GLA_HEADS, GLA_KEY_DIM, GLA_VALUE_DIM), F32)],
        compiler_params=_params("parallel", "arbitrary"),
        name="gla",
    )(q, k, v, g, z, w2, gate_b.reshape(1, kw), norm_g.reshape(1, D_MODEL))


def kernel(x, mem, norm_g, ffn_w_up, ffn_w_down, mem_norm_g, mem_w_kv, pool_w_in, pool_w_group, pool_scale, pool_w_out, diff_w_in, diff_lambda, diff_norm_g, diff_w_out, mlstm_w_in, mlstm_conv_w, mlstm_gate_b, mlstm_norm_g, mlstm_w_out, gla_w_in, gla_gate_w2, gla_gate_b, gla_norm_g, gla_w_out, final_norm_g):
    B, S, D = x.shape
    M = mem.shape[1]
    T = B * S
    depth = norm_g.shape[0]
    n_mixers = 4
    mem2 = mem.reshape(B * M, D)
    x2 = x.reshape(T, D)
    norm_g4 = norm_g.reshape(depth, 3, 1, D)
    w_up, w_down = ffn_w_up[0, 0].astype(BF16), ffn_w_down[0, 0].astype(BF16)

    def seq(a):
        return a.reshape(B, S, a.shape[-1])

    for i in range(depth):
        kind, j = i % n_mixers, i // n_mixers
        k_scale = X_HEAD_DIM ** -0.5 * LOG2E
        mem_k, mem_v = _norm_proj(mem2, mem_norm_g, [(mem_w_kv[i][:, :X_WIDTH] * k_scale, BF16),
                                                     (mem_w_kv[i][:, X_WIDTH:], BF16)])
        mem_k = mem_k.reshape(B, M, X_WIDTH)
        mem_v = mem_v.reshape(B, M, X_WIDTH)
        x2, w_up, w_down = _ffn(x2, _ffn_weights(norm_g4, w_up, w_down, i, 0), (ffn_w_up, ffn_w_down, i, 1))
        if kind == 0:
            mix, xq = _pool_mix(seq(x2), norm_g[i, 1], pool_w_in[j], pool_w_group[j], pool_scale[j])
            w_out = pool_w_out[j]
        elif kind == 1:
            w = diff_w_in[j]
            q_scale = DIFF_HEAD_DIM ** -0.5 * math.log2(math.e)
            q, k, v_t, xq = _norm_proj(x2, norm_g[i, 1], [
                (w[:, :D] * q_scale, BF16), (w[:, D:2 * D], BF16), (w[:, 2 * D:3 * D], BF16, "values_t"),
                (w[:, 3 * D:], BF16)], seq_len=S)
            mix = _diff_mix(seq(q), seq(k), v_t, diff_lambda[j], diff_norm_g[j], i)
            w_out = diff_w_out[j]
        elif kind == 2:
            w = mlstm_w_in[j]
            ng = 2 * MLSTM_HEADS
            qk, v, og, gates, xq = _norm_proj(x2, norm_g[i, 1], [
                (w[:, :2 * D], F32), (w[:, 2 * D:3 * D], BF16), (w[:, 3 * D:4 * D], F32),
                (w[:, 4 * D:4 * D + ng], F32), (w[:, 4 * D + ng:], BF16)])
            mix = _mlstm_mix(seq(qk), seq(v), seq(og), seq(gates), mlstm_conv_w[j], mlstm_gate_b[j], mlstm_norm_g[j])
            w_out = mlstm_w_out[j]
        else:
            w = gla_w_in[j]
            kw = GLA_HEADS * GLA_KEY_DIM
            o0 = 2 * kw + 2 * D
            q, k, v, g, z, xq = _norm_proj(x2, norm_g[i, 1], [
                (w[:, :kw], F32), (w[:, kw:2 * kw], F32), (w[:, 2 * kw:2 * kw + D], BF16),
                (w[:, 2 * kw + D:o0], F32), (w[:, o0:o0 + GLA_GATE_RANK], BF16), (w[:, o0 + GLA_GATE_RANK:], BF16)])
            mix = _gla_mix(seq(q), seq(k), seq(v), seq(g), seq(z), gla_gate_w2[j], gla_gate_b[j], gla_norm_g[j])
            w_out = gla_w_out[j]
        last = i == depth - 1
        x3, w_up, w_down = _attn_out_ffn(seq(x2), mix, seq(xq), mem_k, mem_v, w_out,
                                         _ffn_weights(norm_g4, w_up, w_down, i, 2),
                                         final_g=final_norm_g if last else None,
                                         next_cast=None if last else (ffn_w_up, ffn_w_down, i + 1, 0))
        x2 = x3.reshape(T, D)
    return x2.reshape(B, S, D)
```

```python
import functools
import math

import jax
import jax.numpy as jnp
from jax import lax
from jax.experimental import pallas as pl
from jax.experimental.pallas import tpu as pltpu

F32 = jnp.float32
BF16 = jnp.bfloat16

D_MODEL = 1024
D_FF = 2816
X_HEADS = 4
X_HEAD_DIM = 128
X_WIDTH = X_HEADS * X_HEAD_DIM
POOL_WINDOWS = (2, 4, 8, 16)
POOL_GROUP_DIM = D_MODEL // len(POOL_WINDOWS)
POOL_HALO = 16
DIFF_HEAD_DIM = 64
DIFF_HEADS = 8
DIFF_HEAD_BLOCK = 8
MLSTM_HEADS = 4
MLSTM_HEAD_DIM = 256
MLSTM_CONV = 4
CONV_HALO = 8
GLA_HEADS = 4
GLA_KEY_DIM = 128
GLA_VALUE_DIM = 256
GLA_GATE_RANK = 16
GLA_TAU = 16.0
GLA_SUBCHUNK = 16
LOG2E = math.log2(math.e)
NORM_EPS = 1e-6
SUBLN_EPS = 1e-5
LANES = 128
SUBLANES = 8

VMEM_LIMIT = 56 * 1024 * 1024
TOKEN_TILE = 512
FF_CHUNKS = (0, 1536, D_FF)
FFN_TOKEN_TILE = 1024
FFN_WIDE_CHUNKS = (0, 768, 1536, 2304, D_FF)
ATTN_TILE = 256
MLSTM_CHUNK = 256
GLA_CHUNK = 128


def _params(*sem):
    return pltpu.CompilerParams(dimension_semantics=sem, vmem_limit_bytes=VMEM_LIMIT)


def _resident(shape):
    nd = len(shape)
    return pl.BlockSpec(shape, lambda *_: (0,) * nd, pipeline_mode=pl.Buffered(1))


def _rms(x, g, eps):
    return x * lax.rsqrt(jnp.mean(x * x, axis=-1, keepdims=True) + eps) * g


def _sigmoid(x):
    return 1.0 / (1.0 + jnp.exp(-x))


def _log_sigmoid(x):
    return jnp.minimum(x, 0.0) - jnp.log(1.0 + jnp.exp(-jnp.abs(x)))


def _dot(a, b):
    return jnp.dot(a, b, preferred_element_type=F32)


def _dot_nt(a, b):
    return lax.dot_general(a, b, (((1,), (1,)), ((), ())), preferred_element_type=F32)


def _split3(x):
    hi = x.astype(BF16)
    r1 = x - hi.astype(F32)
    mid = r1.astype(BF16)
    lo = (r1 - mid.astype(F32)).astype(BF16)
    return hi, mid, lo


def _tri(n, upper):
    r = lax.broadcasted_iota(jnp.int32, (n, n), 0)
    c = lax.broadcasted_iota(jnp.int32, (n, n), 1)
    keep = (r <= c) if upper else (c <= r)
    return jnp.where(keep, 1.0, 0.0).astype(BF16)


def _cumsum_rows(x):
    tri = _tri(x.shape[0], upper=False)
    hi, mid, lo = _split3(x)
    return _dot(tri, hi) + _dot(tri, mid) + _dot(tri, lo)


def _cumsum_lanes(x):
    tri = _tri(x.shape[1], upper=True)
    hi, mid, lo = _split3(x)
    return _dot(hi, tri) + _dot(mid, tri) + _dot(lo, tri)


def _ffn_half_step(x, g_ref, wg_ref, wu_ref, wd_ref, chunks=FF_CHUNKS):
    h = _rms(x, g_ref[...], NORM_EPS).astype(BF16)
    acc = jnp.zeros_like(x)
    for lo, hi in zip(chunks[:-1], chunks[1:]):
        sl = slice(lo, hi)
        gate = _dot(h, wg_ref[:, sl])
        up = _dot(h, wu_ref[:, sl])
        act = (gate * _sigmoid(gate) * up).astype(BF16)
        acc = acc + _dot(act, wd_ref[sl, :])
    return x + 0.5 * acc


def _stacked(shape, lead, tail=None):
    tail = tail or (0,) * len(shape)
    return pl.BlockSpec((None,) * len(lead) + shape, lambda *_: lead + tail, pipeline_mode=pl.Buffered(1))


def _ffn_weights(norm_g, w_up, w_down, layer, norm_slot):
    args = [norm_g, w_up, w_up, w_down]
    specs = [_stacked((1, D_MODEL), (layer, norm_slot)), _stacked((D_MODEL, D_FF), (), (0, 0)),
             _stacked((D_MODEL, D_FF), (), (0, 1)), _stacked((D_FF, D_MODEL), ())]
    return args, specs


def _next_weights_cast(w_up_all, w_down_all, layer, slot, step_of, n_steps):
    args, in_specs, out_specs, out_shapes = [w_up_all, w_down_all], [], [], []
    for w in args:
        n_rows, n_cols = w.shape[2:]
        n_slabs = max(s for s in range(1, n_steps + 1) if n_rows % (16 * s) == 0)
        slab = lambda *idx, n_slabs=n_slabs: jnp.minimum(step_of(*idx), n_slabs - 1)
        in_specs.append(pl.BlockSpec((None, None, n_rows // n_slabs, n_cols),
                                     lambda *idx, slab=slab: (layer, slot, slab(*idx), 0)))
        out_specs.append(pl.BlockSpec((n_rows // n_slabs, n_cols), lambda *idx, slab=slab: (slab(*idx), 0)))
        out_shapes.append(jax.ShapeDtypeStruct((n_rows, n_cols), BF16))
    return args, in_specs, out_specs, out_shapes


def _ffn_kernel(x_ref, g_ref, wg_ref, wu_ref, wd_ref, nu_ref, nd_ref, o_ref, nu_out_ref, nd_out_ref):
    o_ref[...] = _ffn_half_step(x_ref[...], g_ref, wg_ref, wu_ref, wd_ref, FFN_WIDE_CHUNKS)
    nu_out_ref[...] = nu_ref[...].astype(BF16)
    nd_out_ref[...] = nd_ref[...].astype(BF16)


def _ffn(x, ffn_weights, next_cast):
    T = x.shape[0]
    w_args, w_specs = ffn_weights
    n_steps = T // FFN_TOKEN_TILE
    c_args, c_in, c_out, c_shapes = _next_weights_cast(*next_cast, step_of=lambda i: i, n_steps=n_steps)
    tok = pl.BlockSpec((FFN_TOKEN_TILE, D_MODEL), lambda i: (i, 0))
    y, w_up, w_down = pl.pallas_call(
        _ffn_kernel,
        grid=(n_steps,),
        in_specs=[tok] + w_specs + c_in,
        out_specs=[tok] + c_out,
        out_shape=[jax.ShapeDtypeStruct((T, D_MODEL), F32)] + c_shapes,
        compiler_params=_params("parallel"),
        name="ffn",
    )(x, *w_args, *c_args)
    return y, w_up, w_down


def _norm_proj_kernel(x_ref, g_ref, *refs, n_out):
    h = _rms(x_ref[...], g_ref[...], NORM_EPS).astype(BF16)
    for w_ref, o_ref in zip(refs[:n_out], refs[n_out:]):
        y = _dot(h, w_ref[...])
        if len(o_ref.shape) == 2:
            o_ref[...] = y.astype(o_ref.dtype)
            continue
        n_heads, n_sub, rows, t = o_ref.shape
        hd = rows - DIFF_AUX_ROWS
        aux = jnp.where(lax.broadcasted_iota(jnp.int32, (DIFF_AUX_ROWS, t), 0) == 0, 1.0, 0.0).astype(o_ref.dtype)
        for head in range(n_heads):
            for s in range(n_sub):
                o_ref[head, s, 0:hd, :] = y[s * t:(s + 1) * t, head * hd:(head + 1) * hd].T.astype(o_ref.dtype)
                o_ref[head, s, hd:rows, :] = aux


def _norm_proj(x, g, pieces, seq_len=None):
    T = x.shape[0]
    ws, out_specs, out_shapes = [], [], []
    for w, dt, *layout in pieces:
        n = w.shape[1]
        n_pad = -(-n // LANES) * LANES
        if n_pad != n:
            w = jnp.pad(w, ((0, 0), (0, n_pad - n)))
        ws.append(w.astype(BF16))
        if layout:
            hd, t, per_seq = 2 * DIFF_HEAD_DIM, ATTN_TILE, seq_len // TOKEN_TILE
            blk = (None, n // hd, TOKEN_TILE // t, hd + DIFF_AUX_ROWS, t)
            out_specs.append(pl.BlockSpec(blk, lambda i: (i // per_seq, 0, i % per_seq, 0, 0)))
            out_shapes.append(jax.ShapeDtypeStruct((T // seq_len, n // hd, seq_len // t) + blk[3:], dt))
            continue
        out_specs.append(pl.BlockSpec((TOKEN_TILE, n_pad), lambda i: (i, 0)))
        out_shapes.append(jax.ShapeDtypeStruct((T, n_pad), dt))
    in_specs = [pl.BlockSpec((TOKEN_TILE, D_MODEL), lambda i: (i, 0)), _resident((1, D_MODEL))]
    in_specs += [_resident(w.shape) for w in ws]
    return pl.pallas_call(
        functools.partial(_norm_proj_kernel, n_out=len(ws)),
        grid=(T // TOKEN_TILE,),
        in_specs=in_specs,
        out_specs=out_specs,
        out_shape=out_shapes,
        compiler_params=_params("parallel"),
        name="norm_proj",
    )(x, g.reshape(1, D_MODEL), *ws)


def _attn_out_kernel(x_ref, mix_ref, xq_ref, mk_ref, mv_ref, w1_ref, w2_ref, g_ref, wg_ref, wu_ref, wd_ref,
                     *rest, final):
    xq = xq_ref[...]
    mk = mk_ref[...]
    mv = mv_ref[...]
    outs = []
    for h in range(X_HEADS):
        sl = slice(h * X_HEAD_DIM, (h + 1) * X_HEAD_DIM)
        s = _dot_nt(xq[:, sl], mk[:, sl])
        e = jnp.exp2(s - jnp.max(s, axis=-1, keepdims=True))
        o = _dot(e.astype(BF16), mv[:, sl]) * (1.0 / jnp.sum(e, axis=-1, keepdims=True))
        outs.append(o.astype(BF16))
    xo = jnp.concatenate(outs, axis=-1)
    x = x_ref[...] + _dot(mix_ref[...], w1_ref[...]) + _dot(xo, w2_ref[...])
    y = _ffn_half_step(x, g_ref, wg_ref, wu_ref, wd_ref)
    if final:
        fg_ref, o_ref = rest
        o_ref[...] = _rms(y, fg_ref[...], NORM_EPS)
    else:
        nu_ref, nd_ref, o_ref, nu_out_ref, nd_out_ref = rest
        o_ref[...] = y
        nu_out_ref[...] = nu_ref[...].astype(BF16)
        nd_out_ref[...] = nd_ref[...].astype(BF16)


def _attn_out_ffn(x, mix, xq, mem_k, mem_v, w_out, ffn_weights, final_g=None, next_cast=None):
    B, S, _ = x.shape
    M = mem_k.shape[1]
    w1 = w_out[:D_MODEL].astype(BF16)
    w2 = w_out[D_MODEL:].astype(BF16)
    w_args, w_specs = ffn_weights
    tok = lambda n: pl.BlockSpec((None, TOKEN_TILE, n), lambda b, i: (b, i, 0))
    mem = pl.BlockSpec((None, M, X_WIDTH), lambda b, i: (b, 0, 0))
    out_specs, out_shapes = [tok(D_MODEL)], [jax.ShapeDtypeStruct((B, S, D_MODEL), F32)]
    per_seq = S // TOKEN_TILE
    if final_g is not None:
        w_args = w_args + [final_g.reshape(1, D_MODEL)]
        w_specs = w_specs + [_resident((1, D_MODEL))]
    else:
        c_args, c_in, c_out, c_shapes = _next_weights_cast(*next_cast, step_of=lambda b, i: b * per_seq + i,
                                                           n_steps=B * per_seq)
        w_args, w_specs = w_args + c_args, w_specs + c_in
        out_specs, out_shapes = out_specs + c_out, out_shapes + c_shapes
    outs = pl.pallas_call(
        functools.partial(_attn_out_kernel, final=final_g is not None),
        grid=(B, per_seq),
        in_specs=[tok(D_MODEL), tok(D_MODEL), tok(X_WIDTH), mem, mem, _resident(w1.shape), _resident(w2.shape)]
        + w_specs,
        out_specs=out_specs,
        out_shape=out_shapes,
        compiler_params=_params("parallel", "parallel"),
        name="attn_out_ffn",
    )(x, mix, xq, mem_k, mem_v, w1, w2, *w_args)
    if final_g is not None:
        return outs[0], None, None
    return outs[0], outs[1], outs[2]


def _pool_kernel(x_ref, g_ref, wu_ref, wq_ref, wg_ref, sc_ref, o_ref, xq_ref, ext_ref, a_ref, b_ref, *, ts):
    j = pl.program_id(1)
    H = POOL_HALO

    @pl.when(j == 0)
    def _():
        ext_ref[0:2 * H, :] = jnp.zeros((2 * H, D_MODEL), F32)
        a_ref[0:H, :] = jnp.zeros((H, POOL_GROUP_DIM), F32)
        b_ref[0:H, :] = jnp.zeros((H, POOL_GROUP_DIM), F32)

    h = _rms(x_ref[...], g_ref[...], NORM_EPS).astype(BF16)
    xq_ref[...] = _dot(h, wq_ref[...]).astype(xq_ref.dtype)
    ext_ref[2 * H:2 * H + ts, :] = _dot(h, wu_ref[...])
    pos = j * ts + lax.broadcasted_iota(jnp.int32, (ts, 1), 0)
    n = ts + H
    for g, w in enumerate(POOL_WINDOWS):
        sl = slice(g * POOL_GROUP_DIM, (g + 1) * POOL_GROUP_DIM)
        src, cols, d, bufs = ext_ref, sl, 1, [a_ref, b_ref]
        while d < w:
            dst = bufs.pop(0)
            dst[H:H + n, :] = src[H:H + n, cols] + src[H - d:H - d + n, cols]
            bufs.append(dst)
            src, cols, d = dst, slice(None), 2 * d
        inv = 1.0 / jnp.minimum(pos + 1, w).astype(F32)
        pooled = (src[2 * H:2 * H + ts, cols] * inv - ext_ref[2 * H:2 * H + ts, sl]).astype(BF16)
        o_ref[:, sl] = (_dot(pooled, wg_ref[g]) * sc_ref[:, sl]).astype(o_ref.dtype)
    ext_ref[H:2 * H, :] = ext_ref[ts + H:ts + 2 * H, :]


def _pool_mix(x, g, w_in, w_group, scale):
    B, S, _ = x.shape
    ts = TOKEN_TILE
    tok = lambda n: pl.BlockSpec((None, ts, n), lambda b, j: (b, j, 0))
    return pl.pallas_call(
        functools.partial(_pool_kernel, ts=ts),
        grid=(B, S // ts),
        in_specs=[tok(D_MODEL), _resident((1, D_MODEL)), _resident((D_MODEL, D_MODEL)), _resident((D_MODEL, X_WIDTH)),
                  _resident(w_group.shape), _resident((1, D_MODEL))],
        out_specs=[tok(D_MODEL), tok(X_WIDTH)],
        out_shape=[jax.ShapeDtypeStruct((B, S, D_MODEL), BF16), jax.ShapeDtypeStruct((B, S, X_WIDTH), BF16)],
        scratch_shapes=[pltpu.VMEM((ts + 2 * POOL_HALO, D_MODEL), F32),
                        pltpu.VMEM((ts + 2 * POOL_HALO, POOL_GROUP_DIM), F32),
                        pltpu.VMEM((ts + 2 * POOL_HALO, POOL_GROUP_DIM), F32)],
        compiler_params=_params("parallel", "arbitrary"),
        name="pool_mix",
    )(x, g.reshape(1, D_MODEL), w_in[:, :D_MODEL].astype(BF16), w_in[:, D_MODEL:].astype(BF16),
      w_group.astype(BF16), scale.reshape(1, D_MODEL))


DIFF_SLOPE_TERMS = 3
DIFF_AUX_ROWS = 16


def _lane_features(lane, groups):
    out = jnp.zeros(lane.shape, F32)
    for g, v in enumerate(groups):
        for r in range(DIFF_SLOPE_TERMS):
            out = jnp.where(lane == DIFF_SLOPE_TERMS * g + r, v[r] if isinstance(v, list) else v, out)
    return out


def _diff_kernel(slopes_ref, lam_ref, q_ref, k_ref, vt_ref, ng_ref, o_ref,
                 ka_ref, qf_ref, acc_ref, m_ref, *, t, n_tiles, lam_init):
    hg = pl.program_id(1)
    qi = pl.program_id(2)
    hd = 2 * DIFF_HEAD_DIM
    lane = lax.broadcasted_iota(jnp.int32, (t, hd), 1)
    rowf = lax.broadcasted_iota(jnp.int32, (t, hd), 0).astype(F32)
    own = (lane < DIFF_HEAD_DIM, lane >= DIFF_HEAD_DIM)
    heads = range(DIFF_HEAD_BLOCK)
    c_terms = [[slopes_ref[hg * DIFF_HEAD_BLOCK + hb, r] for r in range(DIFF_SLOPE_TERMS)] for hb in heads]
    chains = [(hb, c) for hb in heads for c in range(2)]

    @pl.when(qi == 0)
    def _():
        key_feat = []
        for hb in heads:
            neg = [-cr for cr in c_terms[hb]]
            key_feat.append(_lane_features(lane, [rowf, 0.0, neg, [cr * t for cr in neg]]))
        key_tile_lanes = (lane >= DIFF_SLOPE_TERMS) & (lane < 2 * DIFF_SLOPE_TERMS)

        def build(j, carry):
            start = pl.multiple_of(j * t, t)
            for hb in heads:
                feat = jnp.where(key_tile_lanes, jnp.asarray(j, F32), key_feat[hb])
                ka_ref[hb, pl.ds(start, t), 0:hd] = k_ref[pl.ds(start, t), hb * hd:(hb + 1) * hd]
                ka_ref[hb, pl.ds(start, t), hd:2 * hd] = feat.astype(BF16)
            return carry

        lax.fori_loop(0, n_tiles, build, 0)
        for hb in heads:
            qf_ref[hb] = _lane_features(lane, [c_terms[hb], [cr * t for cr in c_terms[hb]], rowf, 0.0])

    qa = {}
    tile_lanes = (lane >= 3 * DIFF_SLOPE_TERMS) & (lane < 4 * DIFF_SLOPE_TERMS)
    for hb in heads:
        q = q_ref[:, hb * hd:(hb + 1) * hd]
        feat = jnp.where(tile_lanes, jnp.asarray(qi, F32), qf_ref[hb])
        for c in range(2):
            qa[hb, c] = jnp.concatenate([jnp.where(own[c], q, jnp.zeros_like(q)), feat.astype(BF16)], axis=1)
    acc_ref[...] = jnp.zeros_like(acc_ref)
    m_ref[...] = jnp.full(m_ref.shape, -jnp.inf, F32)

    def tiles(kjs, masked):
        starts = [pl.multiple_of(kj * t, t) for kj in kjs]
        scores = {}
        for hb, c in chains:
            parts = []
            for start in starts:
                s = _dot_nt(ka_ref[hb, pl.ds(start, t), :], qa[hb, c])
                if masked:
                    key = lax.broadcasted_iota(jnp.int32, (t, t), 0)
                    qry = lax.broadcasted_iota(jnp.int32, (t, t), 1)
                    s = jnp.where(key <= qry, s, -jnp.inf)
                parts.append(s)
            scores[hb, c] = parts
        probs = {}
        for hb, c in chains:
            parts = scores[hb, c]
            m_prev = m_ref[hb, c]
            m_new = m_prev
            for s in parts:
                m_new = jnp.maximum(m_new, jnp.max(s, axis=0, keepdims=True))
            alpha = jnp.exp2(m_prev - m_new)
            m_ref[hb, c] = m_new
            probs[hb, c] = (alpha, jnp.concatenate([jnp.exp2(s - m_new).astype(BF16) for s in parts], axis=0))
        for hb, c in chains:
            alpha, p = probs[hb, c]
            vt = jnp.concatenate([vt_ref[hb, kj] for kj in kjs], axis=1)
            acc_ref[hb, c] = alpha * acc_ref[hb, c] + _dot(vt, p)

    def body(pj, carry):
        tiles([2 * pj, 2 * pj + 1], masked=False)
        return carry

    lax.fori_loop(0, qi // 2, body, 0)

    @pl.when(qi % 2 == 1)
    def _():
        tiles([qi - 1], masked=False)

    tiles([qi], masked=True)

    lp = lam_ref[...]
    lam = (jnp.exp(jnp.sum(lp[0:1] * lp[1:2], axis=-1, keepdims=True))
           - jnp.exp(jnp.sum(lp[2:3] * lp[3:4], axis=-1, keepdims=True)) + lam_init)
    for hb in range(DIFF_HEAD_BLOCK):
        a0, a1 = acc_ref[hb, 0], acc_ref[hb, 1]
        o_t = (a0[:hd] * (1.0 / a0[hd:hd + 1]) - lam * (a1[:hd] * (1.0 / a1[hd:hd + 1])))
        o_ref[:, hb * hd:(hb + 1) * hd] = (_rms(o_t.T, ng_ref[...], SUBLN_EPS) * (1.0 - lam_init)).astype(o_ref.dtype)


def _diff_mix(q, k, v_t, lam_p, norm_g, layer_idx):
    B, S, _ = q.shape
    t = ATTN_TILE
    n_tiles = S // t
    hd = 2 * DIFF_HEAD_DIM
    hb = DIFF_HEAD_BLOCK
    lam_init = 0.8 - 0.6 * math.exp(-0.3 * layer_idx)
    slopes = jnp.asarray([2.0 ** (-8.0 * (h + 1) / DIFF_HEADS) for h in range(DIFF_HEADS)], dtype=F32)
    rest = slopes * math.log2(math.e)
    terms = []
    for _ in range(DIFF_SLOPE_TERMS):
        terms.append(rest.astype(BF16).astype(F32))
        rest = rest - terms[-1]
    slopes = jnp.stack(terms, axis=1)
    qspec = pl.BlockSpec((None, t, hb * hd), lambda b, h, i: (b, i, h))
    return pl.pallas_call(
        functools.partial(_diff_kernel, t=t, n_tiles=n_tiles, lam_init=lam_init),
        grid=(B, DIFF_HEADS // hb, n_tiles),
        in_specs=[pl.BlockSpec(memory_space=pltpu.SMEM), _resident(lam_p.shape), qspec,
                  pl.BlockSpec((None, S, hb * hd), lambda b, h, i: (b, 0, h), pipeline_mode=pl.Buffered(1)),
                  pl.BlockSpec((None, hb, n_tiles, hd + DIFF_AUX_ROWS, t), lambda b, h, i: (b, h, 0, 0, 0),
                               pipeline_mode=pl.Buffered(1)),
                  _resident((1, hd))],
        out_specs=qspec,
        out_shape=jax.ShapeDtypeStruct((B, S, D_MODEL), BF16),
        scratch_shapes=[pltpu.VMEM((hb, S, 2 * hd), BF16), pltpu.VMEM((hb, t, hd), F32),
                        pltpu.VMEM((hb, 2, hd + DIFF_AUX_ROWS, t), F32),
                        pltpu.VMEM((hb, 2, 1, t), F32)],
        compiler_params=_params("parallel", "parallel", "arbitrary"),
        name="diff_attn",
    )(slopes, lam_p, q, k, v_t, norm_g.reshape(1, hd))


def _mlstm_kernel(qk_ref, v_ref, og_ref, gc_ref, gr_ref, cw_ref, gbc_ref, gbr_ref, ng_ref, o_ref,
                  ext_ref, c_ref, n_ref, m_ref, *, L):
    c = pl.program_id(1)
    W = 2 * D_MODEL

    @pl.when(c == 0)
    def _():
        ext_ref[0:CONV_HALO, :] = jnp.zeros((CONV_HALO, W), F32)
        c_ref[...] = jnp.zeros_like(c_ref)
        n_ref[...] = jnp.zeros_like(n_ref)
        m_ref[...] = jnp.zeros_like(m_ref)

    ext_ref[CONV_HALO:CONV_HALO + L, :] = qk_ref[...]
    conv = cw_ref[MLSTM_CONV - 1:MLSTM_CONV, :] * qk_ref[...]
    for j in range(1, MLSTM_CONV):
        conv = conv + (cw_ref[MLSTM_CONV - 1 - j:MLSTM_CONV - j, :]
                       * ext_ref[CONV_HALO - j:CONV_HALO - j + L, :])
    ext_ref[0:CONV_HALO, :] = ext_ref[L:L + CONV_HALO, :]
    qk = conv * _sigmoid(conv)

    gc = gc_ref[...] + gbc_ref[...]
    gr = gr_ref[...] + gbr_ref[...]
    b_c = _cumsum_rows(_log_sigmoid(gc))
    b_r = _cumsum_lanes(_log_sigmoid(gr))
    row = lax.broadcasted_iota(jnp.int32, (L, L), 0)
    col = lax.broadcasted_iota(jnp.int32, (L, L), 1)
    causal = col <= row

    for hd in range(MLSTM_HEADS):
        sl = slice(hd * MLSTM_HEAD_DIM, (hd + 1) * MLSTM_HEAD_DIM)
        q32 = qk[:, sl] * (MLSTM_HEAD_DIM ** -0.5)
        k32 = qk[:, D_MODEL + hd * MLSTM_HEAD_DIM:D_MODEL + (hd + 1) * MLSTM_HEAD_DIM]
        vh = v_ref[:, sl]
        fi = MLSTM_HEADS + hd
        bcol, icol = b_c[:, fi:fi + 1], gc[:, hd:hd + 1]
        brow, irow = b_r[fi:fi + 1, :], gr[hd:hd + 1, :]
        b_last = brow[:, L - 1:L]
        m = m_ref[hd:hd + 1, 0:1]
        cmat = c_ref[hd]
        nrow = n_ref[hd]

        dmat = jnp.where(causal, bcol - brow + irow, -jnp.inf)
        inter = bcol + m
        m_t = jnp.maximum(inter, jnp.max(dmat, axis=-1, keepdims=True))
        dec = jnp.exp(inter - m_t)
        qb = q32.astype(BF16)
        k_t = k32.T
        sqk = _dot(qb, k_t.astype(BF16)) * jnp.exp(dmat - m_t)
        num = dec * _dot(qb, cmat.astype(BF16)) + _dot(sqk.astype(BF16), vh)
        den = dec * jnp.sum(q32 * nrow, axis=-1, keepdims=True) + jnp.sum(sqk, axis=-1, keepdims=True)
        hc = num * (1.0 / jnp.maximum(jnp.abs(den), jnp.exp(-m_t)))

        gs_r = b_last - brow + irow
        gs_c = b_last - bcol + icol
        m_new = jnp.maximum(b_last + m, jnp.max(gs_r, axis=-1, keepdims=True))
        carry_dec = jnp.exp(b_last + m - m_new)
        c_ref[hd] = carry_dec * cmat + _dot((k_t * jnp.exp(gs_r - m_new)).astype(BF16), vh)
        n_ref[hd] = carry_dec * nrow + jnp.sum(k32 * jnp.exp(gs_c - m_new), axis=0, keepdims=True)
        m_ref[hd:hd + 1, :] = jnp.broadcast_to(m_new, (1, LANES))

        o_ref[:, sl] = (_rms(hc, ng_ref[:, sl], NORM_EPS) * _sigmoid(og_ref[:, sl])).astype(o_ref.dtype)


def _mlstm_mix(qk, v, og, gates, conv_w, gate_b, norm_g):
    B, S, _ = qk.shape
    L = MLSTM_CHUNK
    H = MLSTM_HEADS
    gates_r = jnp.transpose(gates[..., :2 * H], (0, 2, 1))
    gb = gate_b.reshape(2 * H)
    gb_c = jnp.pad(gb, (0, LANES - 2 * H)).reshape(1, LANES)
    gb_r = gb.reshape(2 * H, 1)
    tok = lambda n: pl.BlockSpec((None, L, n), lambda b, c: (b, c, 0))
    return pl.pallas_call(
        functools.partial(_mlstm_kernel, L=L),
        grid=(B, S // L),
        in_specs=[tok(2 * D_MODEL), tok(D_MODEL), tok(D_MODEL), tok(LANES),
                  pl.BlockSpec((None, 2 * H, L), lambda b, c: (b, 0, c)),
                  _resident(conv_w.shape), _resident((1, LANES)), _resident((2 * H, 1)), _resident((1, D_MODEL))],
        out_specs=tok(D_MODEL),
        out_shape=jax.ShapeDtypeStruct((B, S, D_MODEL), BF16),
        scratch_shapes=[pltpu.VMEM((L + CONV_HALO, 2 * D_MODEL), F32),
                        pltpu.VMEM((H, MLSTM_HEAD_DIM, MLSTM_HEAD_DIM), F32),
                        pltpu.VMEM((H, 1, MLSTM_HEAD_DIM), F32),
                        pltpu.VMEM((8, LANES), F32)],
        compiler_params=_params("parallel", "arbitrary"),
        name="mlstm",
    )(qk, v, og, gates, gates_r, conv_w, gb_c, gb_r, norm_g.reshape(1, D_MODEL))


def _gla_kernel(q_ref, k_ref, v_ref, g_ref, z_ref, w2_ref, gb_ref, ng_ref, o_ref, st_ref, *, L):
    c = pl.program_id(1)

    @pl.when(c == 0)
    def _():
        st_ref[...] = jnp.zeros_like(st_ref)

    log_a = _log_sigmoid(_dot(z_ref[...], w2_ref[...]) + gb_ref[...]) * (1.0 / GLA_TAU)
    b_all = _cumsum_rows(log_a) * LOG2E
    sub = GLA_SUBCHUNK
    row = lax.broadcasted_iota(jnp.int32, (sub, L), 0)
    col = lax.broadcasted_iota(jnp.int32, (sub, L), 1)
    ones = jnp.ones((GLA_KEY_DIM, LANES), BF16)

    for hd in range(GLA_HEADS):
        ks = slice(hd * GLA_KEY_DIM, (hd + 1) * GLA_KEY_DIM)
        vs = slice(hd * GLA_VALUE_DIM, (hd + 1) * GLA_VALUE_DIM)
        b = b_all[:, ks]
        vh = v_ref[:, vs]
        st = st_ref[hd]
        q = q_ref[:, ks] * (GLA_KEY_DIM ** -0.5)
        k = k_ref[:, ks]
        b_t = b.T
        k_t = k.T
        b_last = b_t[:, L - 1:L]
        blocks = []
        for i in range(L // sub):
            r0 = i * sub
            q_i, k_i, b_i = q[r0:r0 + sub], k[r0:r0 + sub], b[r0:r0 + sub]
            terms, offs = [], []
            for s in range(sub):
                t0 = s // SUBLANES * SUBLANES
                offs.append(sum(x.shape[0] for x in terms))
                terms.append(q_i[t0:] * k_i[s:s + 1] * jnp.exp2(b_i[t0:] - b_i[s:s + 1]))
            sums = _dot(jnp.concatenate(terms, axis=0).astype(BF16), ones)
            groups = []
            for t0 in range(0, sub, SUBLANES):
                a_g = jnp.zeros((SUBLANES, L), F32)
                for s in range(min(sub, t0 + SUBLANES)):
                    lo = offs[s] + t0 - s // SUBLANES * SUBLANES
                    a_g = jnp.where(col[:SUBLANES] == r0 + s, sums[lo:lo + SUBLANES], a_g)
                groups.append(a_g)
            a_i = jnp.where(col <= r0 + row, jnp.concatenate(groups, axis=0), 0.0)
            if i > 0:
                ref_row, ref_col = b[r0 - 1:r0], b_t[:, r0 - 1:r0]
                q_s = (q_i * jnp.exp2(b_i - ref_row)).astype(BF16)
                k_s = (k_t * jnp.exp2(ref_col - b_t)).astype(BF16)
                a_i = jnp.where(col < r0, _dot(q_s, k_s), a_i)
            blocks.append(a_i)
        amat = jnp.concatenate(blocks, axis=0)
        o = _dot((q * jnp.exp2(b)).astype(BF16), st.astype(BF16)) + _dot(amat.astype(BF16), vh)
        st_ref[hd] = jnp.exp2(b_last) * st + _dot((k_t * jnp.exp2(b_last - b_t)).astype(BF16), vh)
        gate = g_ref[:, vs]
        o_ref[:, vs] = (_rms(o, ng_ref[:, vs], NORM_EPS) * (gate * _sigmoid(gate))).astype(o_ref.dtype)


def _gla_mix(q, k, v, g, z, gate_w2, gate_b, norm_g):
    B, S, _ = q.shape
    L = GLA_CHUNK
    kw =GLA_HEADS * GLA_KEY_DIM
    w2 = jnp.pad(gate_w2, ((0, LANES - GLA_GATE_RANK), (0, 0))).astype(BF16)
    tok = lambda n: pl.BlockSpec((None, L, n), lambda b, c: (b, c, 0))
    return pl.pallas_call(
        functools.partial(_gla_kernel, L=L),
        grid=(B, S // L),
        in_specs=[tok(kw), tok(kw), tok(D_MODEL), tok(D_MODEL), tok(LANES),
                  _resident(w2.shape), _resident((1, kw)), _resident((1, D_MODEL))],
        out_specs=tok(D_MODEL),
        out_shape=jax.ShapeDtypeStruct((B, S, D_MODEL), BF16),
        scratch_shapes=[pltpu.VMEM((GLA_HEADS, GLA_KEY_DIM, GLA_VALUE_DIM), F32)],
        compiler_params=_params("parallel", "arbitrary"),
        name="gla",
    )(q, k, v, g, z, w2, gate_b.reshape(1, kw), norm_g.reshape(1, D_MODEL))


def kernel(x, mem, norm_g, ffn_w_up, ffn_w_down, mem_norm_g, mem_w_kv, pool_w_in, pool_w_group, pool_scale, pool_w_out, diff_w_in, diff_lambda, diff_norm_g, diff_w_out, mlstm_w_in, mlstm_conv_w, mlstm_gate_b, mlstm_norm_g, mlstm_w_out, gla_w_in, gla_gate_w2, gla_gate_b, gla_norm_g, gla_w_out, final_norm_g):
    B, S, D = x.shape
    M = mem.shape[1]
    T = B * S
    depth = norm_g.shape[0]
    n_mixers = 4
    mem2 = mem.reshape(B * M, D)
    x2 = x.reshape(T, D)
    norm_g4 = norm_g.reshape(depth, 3, 1, D)
    w_up, w_down = ffn_w_up[0, 0].astype(BF16), ffn_w_down[0, 0].astype(BF16)

    def seq(a):
        return a.reshape(B, S, a.shape[-1])

    for i in range(depth):
        kind, j = i % n_mixers, i // n_mixers
        k_scale = X_HEAD_DIM ** -0.5 * LOG2E
        mem_k, mem_v = _norm_proj(mem2, mem_norm_g, [(mem_w_kv[i][:, :X_WIDTH] * k_scale, BF16),
                                                     (mem_w_kv[i][:, X_WIDTH:], BF16)])
        mem_k = mem_k.reshape(B, M, X_WIDTH)
        mem_v = mem_v.reshape(B, M, X_WIDTH)
        x2, w_up, w_down = _ffn(x2, _ffn_weights(norm_g4, w_up, w_down, i, 0), (ffn_w_up, ffn_w_down, i, 1))
        if kind == 0:
            mix, xq = _pool_mix(seq(x2), norm_g[i, 1], pool_w_in[j], pool_w_group[j], pool_scale[j])
            w_out = pool_w_out[j]
        elif kind == 1:
            w = diff_w_in[j]
            q_scale = DIFF_HEAD_DIM ** -0.5 * math.log2(math.e)
            q, k, v_t, xq = _norm_proj(x2, norm_g[i, 1], [
                (w[:, :D] * q_scale, BF16), (w[:, D:2 * D], BF16), (w[:, 2 * D:3 * D], BF16, "values_t"),
                (w[:, 3 * D:], BF16)], seq_len=S)
            mix = _diff_mix(seq(q), seq(k), v_t, diff_lambda[j], diff_norm_g[j], i)
            w_out = diff_w_out[j]
        elif kind == 2:
            w = mlstm_w_in[j]
            ng = 2 * MLSTM_HEADS
            qk, v, og, gates, xq = _norm_proj(x2, norm_g[i, 1], [
                (w[:, :2 * D], F32), (w[:, 2 * D:3 * D], BF16), (w[:, 3 * D:4 * D], F32),
                (w[:, 4 * D:4 * D + ng], F32), (w[:, 4 * D + ng:], BF16)])
            mix = _mlstm_mix(seq(qk), seq(v), seq(og), seq(gates), mlstm_conv_w[j], mlstm_gate_b[j], mlstm_norm_g[j])
            w_out = mlstm_w_out[j]
        else:
            w = gla_w_in[j]
            kw = GLA_HEADS * GLA_KEY_DIM
            o0 = 2 * kw + 2 * D
            q, k, v, g, z, xq = _norm_proj(x2, norm_g[i, 1], [
                (w[:, :kw], F32), (w[:, kw:2 * kw], F32), (w[:, 2 * kw:2 * kw + D], BF16),
                (w[:, 2 * kw + D:o0], F32), (w[:, o0:o0 + GLA_GATE_RANK], BF16), (w[:, o0 + GLA_GATE_RANK:], BF16)])
            mix = _gla_mix(seq(q), seq(k), seq(v), seq(g), seq(z), gla_gate_w2[j], gla_gate_b[j], gla_norm_g[j])
            w_out = gla_w_out[j]
        last = i == depth - 1
        x3, w_up, w_down = _attn_out_ffn(seq(x2), mix, seq(xq), mem_k, mem_v, w_out,
                                         _ffn_weights(norm_g4, w_up, w_down, i, 2),
                                         final_g=final_norm_g if last else None,
                                         next_cast=None if last else (ffn_w_up, ffn_w_down, i + 1, 0))
        x2 = x3.reshape(T, D)
    return x2.reshape(B, S, D)
```

```python
import functools
import math

import jax
import jax.numpy as jnp
from jax import lax
from jax.experimental import pallas as pl
from jax.experimental.pallas import tpu as pltpu

F32 = jnp.float32
BF16 = jnp.bfloat16

D_MODEL = 1024
D_FF = 2816
X_HEADS = 4
X_HEAD_DIM = 128
X_WIDTH = X_HEADS * X_HEAD_DIM
POOL_WINDOWS = (2, 4, 8, 16)
POOL_GROUP_DIM = D_MODEL // len(POOL_WINDOWS)
POOL_HALO = 16
DIFF_HEAD_DIM = 64
DIFF_HEADS = 8
DIFF_HEAD_BLOCK = 8
DIFF_SLOPE_TERMS = 3
DIFF_AUX_ROWS = 16
MLSTM_HEADS = 4
MLSTM_HEAD_DIM = 256
MLSTM_CONV = 4
CONV_HALO = 8
GLA_HEADS = 4
GLA_KEY_DIM = 128
GLA_VALUE_DIM = 256
GLA_GATE_RANK = 16
GLA_TAU = 16.0
GLA_SUBCHUNK = 16
LOG2E = math.log2(math.e)
NORM_EPS = 1e-6
SUBLN_EPS = 1e-5
LANES = 128
SUBLANES = 8

VMEM_LIMIT = 56 * 1024 * 1024
TOKEN_TILE = 512
FF_CHUNKS = (0, 1536, D_FF)
FFN_TOKEN_TILE = 1024
FFN_WIDE_CHUNKS = (0, 768, 1536, 2304, D_FF)
ATTN_TILE = 256
MLSTM_CHUNK = 256
GLA_CHUNK = 128
GLA_CHUNKS_PER_STEP = 2


def _params(*sem):
    return pltpu.CompilerParams(dimension_semantics=sem, vmem_limit_bytes=VMEM_LIMIT)


def _resident(shape):
    nd = len(shape)
    return pl.BlockSpec(shape, lambda *_: (0,) * nd, pipeline_mode=pl.Buffered(1))


def _rms(x, g, eps):
    return x * lax.rsqrt(jnp.mean(x * x, axis=-1, keepdims=True) + eps) * g


def _sigmoid(x):
    return 1.0 / (1.0 + jnp.exp(-x))


def _log_sigmoid(x):
    return jnp.minimum(x, 0.0) - jnp.log(1.0 + jnp.exp(-jnp.abs(x)))


def _dot(a, b):
    return jnp.dot(a, b, preferred_element_type=F32)


def _dot_nt(a, b):
    return lax.dot_general(a, b, (((1,), (1,)), ((), ())), preferred_element_type=F32)


def _split3(x):
    hi = x.astype(BF16)
    r1 = x - hi.astype(F32)
    mid = r1.astype(BF16)
    lo = (r1 - mid.astype(F32)).astype(BF16)
    return hi, mid, lo


def _tri(n, upper):
    r = lax.broadcasted_iota(jnp.int32, (n, n), 0)
    c = lax.broadcasted_iota(jnp.int32, (n, n), 1)
    keep = (r <= c) if upper else (c <= r)
    return jnp.where(keep, 1.0, 0.0).astype(BF16)


def _cumsum_rows(x):
    tri = _tri(x.shape[0], upper=False)
    hi, mid, lo = _split3(x)
    return _dot(tri, hi) + _dot(tri, mid) + _dot(tri, lo)


def _cumsum_lanes(x):
    tri = _tri(x.shape[1], upper=True)
    hi, mid, lo = _split3(x)
    return _dot(hi, tri) + _dot(mid, tri) + _dot(lo, tri)


def _ffn_half_step(x, g_ref, wg_ref, wu_ref, wd_ref, chunks=FF_CHUNKS):
    h = _rms(x, g_ref[...], NORM_EPS).astype(BF16)
    acc = jnp.zeros_like(x)
    for lo, hi in zip(chunks[:-1], chunks[1:]):
        sl = slice(lo, hi)
        gate = _dot(h, wg_ref[:, sl])
        up = _dot(h, wu_ref[:, sl])
        act = (gate * _sigmoid(gate) * up).astype(BF16)
        acc = acc + _dot(act, wd_ref[sl, :])
    return x + 0.5 * acc


def _stacked(shape, lead, tail=None):
    tail = tail or (0,) * len(shape)
    return pl.BlockSpec((None,) * len(lead) + shape, lambda *_: lead + tail, pipeline_mode=pl.Buffered(1))


def _ffn_weights(norm_g, w_up, w_down, layer, norm_slot):
    args = [norm_g, w_up, w_up, w_down]
    specs = [_stacked((1, D_MODEL), (layer, norm_slot)), _stacked((D_MODEL, D_FF), (), (0, 0)),
             _stacked((D_MODEL, D_FF), (), (0, 1)), _stacked((D_FF, D_MODEL), ())]
    return args, specs


def _next_weights_cast(w_up_all, w_down_all, layer, slot, step_of, n_steps):
    args, in_specs, out_specs, out_shapes = [w_up_all, w_down_all], [], [], []
    for w in args:
        n_rows, n_cols = w.shape[2:]
        n_slabs = max(s for s in range(1, n_steps + 1) if n_rows % (16 * s) == 0)
        slab = lambda *idx, n_slabs=n_slabs: jnp.minimum(step_of(*idx), n_slabs - 1)
        in_specs.append(pl.BlockSpec((None, None, n_rows // n_slabs, n_cols),
                                     lambda *idx, slab=slab: (layer, slot, slab(*idx), 0)))
        out_specs.append(pl.BlockSpec((n_rows // n_slabs, n_cols), lambda *idx, slab=slab: (slab(*idx), 0)))
        out_shapes.append(jax.ShapeDtypeStruct((n_rows, n_cols), BF16))
    return args, in_specs, out_specs, out_shapes


def _ffn_kernel(x_ref, g_ref, wg_ref, wu_ref, wd_ref, nu_ref, nd_ref, o_ref, nu_out_ref, nd_out_ref):
    o_ref[...] = _ffn_half_step(x_ref[...], g_ref, wg_ref, wu_ref, wd_ref, FFN_WIDE_CHUNKS)
    nu_out_ref[...] = nu_ref[...].astype(BF16)
    nd_out_ref[...] = nd_ref[...].astype(BF16)


def _ffn(x, ffn_weights, next_cast):
    T = x.shape[0]
    w_args, w_specs = ffn_weights
    n_steps = T // FFN_TOKEN_TILE
    c_args, c_in, c_out, c_shapes = _next_weights_cast(*next_cast, step_of=lambda i: i, n_steps=n_steps)
    tok = pl.BlockSpec((FFN_TOKEN_TILE, D_MODEL), lambda i: (i, 0))
    y, w_up, w_down = pl.pallas_call(
        _ffn_kernel,
        grid=(n_steps,),
        in_specs=[tok] + w_specs + c_in,
        out_specs=[tok] + c_out,
        out_shape=[jax.ShapeDtypeStruct((T, D_MODEL), F32)] + c_shapes,
        compiler_params=_params("parallel"),
        name="ffn",
    )(x, *w_args, *c_args)
    return y, w_up, w_down


def _norm_proj_kernel(x_ref, g_ref, *refs, n_out):
    h = _rms(x_ref[...], g_ref[...], NORM_EPS).astype(BF16)
    for w_ref, o_ref in zip(refs[:n_out], refs[n_out:]):
        y = _dot(h, w_ref[...])
        if len(o_ref.shape) == 2:
            o_ref[...] = y.astype(o_ref.dtype)
            continue
        n_heads, n_sub, rows, t = o_ref.shape
        hd = rows - DIFF_AUX_ROWS
        aux = jnp.where(lax.broadcasted_iota(jnp.int32, (DIFF_AUX_ROWS, t), 0) == 0, 1.0, 0.0).astype(o_ref.dtype)
        for head in range(n_heads):
            for s in range(n_sub):
                o_ref[head, s, 0:hd, :] = y[s * t:(s + 1) * t, head * hd:(head + 1) * hd].T.astype(o_ref.dtype)
                o_ref[head, s, hd:rows, :] = aux


def _norm_proj(x, g, pieces, seq_len=None):
    T = x.shape[0]
    ws, out_specs, out_shapes = [], [], []
    for w, dt, *layout in pieces:
        n = w.shape[1]
        n_pad = -(-n // LANES) * LANES
        if n_pad != n:
            w = jnp.pad(w, ((0, 0), (0, n_pad - n)))
        ws.append(w.astype(BF16))
        if layout:
            hd, t, per_seq = 2 * DIFF_HEAD_DIM, ATTN_TILE, seq_len // TOKEN_TILE
            blk = (None, n // hd, TOKEN_TILE // t, hd + DIFF_AUX_ROWS, t)
            out_specs.append(pl.BlockSpec(blk, lambda i: (i // per_seq, 0, i % per_seq, 0, 0)))
            out_shapes.append(jax.ShapeDtypeStruct((T // seq_len, n // hd, seq_len // t) + blk[3:], dt))
            continue
        out_specs.append(pl.BlockSpec((TOKEN_TILE, n_pad), lambda i: (i, 0)))
        out_shapes.append(jax.ShapeDtypeStruct((T, n_pad), dt))
    in_specs = [pl.BlockSpec((TOKEN_TILE, D_MODEL), lambda i: (i, 0)), _resident((1, D_MODEL))]
    in_specs += [_resident(w.shape) for w in ws]
    return pl.pallas_call(
        functools.partial(_norm_proj_kernel, n_out=len(ws)),
        grid=(T // TOKEN_TILE,),
        in_specs=in_specs,
        out_specs=out_specs,
        out_shape=out_shapes,
        compiler_params=_params("parallel"),
        name="norm_proj",
    )(x, g.reshape(1, D_MODEL), *ws)


def _attn_out_kernel(x_ref, mix_ref, xq_ref, mk_ref, mv_ref, w1_ref, w2_ref, g_ref, wg_ref, wu_ref, wd_ref,
                     *rest, final):
    xq = xq_ref[...]
    mk = mk_ref[...]
    mv = mv_ref[...]
    outs = []
    for h in range(X_HEADS):
        sl = slice(h * X_HEAD_DIM, (h + 1) * X_HEAD_DIM)
        s = _dot_nt(xq[:, sl], mk[:, sl])
        e = jnp.exp2(s - jnp.max(s, axis=-1, keepdims=True))
        o = _dot(e.astype(BF16), mv[:, sl]) * (1.0 / jnp.sum(e, axis=-1, keepdims=True))
        outs.append(o.astype(BF16))
    xo = jnp.concatenate(outs, axis=-1)
    x = x_ref[...] + _dot(mix_ref[...], w1_ref[...]) + _dot(xo, w2_ref[...])
    y = _ffn_half_step(x, g_ref, wg_ref, wu_ref, wd_ref)
    if final:
        fg_ref, o_ref = rest
        o_ref[...] = _rms(y, fg_ref[...], NORM_EPS)
    else:
        nu_ref, nd_ref, o_ref, nu_out_ref, nd_out_ref = rest
        o_ref[...] = y
        nu_out_ref[...] = nu_ref[...].astype(BF16)
        nd_out_ref[...] = nd_ref[...].astype(BF16)


def _attn_out_ffn(x, mix, xq, mem_k, mem_v, w_out, ffn_weights, final_g=None, next_cast=None):
    B, S, _ = x.shape
    M = mem_k.shape[1]
    w1 = w_out[:D_MODEL].astype(BF16)
    w2 = w_out[D_MODEL:].astype(BF16)
    w_args, w_specs = ffn_weights
    tok = lambda n: pl.BlockSpec((None, TOKEN_TILE, n), lambda b, i: (b, i, 0))
    mem = pl.BlockSpec((None, M, X_WIDTH), lambda b, i: (b, 0, 0))
    out_specs, out_shapes = [tok(D_MODEL)], [jax.ShapeDtypeStruct((B, S, D_MODEL), F32)]
    per_seq = S // TOKEN_TILE
    if final_g is not None:
        w_args = w_args + [final_g.reshape(1, D_MODEL)]
        w_specs = w_specs + [_resident((1, D_MODEL))]
    else:
        c_args, c_in, c_out, c_shapes = _next_weights_cast(*next_cast, step_of=lambda b, i: b * per_seq + i,
                                                           n_steps=B * per_seq)
        w_args, w_specs = w_args + c_args, w_specs + c_in
        out_specs, out_shapes = out_specs + c_out, out_shapes + c_shapes
    outs = pl.pallas_call(
        functools.partial(_attn_out_kernel, final=final_g is not None),
        grid=(B, per_seq),
        in_specs=[tok(D_MODEL), tok(D_MODEL), tok(X_WIDTH), mem, mem, _resident(w1.shape), _resident(w2.shape)]
        + w_specs,
        out_specs=out_specs,
        out_shape=out_shapes,
        compiler_params=_params("parallel", "parallel"),
        name="attn_out_ffn",
    )(x, mix, xq, mem_k, mem_v, w1, w2, *w_args)
    if final_g is not None:
        return outs[0], None, None
    return outs[0], outs[1], outs[2]


def _pool_kernel(x_ref, g_ref, wu_ref, wq_ref, wg_ref, sc_ref, o_ref, xq_ref, ext_ref, a_ref, b_ref, *, ts):
    j = pl.program_id(1)
    H = POOL_HALO

    @pl.when(j == 0)
    def _():
        ext_ref[0:2 * H, :] = jnp.zeros((2 * H, D_MODEL), F32)
        a_ref[0:H, :] = jnp.zeros((H, POOL_GROUP_DIM), F32)
        b_ref[0:H, :] = jnp.zeros((H, POOL_GROUP_DIM), F32)

    h = _rms(x_ref[...], g_ref[...], NORM_EPS).astype(BF16)
    xq_ref[...] = _dot(h, wq_ref[...]).astype(xq_ref.dtype)
    ext_ref[2 * H:2 * H + ts, :] = _dot(h, wu_ref[...])
    pos = j * ts + lax.broadcasted_iota(jnp.int32, (ts, 1), 0)
    n = ts + H
    for g, w in enumerate(POOL_WINDOWS):
        sl = slice(g * POOL_GROUP_DIM, (g + 1) * POOL_GROUP_DIM)
        src, cols, d, bufs = ext_ref, sl, 1, [a_ref, b_ref]
        while d < w:
            dst = bufs.pop(0)
            dst[H:H + n, :] = src[H:H + n, cols] + src[H - d:H - d + n, cols]
            bufs.append(dst)
            src, cols, d = dst, slice(None), 2 * d
        inv = 1.0 / jnp.minimum(pos + 1, w).astype(F32)
        pooled = (src[2 * H:2 * H + ts, cols] * inv - ext_ref[2 * H:2 * H + ts, sl]).astype(BF16)
        o_ref[:, sl] = (_dot(pooled, wg_ref[g]) * sc_ref[:, sl]).astype(o_ref.dtype)
    ext_ref[H:2 * H, :] = ext_ref[ts + H:ts + 2 * H, :]


def _pool_mix(x, g, w_in, w_group, scale):
    B, S, _ = x.shape
    ts = TOKEN_TILE
    tok = lambda n: pl.BlockSpec((None, ts, n), lambda b, j: (b, j, 0))
    return pl.pallas_call(
        functools.partial(_pool_kernel, ts=ts),
        grid=(B, S // ts),
        in_specs=[tok(D_MODEL), _resident((1, D_MODEL)), _resident((D_MODEL, D_MODEL)), _resident((D_MODEL, X_WIDTH)),
                  _resident(w_group.shape), _resident((1, D_MODEL))],
        out_specs=[tok(D_MODEL), tok(X_WIDTH)],
        out_shape=[jax.ShapeDtypeStruct((B, S, D_MODEL), BF16), jax.ShapeDtypeStruct((B, S, X_WIDTH), BF16)],
        scratch_shapes=[pltpu.VMEM((ts + 2 * POOL_HALO, D_MODEL), F32),
                        pltpu.VMEM((ts + 2 * POOL_HALO, POOL_GROUP_DIM), F32),
                        pltpu.VMEM((ts + 2 * POOL_HALO, POOL_GROUP_DIM), F32)],
        compiler_params=_params("parallel", "arbitrary"),
        name="pool_mix",
    )(x, g.reshape(1, D_MODEL), w_in[:, :D_MODEL].astype(BF16), w_in[:, D_MODEL:].astype(BF16),
      w_group.astype(BF16), scale.reshape(1, D_MODEL))


def _lane_features(lane, groups):
    out = jnp.zeros(lane.shape, F32)
    for g, v in enumerate(groups):
        for r in range(DIFF_SLOPE_TERMS):
            out = jnp.where(lane == DIFF_SLOPE_TERMS * g + r, v[r] if isinstance(v, list) else v, out)
    return out


def _diff_kernel(slopes_ref, lam_ref, q_ref, k_ref, vt_ref, ng_ref, o_ref,
                 ka_ref, qf_ref, acc_ref, m_ref, *, t, n_tiles, lam_init):
    hg = pl.program_id(1)
    qi = pl.program_id(2)
    hd = 2 * DIFF_HEAD_DIM
    lane = lax.broadcasted_iota(jnp.int32, (t, hd), 1)
    rowf = lax.broadcasted_iota(jnp.int32, (t, hd), 0).astype(F32)
    own = (lane < DIFF_HEAD_DIM, lane >= DIFF_HEAD_DIM)
    heads = range(DIFF_HEAD_BLOCK)
    c_terms = [[slopes_ref[hg * DIFF_HEAD_BLOCK + hb, r] for r in range(DIFF_SLOPE_TERMS)] for hb in heads]
    chains = [(hb, c) for hb in heads for c in range(2)]

    @pl.when(qi == 0)
    def _():
        key_feat = []
        for hb in heads:
            neg = [-cr for cr in c_terms[hb]]
            key_feat.append(_lane_features(lane, [rowf, 0.0, neg, [cr * t for cr in neg]]))
        key_tile_lanes = (lane >= DIFF_SLOPE_TERMS) & (lane < 2 * DIFF_SLOPE_TERMS)

        def build(j, carry):
            start = pl.multiple_of(j * t, t)
            for hb in heads:
                feat = jnp.where(key_tile_lanes, jnp.asarray(j, F32), key_feat[hb])
                ka_ref[hb, pl.ds(start, t), 0:hd] = k_ref[pl.ds(start, t), hb * hd:(hb + 1) * hd]
                ka_ref[hb, pl.ds(start, t), hd:2 * hd] = feat.astype(BF16)
            return carry

        lax.fori_loop(0, n_tiles, build, 0)
        for hb in heads:
            qf_ref[hb] = _lane_features(lane, [c_terms[hb], [cr * t for cr in c_terms[hb]], rowf, 0.0])

    qa = {}
    tile_lanes = (lane >= 3 * DIFF_SLOPE_TERMS) & (lane < 4 * DIFF_SLOPE_TERMS)
    for hb in heads:
        q = q_ref[:, hb * hd:(hb + 1) * hd]
        feat = jnp.where(tile_lanes, jnp.asarray(qi, F32), qf_ref[hb])
        for c in range(2):
            qa[hb, c] = jnp.concatenate([jnp.where(own[c], q, jnp.zeros_like(q)), feat.astype(BF16)], axis=1)
    acc_ref[...] = jnp.zeros_like(acc_ref)
    m_ref[...] = jnp.full(m_ref.shape, -jnp.inf, F32)

    def tiles(kjs, masked):
        starts = [pl.multiple_of(kj * t, t) for kj in kjs]
        scores = {}
        for hb, c in chains:
            parts = []
            for start in starts:
                s = _dot_nt(ka_ref[hb, pl.ds(start, t), :], qa[hb, c])
                if masked:
                    key = lax.broadcasted_iota(jnp.int32, (t, t), 0)
                    qry = lax.broadcasted_iota(jnp.int32, (t, t), 1)
                    s = jnp.where(key <= qry, s, -jnp.inf)
                parts.append(s)
            scores[hb, c] = parts
        probs = {}
        for hb, c in chains:
            parts = scores[hb, c]
            m_prev = m_ref[hb, c]
            m_new = m_prev
            for s in parts:
                m_new = jnp.maximum(m_new, jnp.max(s, axis=0, keepdims=True))
            alpha = jnp.exp2(m_prev - m_new)
            m_ref[hb, c] = m_new
            probs[hb, c] = (alpha, jnp.concatenate([jnp.exp2(s - m_new).astype(BF16) for s in parts], axis=0))
        for hb, c in chains:
            alpha, p = probs[hb, c]
            vt = jnp.concatenate([vt_ref[hb, kj] for kj in kjs], axis=1)
            acc_ref[hb, c] = alpha * acc_ref[hb, c] + _dot(vt, p)

    def body(pj, carry):
        tiles([2 * pj, 2 * pj + 1], masked=False)
        return carry

    lax.fori_loop(0, qi // 2, body, 0)

    @pl.when(qi % 2 == 1)
    def _():
        tiles([qi - 1], masked=False)

    tiles([qi], masked=True)

    lp = lam_ref[...]
    lam = (jnp.exp(jnp.sum(lp[0:1] * lp[1:2], axis=-1, keepdims=True))
           - jnp.exp(jnp.sum(lp[2:3] * lp[3:4], axis=-1, keepdims=True)) + lam_init)
    for hb in range(DIFF_HEAD_BLOCK):
        a0, a1 = acc_ref[hb, 0], acc_ref[hb, 1]
        o_t = (a0[:hd] * (1.0 / a0[hd:hd + 1]) - lam * (a1[:hd] * (1.0 / a1[hd:hd + 1])))
        o_ref[:, hb * hd:(hb + 1) * hd] = (_rms(o_t.T, ng_ref[...], SUBLN_EPS) * (1.0 - lam_init)).astype(o_ref.dtype)


def _diff_mix(q, k, v_t, lam_p, norm_g, layer_idx):
    B, S, _ = q.shape
    t = ATTN_TILE
    n_tiles = S // t
    hd = 2 * DIFF_HEAD_DIM
    hb = DIFF_HEAD_BLOCK
    lam_init = 0.8 - 0.6 * math.exp(-0.3 * layer_idx)
    slopes = jnp.asarray([2.0 ** (-8.0 * (h + 1) / DIFF_HEADS) for h in range(DIFF_HEADS)], dtype=F32)
    rest = slopes * math.log2(math.e)
    terms = []
    for _ in range(DIFF_SLOPE_TERMS):
        terms.append(rest.astype(BF16).astype(F32))
        rest = rest - terms[-1]
    slopes = jnp.stack(terms, axis=1)
    qspec = pl.BlockSpec((None, t, hb * hd), lambda b, h, i: (b, i, h))
    return pl.pallas_call(
        functools.partial(_diff_kernel, t=t, n_tiles=n_tiles, lam_init=lam_init),
        grid=(B, DIFF_HEADS // hb, n_tiles),
        in_specs=[pl.BlockSpec(memory_space=pltpu.SMEM), _resident(lam_p.shape), qspec,
                  pl.BlockSpec((None, S, hb * hd), lambda b, h, i: (b, 0, h), pipeline_mode=pl.Buffered(1)),
                  pl.BlockSpec((None, hb, n_tiles, hd + DIFF_AUX_ROWS, t), lambda b, h, i: (b, h, 0, 0, 0),
                               pipeline_mode=pl.Buffered(1)),
                  _resident((1, hd))],
        out_specs=qspec,
        out_shape=jax.ShapeDtypeStruct((B, S, D_MODEL), BF16),
        scratch_shapes=[pltpu.VMEM((hb, S, 2 * hd), BF16), pltpu.VMEM((hb, t, hd), F32),
                        pltpu.VMEM((hb, 2, hd + DIFF_AUX_ROWS, t), F32),
                        pltpu.VMEM((hb, 2, 1, t), F32)],
        compiler_params=_params("parallel", "parallel", "arbitrary"),
        name="diff_attn",
    )(slopes, lam_p, q, k, v_t, norm_g.reshape(1, hd))


def _mlstm_kernel(qk_ref, v_ref, og_ref, gc_ref, gr_ref, cw_ref, gbc_ref, gbr_ref, ng_ref, o_ref,
                  ext_ref, c_ref, n_ref, m_ref, *, L):
    c = pl.program_id(1)
    W = 2 * D_MODEL

    @pl.when(c == 0)
    def _():
        ext_ref[0:CONV_HALO, :] = jnp.zeros((CONV_HALO, W), F32)
        c_ref[...] = jnp.zeros_like(c_ref)
        n_ref[...] = jnp.zeros_like(n_ref)
        m_ref[...] = jnp.zeros_like(m_ref)

    ext_ref[CONV_HALO:CONV_HALO + L, :] = qk_ref[...]
    conv = cw_ref[MLSTM_CONV - 1:MLSTM_CONV, :] * qk_ref[...]
    for j in range(1, MLSTM_CONV):
        conv = conv + (cw_ref[MLSTM_CONV - 1 - j:MLSTM_CONV - j, :]
                       * ext_ref[CONV_HALO - j:CONV_HALO - j + L, :])
    ext_ref[0:CONV_HALO, :] = ext_ref[L:L + CONV_HALO, :]
    qk = conv * _sigmoid(conv)

    gc = gc_ref[...] + gbc_ref[...]
    gr = gr_ref[...] + gbr_ref[...]
    b_c = _cumsum_rows(_log_sigmoid(gc))
    b_r = _cumsum_lanes(_log_sigmoid(gr))
    row = lax.broadcasted_iota(jnp.int32, (L, L), 0)
    col = lax.broadcasted_iota(jnp.int32, (L, L), 1)
    causal = col <= row

    for hd in range(MLSTM_HEADS):
        sl = slice(hd * MLSTM_HEAD_DIM, (hd + 1) * MLSTM_HEAD_DIM)
        q32 = qk[:, sl] * (MLSTM_HEAD_DIM ** -0.5)
        k32 = qk[:, D_MODEL + hd * MLSTM_HEAD_DIM:D_MODEL + (hd + 1) * MLSTM_HEAD_DIM]
        vh = v_ref[:, sl]
        fi = MLSTM_HEADS + hd
        bcol, icol = b_c[:, fi:fi + 1], gc[:, hd:hd + 1]
        brow, irow = b_r[fi:fi + 1, :], gr[hd:hd + 1, :]
        b_last = brow[:, L - 1:L]
        m = m_ref[hd:hd + 1, 0:1]
        cmat = c_ref[hd]
        nrow = n_ref[hd]

        dmat = jnp.where(causal, bcol - brow + irow, -jnp.inf)
        inter = bcol + m
        m_t = jnp.maximum(inter, jnp.max(dmat, axis=-1, keepdims=True))
        dec = jnp.exp(inter - m_t)
        qb = q32.astype(BF16)
        k_t = k32.T
        sqk = _dot(qb, k_t.astype(BF16)) * jnp.exp(dmat - m_t)
        num = dec * _dot(qb, cmat.astype(BF16)) + _dot(sqk.astype(BF16), vh)
        den = dec * jnp.sum(q32 * nrow, axis=-1, keepdims=True) + jnp.sum(sqk, axis=-1, keepdims=True)
        hc = num * (1.0 / jnp.maximum(jnp.abs(den), jnp.exp(-m_t)))

        gs_r = b_last - brow + irow
        gs_c = b_last - bcol + icol
        m_new = jnp.maximum(b_last + m, jnp.max(gs_r, axis=-1, keepdims=True))
        carry_dec = jnp.exp(b_last + m - m_new)
        c_ref[hd] = carry_dec * cmat + _dot((k_t * jnp.exp(gs_r - m_new)).astype(BF16), vh)
        n_ref[hd] = carry_dec * nrow + jnp.sum(k32 * jnp.exp(gs_c - m_new), axis=0, keepdims=True)
        m_ref[hd:hd + 1, :] = jnp.broadcast_to(m_new, (1, LANES))

        o_ref[:, sl] = (_rms(hc, ng_ref[:, sl], NORM_EPS) * _sigmoid(og_ref[:, sl])).astype(o_ref.dtype)


def _mlstm_mix(qk, v, og, gates, conv_w, gate_b, norm_g):
    B, S, _ = qk.shape
    L = MLSTM_CHUNK
    H = MLSTM_HEADS
    gates_r = jnp.transpose(gates[..., :2 * H], (0, 2, 1))
    gb = gate_b.reshape(2 * H)
    gb_c = jnp.pad(gb, (0, LANES - 2 * H)).reshape(1, LANES)
    gb_r = gb.reshape(2 * H, 1)
    tok = lambda n: pl.BlockSpec((None, L, n), lambda b, c: (b, c, 0))
    return pl.pallas_call(
        functools.partial(_mlstm_kernel, L=L),
        grid=(B, S // L),
        in_specs=[tok(2 * D_MODEL), tok(D_MODEL), tok(D_MODEL), tok(LANES),
                  pl.BlockSpec((None, 2 * H, L), lambda b, c: (b, 0, c)),
                  _resident(conv_w.shape), _resident((1, LANES)), _resident((2 * H, 1)), _resident((1, D_MODEL))],
        out_specs=tok(D_MODEL),
        out_shape=jax.ShapeDtypeStruct((B, S, D_MODEL), BF16),
        scratch_shapes=[pltpu.VMEM((L + CONV_HALO, 2 * D_MODEL), F32),
                        pltpu.VMEM((H, MLSTM_HEAD_DIM, MLSTM_HEAD_DIM), F32),
                        pltpu.VMEM((H, 1, MLSTM_HEAD_DIM), F32),
                        pltpu.VMEM((8, LANES), F32)],
        compiler_params=_params("parallel", "arbitrary"),
        name="mlstm",
    )(qk, v, og, gates, gates_r, conv_w, gb_c, gb_r, norm_g.reshape(1, D_MODEL))


def _gla_kernel(q_ref, k_ref, v_ref, g_ref, z_ref, w2_ref, gb_ref, ng_ref, o_ref, st_ref, *, L):
    @pl.when(pl.program_id(1) == 0)
    def _():
        st_ref[...] = jnp.zeros_like(st_ref)

    for ci in range(GLA_CHUNKS_PER_STEP):
        _gla_chunk(q_ref, k_ref, v_ref, g_ref, z_ref, w2_ref, gb_ref, ng_ref, o_ref, st_ref,
                   slice(ci * L, (ci + 1) * L), L)


def _gla_chunk(q_ref, k_ref, v_ref, g_ref, z_ref, w2_ref, gb_ref, ng_ref, o_ref, st_ref, rows, L):
    log_a = _log_sigmoid(_dot(z_ref[rows, :], w2_ref[...]) + gb_ref[...]) * (1.0 / GLA_TAU)
    b_all = _cumsum_rows(log_a) * LOG2E
    sub = GLA_SUBCHUNK
    row = lax.broadcasted_iota(jnp.int32, (sub, L), 0)
    col = lax.broadcasted_iota(jnp.int32, (sub, L), 1)
    ones = jnp.ones((GLA_KEY_DIM, LANES), BF16)

    for hd in range(GLA_HEADS):
        ks = slice(hd * GLA_KEY_DIM, (hd + 1) * GLA_KEY_DIM)
        vs = slice(hd * GLA_VALUE_DIM, (hd + 1) * GLA_VALUE_DIM)
        b = b_all[:, ks]
        vh = v_ref[rows, vs]
        st = st_ref[hd]
        q = q_ref[rows, ks] * (GLA_KEY_DIM ** -0.5)
        k = k_ref[rows, ks]
        b_t = b.T
        k_t = k.T
        b_last = b_t[:, L - 1:L]
        blocks = []
        for i in range(L // sub):
            r0 = i * sub
            q_i, k_i, b_i = q[r0:r0 + sub], k[r0:r0 + sub], b[r0:r0 + sub]
            terms, offs = [], []
            for s in range(sub):
                t0 = s // SUBLANES * SUBLANES
                offs.append(sum(x.shape[0] for x in terms))
                terms.append(q_i[t0:] * k_i[s:s + 1] * jnp.exp2(b_i[t0:] - b_i[s:s + 1]))
            sums = _dot(jnp.concatenate(terms, axis=0).astype(BF16), ones)
            groups = []
            for t0 in range(0, sub, SUBLANES):
                a_g = jnp.zeros((SUBLANES, L), F32)
                for s in range(min(sub, t0 + SUBLANES)):
                    lo = offs[s] + t0 - s // SUBLANES * SUBLANES
                    a_g = jnp.where(col[:SUBLANES] == r0 + s, sums[lo:lo + SUBLANES], a_g)
                groups.append(a_g)
            a_i = jnp.where(col <= r0 + row, jnp.concatenate(groups, axis=0), 0.0)
            if i > 0:
                ref_row, ref_col = b[r0 - 1:r0], b_t[:, r0 - 1:r0]
                q_s = (q_i * jnp.exp2(b_i - ref_row)).astype(BF16)
                k_s = (k_t * jnp.exp2(ref_col - b_t)).astype(BF16)
                a_i = jnp.where(col < r0, _dot(q_s, k_s), a_i)
            blocks.append(a_i)
        amat = jnp.concatenate(blocks, axis=0)
        o = _dot((q * jnp.exp2(b)).astype(BF16), st.astype(BF16)) + _dot(amat.astype(BF16), vh)
        st_ref[hd] = jnp.exp2(b_last) * st + _dot((k_t * jnp.exp2(b_last - b_t)).astype(BF16), vh)
        gate = g_ref[rows, vs]
        o_ref[rows, vs] = (_rms(o, ng_ref[:, vs], NORM_EPS) * (gate * _sigmoid(gate))).astype(o_ref.dtype)


def _gla_mix(q, k, v, g, z, gate_w2, gate_b, norm_g):
    B, S, _ = q.shape
    L = GLA_CHUNK
    kw =GLA_HEADS * GLA_KEY_DIM
    w2 = jnp.pad(gate_w2, ((0, LANES - GLA_GATE_RANK), (0, 0))).astype(BF16)
    step = L * GLA_CHUNKS_PER_STEP
    tok = lambda n: pl.BlockSpec((None, step, n), lambda b, c: (b, c, 0))
    return pl.pallas_call(
        functools.partial(_gla_kernel, L=L),
        grid=(B, S // step),
        in_specs=[tok(kw), tok(kw), tok(D_MODEL), tok(D_MODEL), tok(LANES),
                  _resident(w2.shape), _resident((1, kw)), _resident((1, D_MODEL))],
        out_specs=tok(D_MODEL),
        out_shape=jax.ShapeDtypeStruct((B, S, D_MODEL), BF16),
        scratch_shapes=[pltpu.VMEM((GLA_HEADS, GLA_KEY_DIM, GLA_VALUE_DIM), F32)],
        compiler_params=_params("parallel", "arbitrary"),
        name="gla",
    )(q, k, v, g, z, w2, gate_b.reshape(1, kw), norm_g.reshape(1, D_MODEL))


def kernel(x, mem, norm_g, ffn_w_up, ffn_w_down, mem_norm_g, mem_w_kv, pool_w_in, pool_w_group, pool_scale, pool_w_out, diff_w_in, diff_lambda, diff_norm_g, diff_w_out, mlstm_w_in, mlstm_conv_w, mlstm_gate_b, mlstm_norm_g, mlstm_w_out, gla_w_in, gla_gate_w2, gla_gate_b, gla_norm_g, gla_w_out, final_norm_g):
    B, S, D = x.shape
    M = mem.shape[1]
    T = B * S
    depth = norm_g.shape[0]
    n_mixers = 4
    mem2 = mem.reshape(B * M, D)
    x2 = x.reshape(T, D)
    norm_g4 = norm_g.reshape(depth, 3, 1, D)
    w_up, w_down = ffn_w_up[0, 0].astype(BF16), ffn_w_down[0, 0].astype(BF16)

    def seq(a):
        return a.reshape(B, S, a.shape[-1])

    for i in range(depth):
        kind, j = i % n_mixers, i // n_mixers
        k_scale = X_HEAD_DIM ** -0.5 * LOG2E
        mem_k, mem_v = _norm_proj(mem2, mem_norm_g, [(mem_w_kv[i][:, :X_WIDTH] * k_scale, BF16),
                                                     (mem_w_kv[i][:, X_WIDTH:], BF16)])
        mem_k = mem_k.reshape(B, M, X_WIDTH)
        mem_v = mem_v.reshape(B, M, X_WIDTH)
        x2, w_up, w_down = _ffn(x2, _ffn_weights(norm_g4, w_up, w_down, i, 0), (ffn_w_up, ffn_w_down, i, 1))
        if kind == 0:
            mix, xq = _pool_mix(seq(x2), norm_g[i, 1], pool_w_in[j], pool_w_group[j], pool_scale[j])
            w_out = pool_w_out[j]
        elif kind == 1:
            w = diff_w_in[j]
            q_scale = DIFF_HEAD_DIM ** -0.5 * math.log2(math.e)
            q, k, v_t, xq = _norm_proj(x2, norm_g[i, 1], [
                (w[:, :D] * q_scale, BF16), (w[:, D:2 * D], BF16), (w[:, 2 * D:3 * D], BF16, "values_t"),
                (w[:, 3 * D:], BF16)], seq_len=S)
            mix = _diff_mix(seq(q), seq(k), v_t, diff_lambda[j], diff_norm_g[j], i)
            w_out = diff_w_out[j]
        elif kind == 2:
            w = mlstm_w_in[j]
            ng = 2 * MLSTM_HEADS
            qk, v, og, gates, xq = _norm_proj(x2, norm_g[i, 1], [
                (w[:, :2 * D], F32), (w[:, 2 * D:3 * D], BF16), (w[:, 3 * D:4 * D], F32),
                (w[:, 4 * D:4 * D + ng], F32), (w[:, 4 * D + ng:], BF16)])
            mix = _mlstm_mix(seq(qk), seq(v), seq(og), seq(gates), mlstm_conv_w[j], mlstm_gate_b[j], mlstm_norm_g[j])
            w_out = mlstm_w_out[j]
        else:
            w = gla_w_in[j]
            kw = GLA_HEADS * GLA_KEY_DIM
            o0 = 2 * kw + 2 * D
            q, k, v, g, z, xq = _norm_proj(x2, norm_g[i, 1], [
                (w[:, :kw], F32), (w[:, kw:2 * kw], F32), (w[:, 2 * kw:2 * kw + D], BF16),
                (w[:, 2 * kw + D:o0], F32), (w[:, o0:o0 + GLA_GATE_RANK], BF16), (w[:, o0 + GLA_GATE_RANK:], BF16)])
            mix = _gla_mix(seq(q), seq(k), seq(v), seq(g), seq(z), gla_gate_w2[j], gla_gate_b[j], gla_norm_g[j])
            w_out = gla_w_out[j]
        last = i == depth - 1
        x3, w_up, w_down = _attn_out_ffn(seq(x2), mix, seq(xq), mem_k, mem_v, w_out,
                                         _ffn_weights(norm_g4, w_up, w_down, i, 2),
                                         final_g=final_norm_g if last else None,
                                         next_cast=None if last else (ffn_w_up, ffn_w_down, i + 1, 0))
        x2 = x3.reshape(T, D)
    return x2.reshape(B, S, D)
```

```python
import functools
import math

import jax
import jax.numpy as jnp
from jax import lax
from jax.experimental import pallas as pl
from jax.experimental.pallas import tpu as pltpu

F32 = jnp.float32
BF16 = jnp.bfloat16

D_MODEL = 1024
D_FF = 2816
X_HEADS = 4
X_HEAD_DIM = 128
X_WIDTH = X_HEADS * X_HEAD_DIM
POOL_WINDOWS = (2, 4, 8, 16)
POOL_GROUP_DIM = D_MODEL // len(POOL_WINDOWS)
POOL_HALO = 16
DIFF_HEAD_DIM = 64
DIFF_HEADS = 8
DIFF_HEAD_BLOCK = 8
DIFF_SLOPE_TERMS = 3
DIFF_AUX_ROWS = 16
MLSTM_HEADS = 4
MLSTM_HEAD_DIM = 256
MLSTM_CONV = 4
CONV_HALO = 8
GLA_HEADS = 4
GLA_KEY_DIM = 128
GLA_VALUE_DIM = 256
GLA_GATE_RANK = 16
GLA_TAU = 16.0
GLA_SUBCHUNK = 16
LOG2E = math.log2(math.e)
NORM_EPS = 1e-6
SUBLN_EPS = 1e-5
LANES = 128
SUBLANES = 8

VMEM_LIMIT = 56 * 1024 * 1024
TOKEN_TILE = 512
FF_CHUNKS = (0, 1536, D_FF)
FFN_TOKEN_TILE = 1024
FFN_WIDE_CHUNKS = (0, 768, 1536, 2304, D_FF)
ATTN_TILE = 256
MLSTM_CHUNK = 256
MLSTM_CHUNKS_PER_STEP = 2
GLA_CHUNK = 128
GLA_CHUNKS_PER_STEP = 2


def _params(*sem):
    return pltpu.CompilerParams(dimension_semantics=sem, vmem_limit_bytes=VMEM_LIMIT)


def _resident(shape):
    nd = len(shape)
    return pl.BlockSpec(shape, lambda *_: (0,) * nd, pipeline_mode=pl.Buffered(1))


def _rms(x, g, eps):
    return x * lax.rsqrt(jnp.mean(x * x, axis=-1, keepdims=True) + eps) * g


def _sigmoid(x):
    return 1.0 / (1.0 + jnp.exp(-x))


def _log_sigmoid(x):
    return jnp.minimum(x, 0.0) - jnp.log(1.0 + jnp.exp(-jnp.abs(x)))


def _dot(a, b):
    return jnp.dot(a, b, preferred_element_type=F32)


def _dot_nt(a, b):
    return lax.dot_general(a, b, (((1,), (1,)), ((), ())), preferred_element_type=F32)


def _split3(x):
    hi = x.astype(BF16)
    r1 = x - hi.astype(F32)
    mid = r1.astype(BF16)
    lo = (r1 - mid.astype(F32)).astype(BF16)
    return hi, mid, lo


def _tri(n, upper):
    r = lax.broadcasted_iota(jnp.int32, (n, n), 0)
    c = lax.broadcasted_iota(jnp.int32, (n, n), 1)
    keep = (r <= c) if upper else (c <= r)
    return jnp.where(keep, 1.0, 0.0).astype(BF16)


def _cumsum_rows(x):
    tri = _tri(x.shape[0], upper=False)
    hi, mid, lo = _split3(x)
    return _dot(tri, hi) + _dot(tri, mid) + _dot(tri, lo)


def _cumsum_lanes(x):
    tri = _tri(x.shape[1], upper=True)
    hi, mid, lo = _split3(x)
    return _dot(hi, tri) + _dot(mid, tri) + _dot(lo, tri)


def _ffn_half_step(x, g_ref, wg_ref, wu_ref, wd_ref, chunks=FF_CHUNKS):
    h = _rms(x, g_ref[...], NORM_EPS).astype(BF16)
    acc = jnp.zeros_like(x)
    for lo, hi in zip(chunks[:-1], chunks[1:]):
        sl = slice(lo, hi)
        gate = _dot(h, wg_ref[:, sl])
        up = _dot(h, wu_ref[:, sl])
        act = (gate * _sigmoid(gate) * up).astype(BF16)
        acc = acc + _dot(act, wd_ref[sl, :])
    return x + 0.5 * acc


def _stacked(shape, lead, tail=None):
    tail = tail or (0,) * len(shape)
    return pl.BlockSpec((None,) * len(lead) + shape, lambda *_: lead + tail, pipeline_mode=pl.Buffered(1))


def _ffn_weights(norm_g, w_up, w_down, layer, norm_slot):
    args = [norm_g, w_up, w_up, w_down]
    specs = [_stacked((1, D_MODEL), (layer, norm_slot)), _stacked((D_MODEL, D_FF), (), (0, 0)),
             _stacked((D_MODEL, D_FF), (), (0, 1)), _stacked((D_FF, D_MODEL), ())]
    return args, specs


def _next_weights_cast(w_up_all, w_down_all, layer, slot, step_of, n_steps):
    args, in_specs, out_specs, out_shapes = [w_up_all, w_down_all], [], [], []
    for w in args:
        n_rows, n_cols = w.shape[2:]
        n_slabs = max(s for s in range(1, n_steps + 1) if n_rows % (16 * s) == 0)
        slab = lambda *idx, n_slabs=n_slabs: jnp.minimum(step_of(*idx), n_slabs - 1)
        in_specs.append(pl.BlockSpec((None, None, n_rows // n_slabs, n_cols),
                                     lambda *idx, slab=slab: (layer, slot, slab(*idx), 0)))
        out_specs.append(pl.BlockSpec((n_rows // n_slabs, n_cols), lambda *idx, slab=slab: (slab(*idx), 0)))
        out_shapes.append(jax.ShapeDtypeStruct((n_rows, n_cols), BF16))
    return args, in_specs, out_specs, out_shapes


def _ffn_kernel(x_ref, g_ref, wg_ref, wu_ref, wd_ref, nu_ref, nd_ref, o_ref, nu_out_ref, nd_out_ref):
    o_ref[...] = _ffn_half_step(x_ref[...], g_ref, wg_ref, wu_ref, wd_ref, FFN_WIDE_CHUNKS)
    nu_out_ref[...] = nu_ref[...].astype(BF16)
    nd_out_ref[...] = nd_ref[...].astype(BF16)


def _ffn(x, ffn_weights, next_cast):
    T = x.shape[0]
    w_args, w_specs = ffn_weights
    n_steps = T // FFN_TOKEN_TILE
    c_args, c_in, c_out, c_shapes = _next_weights_cast(*next_cast, step_of=lambda i: i, n_steps=n_steps)
    tok = pl.BlockSpec((FFN_TOKEN_TILE, D_MODEL), lambda i: (i, 0))
    y, w_up, w_down = pl.pallas_call(
        _ffn_kernel,
        grid=(n_steps,),
        in_specs=[tok] + w_specs + c_in,
        out_specs=[tok] + c_out,
        out_shape=[jax.ShapeDtypeStruct((T, D_MODEL), F32)] + c_shapes,
        compiler_params=_params("parallel"),
        name="ffn",
    )(x, *w_args, *c_args)
    return y, w_up, w_down


def _norm_proj_kernel(x_ref, g_ref, *refs, n_out):
    h = _rms(x_ref[...], g_ref[...], NORM_EPS).astype(BF16)
    for w_ref, o_ref in zip(refs[:n_out], refs[n_out:]):
        y = _dot(h, w_ref[...])
        if len(o_ref.shape) == 2:
            o_ref[...] = y.astype(o_ref.dtype)
            continue
        n_heads, n_sub, rows, t = o_ref.shape
        hd = rows - DIFF_AUX_ROWS
        aux = jnp.where(lax.broadcasted_iota(jnp.int32, (DIFF_AUX_ROWS, t), 0) == 0, 1.0, 0.0).astype(o_ref.dtype)
        for head in range(n_heads):
            for s in range(n_sub):
                o_ref[head, s, 0:hd, :] = y[s * t:(s + 1) * t, head * hd:(head + 1) * hd].T.astype(o_ref.dtype)
                o_ref[head, s, hd:rows, :] = aux


def _norm_proj(x, g, pieces, seq_len=None):
    T = x.shape[0]
    ws, out_specs, out_shapes = [], [], []
    for w, dt, *layout in pieces:
        n = w.shape[1]
        n_pad = -(-n // LANES) * LANES
        if n_pad != n:
            w = jnp.pad(w, ((0, 0), (0, n_pad - n)))
        ws.append(w.astype(BF16))
        if layout:
            hd, t, per_seq = 2 * DIFF_HEAD_DIM, ATTN_TILE, seq_len // TOKEN_TILE
            blk = (None, n // hd, TOKEN_TILE // t, hd + DIFF_AUX_ROWS, t)
            out_specs.append(pl.BlockSpec(blk, lambda i: (i // per_seq, 0, i % per_seq, 0, 0)))
            out_shapes.append(jax.ShapeDtypeStruct((T // seq_len, n // hd, seq_len // t) + blk[3:], dt))
            continue
        out_specs.append(pl.BlockSpec((TOKEN_TILE, n_pad), lambda i: (i, 0)))
        out_shapes.append(jax.ShapeDtypeStruct((T, n_pad), dt))
    in_specs = [pl.BlockSpec((TOKEN_TILE, D_MODEL), lambda i: (i, 0)), _resident((1, D_MODEL))]
    in_specs += [_resident(w.shape) for w in ws]
    return pl.pallas_call(
        functools.partial(_norm_proj_kernel, n_out=len(ws)),
        grid=(T // TOKEN_TILE,),
        in_specs=in_specs,
        out_specs=out_specs,
        out_shape=out_shapes,
        compiler_params=_params("parallel"),
        name="norm_proj",
    )(x, g.reshape(1, D_MODEL), *ws)


def _attn_out_kernel(x_ref, mix_ref, xq_ref, mk_ref, mv_ref, w1_ref, w2_ref, g_ref, wg_ref, wu_ref, wd_ref,
                     *rest, final):
    xq = xq_ref[...]
    mk = mk_ref[...]
    mv = mv_ref[...]
    outs = []
    for h in range(X_HEADS):
        sl = slice(h * X_HEAD_DIM, (h + 1) * X_HEAD_DIM)
        s = _dot_nt(xq[:, sl], mk[:, sl])
        e = jnp.exp2(s - jnp.max(s, axis=-1, keepdims=True))
        o = _dot(e.astype(BF16), mv[:, sl]) * (1.0 / jnp.sum(e, axis=-1, keepdims=True))
        outs.append(o.astype(BF16))
    xo = jnp.concatenate(outs, axis=-1)
    x = x_ref[...] + _dot(mix_ref[...], w1_ref[...]) + _dot(xo, w2_ref[...])
    y = _ffn_half_step(x, g_ref, wg_ref, wu_ref, wd_ref)
    if final:
        fg_ref, o_ref = rest
        o_ref[...] = _rms(y, fg_ref[...], NORM_EPS)
    else:
        nu_ref, nd_ref, o_ref, nu_out_ref, nd_out_ref = rest
        o_ref[...] = y
        nu_out_ref[...] = nu_ref[...].astype(BF16)
        nd_out_ref[...] = nd_ref[...].astype(BF16)


def _attn_out_ffn(x, mix, xq, mem_k, mem_v, w_out, ffn_weights, final_g=None, next_cast=None):
    B, S, _ = x.shape
    M = mem_k.shape[1]
    w1 = w_out[:D_MODEL].astype(BF16)
    w2 = w_out[D_MODEL:].astype(BF16)
    w_args, w_specs = ffn_weights
    tok = lambda n: pl.BlockSpec((None, TOKEN_TILE, n), lambda b, i: (b, i, 0))
    mem = pl.BlockSpec((None, M, X_WIDTH), lambda b, i: (b, 0, 0))
    out_specs, out_shapes = [tok(D_MODEL)], [jax.ShapeDtypeStruct((B, S, D_MODEL), F32)]
    per_seq = S // TOKEN_TILE
    if final_g is not None:
        w_args = w_args + [final_g.reshape(1, D_MODEL)]
        w_specs = w_specs + [_resident((1, D_MODEL))]
    else:
        c_args, c_in, c_out, c_shapes = _next_weights_cast(*next_cast, step_of=lambda b, i: b * per_seq + i,
                                                           n_steps=B * per_seq)
        w_args, w_specs = w_args + c_args, w_specs + c_in
        out_specs, out_shapes = out_specs + c_out, out_shapes + c_shapes
    outs = pl.pallas_call(
        functools.partial(_attn_out_kernel, final=final_g is not None),
        grid=(B, per_seq),
        in_specs=[tok(D_MODEL), tok(D_MODEL), tok(X_WIDTH), mem, mem, _resident(w1.shape), _resident(w2.shape)]
        + w_specs,
        out_specs=out_specs,
        out_shape=out_shapes,
        compiler_params=_params("parallel", "parallel"),
        name="attn_out_ffn",
    )(x, mix, xq, mem_k, mem_v, w1, w2, *w_args)
    if final_g is not None:
        return outs[0], None, None
    return outs[0], outs[1], outs[2]


def _pool_kernel(x_ref, g_ref, wu_ref, wq_ref, wg_ref, sc_ref, o_ref, xq_ref, ext_ref, a_ref, b_ref, *, ts):
    j = pl.program_id(1)
    H = POOL_HALO

    @pl.when(j == 0)
    def _():
        ext_ref[0:2 * H, :] = jnp.zeros((2 * H, D_MODEL), F32)
        a_ref[0:H, :] = jnp.zeros((H, POOL_GROUP_DIM), F32)
        b_ref[0:H, :] = jnp.zeros((H, POOL_GROUP_DIM), F32)

    h = _rms(x_ref[...], g_ref[...], NORM_EPS).astype(BF16)
    xq_ref[...] = _dot(h, wq_ref[...]).astype(xq_ref.dtype)
    ext_ref[2 * H:2 * H + ts, :] = _dot(h, wu_ref[...])
    pos = j * ts + lax.broadcasted_iota(jnp.int32, (ts, 1), 0)
    n = ts + H
    for g, w in enumerate(POOL_WINDOWS):
        sl = slice(g * POOL_GROUP_DIM, (g + 1) * POOL_GROUP_DIM)
        src, cols, d, bufs = ext_ref, sl, 1, [a_ref, b_ref]
        while d < w:
            dst = bufs.pop(0)
            dst[H:H + n, :] = src[H:H + n, cols] + src[H - d:H - d + n, cols]
            bufs.append(dst)
            src, cols, d = dst, slice(None), 2 * d
        inv = 1.0 / jnp.minimum(pos + 1, w).astype(F32)
        pooled = (src[2 * H:2 * H + ts, cols] * inv - ext_ref[2 * H:2 * H + ts, sl]).astype(BF16)
        o_ref[:, sl] = (_dot(pooled, wg_ref[g]) * sc_ref[:, sl]).astype(o_ref.dtype)
    ext_ref[H:2 * H, :] = ext_ref[ts + H:ts + 2 * H, :]


def _pool_mix(x, g, w_in, w_group, scale):
    B, S, _ = x.shape
    ts = TOKEN_TILE
    tok = lambda n: pl.BlockSpec((None, ts, n), lambda b, j: (b, j, 0))
    return pl.pallas_call(
        functools.partial(_pool_kernel, ts=ts),
        grid=(B, S // ts),
        in_specs=[tok(D_MODEL), _resident((1, D_MODEL)), _resident((D_MODEL, D_MODEL)), _resident((D_MODEL, X_WIDTH)),
                  _resident(w_group.shape), _resident((1, D_MODEL))],
        out_specs=[tok(D_MODEL), tok(X_WIDTH)],
        out_shape=[jax.ShapeDtypeStruct((B, S, D_MODEL), BF16), jax.ShapeDtypeStruct((B, S, X_WIDTH), BF16)],
        scratch_shapes=[pltpu.VMEM((ts + 2 * POOL_HALO, D_MODEL), F32),
                        pltpu.VMEM((ts + 2 * POOL_HALO, POOL_GROUP_DIM), F32),
                        pltpu.VMEM((ts + 2 * POOL_HALO, POOL_GROUP_DIM), F32)],
        compiler_params=_params("parallel", "arbitrary"),
        name="pool_mix",
    )(x, g.reshape(1, D_MODEL), w_in[:, :D_MODEL].astype(BF16), w_in[:, D_MODEL:].astype(BF16),
      w_group.astype(BF16), scale.reshape(1, D_MODEL))


def _lane_features(lane, groups):
    out = jnp.zeros(lane.shape, F32)
    for g, v in enumerate(groups):
        for r in range(DIFF_SLOPE_TERMS):
            out = jnp.where(lane == DIFF_SLOPE_TERMS * g + r, v[r] if isinstance(v, list) else v, out)
    return out


def _diff_kernel(slopes_ref, lam_ref, q_ref, k_ref, vt_ref, ng_ref, o_ref,
                 ka_ref, qf_ref, acc_ref, m_ref, *, t, n_tiles, lam_init):
    hg = pl.program_id(1)
    qi = pl.program_id(2)
    hd = 2 * DIFF_HEAD_DIM
    lane = lax.broadcasted_iota(jnp.int32, (t, hd), 1)
    rowf = lax.broadcasted_iota(jnp.int32, (t, hd), 0).astype(F32)
    own = (lane < DIFF_HEAD_DIM, lane >= DIFF_HEAD_DIM)
    heads = range(DIFF_HEAD_BLOCK)
    c_terms = [[slopes_ref[hg * DIFF_HEAD_BLOCK + hb, r] for r in range(DIFF_SLOPE_TERMS)] for hb in heads]
    chains = [(hb, c) for hb in heads for c in range(2)]

    @pl.when(qi == 0)
    def _():
        key_feat = []
        for hb in heads:
            neg = [-cr for cr in c_terms[hb]]
            key_feat.append(_lane_features(lane, [rowf, 0.0, neg, [cr * t for cr in neg]]))
        key_tile_lanes = (lane >= DIFF_SLOPE_TERMS) & (lane < 2 * DIFF_SLOPE_TERMS)

        def build(j, carry):
            start = pl.multiple_of(j * t, t)
            for hb in heads:
                feat = jnp.where(key_tile_lanes, jnp.asarray(j, F32), key_feat[hb])
                ka_ref[hb, pl.ds(start, t), 0:hd] = k_ref[pl.ds(start, t), hb * hd:(hb + 1) * hd]
                ka_ref[hb, pl.ds(start, t), hd:2 * hd] = feat.astype(BF16)
            return carry

        lax.fori_loop(0, n_tiles, build, 0)
        for hb in heads:
            qf_ref[hb] = _lane_features(lane, [c_terms[hb], [cr * t for cr in c_terms[hb]], rowf, 0.0])

    qa = {}
    tile_lanes = (lane >= 3 * DIFF_SLOPE_TERMS) & (lane < 4 * DIFF_SLOPE_TERMS)
    for hb in heads:
        q = q_ref[:, hb * hd:(hb + 1) * hd]
        feat = jnp.where(tile_lanes, jnp.asarray(qi, F32), qf_ref[hb])
        for c in range(2):
            qa[hb, c] = jnp.concatenate([jnp.where(own[c], q, jnp.zeros_like(q)), feat.astype(BF16)], axis=1)
    acc_ref[...] = jnp.zeros_like(acc_ref)
    m_ref[...] = jnp.full(m_ref.shape, -jnp.inf, F32)

    def tiles(kjs, masked):
        starts = [pl.multiple_of(kj * t, t) for kj in kjs]
        scores = {}
        for hb, c in chains:
            parts = []
            for start in starts:
                s = _dot_nt(ka_ref[hb, pl.ds(start, t), :], qa[hb, c])
                if masked:
                    key = lax.broadcasted_iota(jnp.int32, (t, t), 0)
                    qry = lax.broadcasted_iota(jnp.int32, (t, t), 1)
                    s = jnp.where(key <= qry, s, -jnp.inf)
                parts.append(s)
            scores[hb, c] = parts
        probs = {}
        for hb, c in chains:
            parts = scores[hb, c]
            m_prev = m_ref[hb, c]
            m_new = m_prev
            for s in parts:
                m_new = jnp.maximum(m_new, jnp.max(s, axis=0, keepdims=True))
            alpha = jnp.exp2(m_prev - m_new)
            m_ref[hb, c] = m_new
            probs[hb, c] = (alpha, jnp.concatenate([jnp.exp2(s - m_new).astype(BF16) for s in parts], axis=0))
        for hb, c in chains:
            alpha, p = probs[hb, c]
            vt = jnp.concatenate([vt_ref[hb, kj] for kj in kjs], axis=1)
            acc_ref[hb, c] = alpha * acc_ref[hb, c] + _dot(vt, p)

    def body(pj, carry):
        tiles([2 * pj, 2 * pj + 1], masked=False)
        return carry

    lax.fori_loop(0, qi // 2, body, 0)

    @pl.when(qi % 2 == 1)
    def _():
        tiles([qi - 1], masked=False)

    tiles([qi], masked=True)

    lp = lam_ref[...]
    lam = (jnp.exp(jnp.sum(lp[0:1] * lp[1:2], axis=-1, keepdims=True))
           - jnp.exp(jnp.sum(lp[2:3] * lp[3:4], axis=-1, keepdims=True)) + lam_init)
    for hb in range(DIFF_HEAD_BLOCK):
        a0, a1 = acc_ref[hb, 0], acc_ref[hb, 1]
        o_t = (a0[:hd] * (1.0 / a0[hd:hd + 1]) - lam * (a1[:hd] * (1.0 / a1[hd:hd + 1])))
        o_ref[:, hb * hd:(hb + 1) * hd] = (_rms(o_t.T, ng_ref[...], SUBLN_EPS) * (1.0 - lam_init)).astype(o_ref.dtype)


def _diff_mix(q, k, v_t, lam_p, norm_g, layer_idx):
    B, S, _ = q.shape
    t = ATTN_TILE
    n_tiles = S // t
    hd = 2 * DIFF_HEAD_DIM
    hb = DIFF_HEAD_BLOCK
    lam_init = 0.8 - 0.6 * math.exp(-0.3 * layer_idx)
    slopes = jnp.asarray([2.0 ** (-8.0 * (h + 1) / DIFF_HEADS) for h in range(DIFF_HEADS)], dtype=F32)
    rest = slopes * math.log2(math.e)
    terms = []
    for _ in range(DIFF_SLOPE_TERMS):
        terms.append(rest.astype(BF16).astype(F32))
        rest = rest - terms[-1]
    slopes = jnp.stack(terms, axis=1)
    qspec = pl.BlockSpec((None, t, hb * hd), lambda b, h, i: (b, i, h))
    return pl.pallas_call(
        functools.partial(_diff_kernel, t=t, n_tiles=n_tiles, lam_init=lam_init),
        grid=(B, DIFF_HEADS // hb, n_tiles),
        in_specs=[pl.BlockSpec(memory_space=pltpu.SMEM), _resident(lam_p.shape), qspec,
                  pl.BlockSpec((None, S, hb * hd), lambda b, h, i: (b, 0, h), pipeline_mode=pl.Buffered(1)),
                  pl.BlockSpec((None, hb, n_tiles, hd + DIFF_AUX_ROWS, t), lambda b, h, i: (b, h, 0, 0, 0),
                               pipeline_mode=pl.Buffered(1)),
                  _resident((1, hd))],
        out_specs=qspec,
        out_shape=jax.ShapeDtypeStruct((B, S, D_MODEL), BF16),
        scratch_shapes=[pltpu.VMEM((hb, S, 2 * hd), BF16), pltpu.VMEM((hb, t, hd), F32),
                        pltpu.VMEM((hb, 2, hd + DIFF_AUX_ROWS, t), F32),
                        pltpu.VMEM((hb, 2, 1, t), F32)],
        compiler_params=_params("parallel", "parallel", "arbitrary"),
        name="diff_attn",
    )(slopes, lam_p, q, k, v_t, norm_g.reshape(1, hd))


def _mlstm_kernel(qk_ref, v_ref, og_ref, gc_ref, gr_ref, cw_ref, gbc_ref, gbr_ref, ng_ref, o_ref,
                  ext_ref, c_ref, n_ref, m_ref, *, L):
    @pl.when(pl.program_id(1) == 0)
    def _():
        ext_ref[0:CONV_HALO, :] = jnp.zeros((CONV_HALO, 2 * D_MODEL), F32)
        c_ref[...] = jnp.zeros_like(c_ref)
        n_ref[...] = jnp.zeros_like(n_ref)
        m_ref[...] = jnp.zeros_like(m_ref)

    for ci in range(MLSTM_CHUNKS_PER_STEP):
        _mlstm_chunk(qk_ref, v_ref, og_ref, gc_ref, gr_ref, cw_ref, gbc_ref, gbr_ref, ng_ref, o_ref,
                     ext_ref, c_ref, n_ref, m_ref, slice(ci * L, (ci + 1) * L), L)


def _mlstm_chunk(qk_ref, v_ref, og_ref, gc_ref, gr_ref, cw_ref, gbc_ref, gbr_ref, ng_ref, o_ref,
                 ext_ref, c_ref, n_ref, m_ref, rows, L):
    ext_ref[CONV_HALO:CONV_HALO + L, :] = qk_ref[rows, :]
    conv = cw_ref[MLSTM_CONV - 1:MLSTM_CONV, :] * qk_ref[rows, :]
    for j in range(1, MLSTM_CONV):
        conv = conv + (cw_ref[MLSTM_CONV - 1 - j:MLSTM_CONV - j, :]
                       * ext_ref[CONV_HALO - j:CONV_HALO - j + L, :])
    ext_ref[0:CONV_HALO, :] = ext_ref[L:L + CONV_HALO, :]
    qk = conv * _sigmoid(conv)

    gc = gc_ref[rows, :] + gbc_ref[...]
    gr = gr_ref[:, rows] + gbr_ref[...]
    b_c = _cumsum_rows(_log_sigmoid(gc))
    b_r = _cumsum_lanes(_log_sigmoid(gr))
    row = lax.broadcasted_iota(jnp.int32, (L, L), 0)
    col = lax.broadcasted_iota(jnp.int32, (L, L), 1)
    causal = col <= row

    for hd in range(MLSTM_HEADS):
        sl = slice(hd * MLSTM_HEAD_DIM, (hd + 1) * MLSTM_HEAD_DIM)
        q32 = qk[:, sl] * (MLSTM_HEAD_DIM ** -0.5)
        k32 = qk[:, D_MODEL + hd * MLSTM_HEAD_DIM:D_MODEL + (hd + 1) * MLSTM_HEAD_DIM]
        vh = v_ref[rows, sl]
        fi = MLSTM_HEADS + hd
        bcol, icol = b_c[:, fi:fi + 1], gc[:, hd:hd + 1]
        brow, irow = b_r[fi:fi + 1, :], gr[hd:hd + 1, :]
        b_last = brow[:, L - 1:L]
        m = m_ref[hd:hd + 1, 0:1]
        cmat = c_ref[hd]
        nrow = n_ref[hd]

        dmat = jnp.where(causal, bcol - brow + irow, -jnp.inf)
        inter = bcol + m
        m_t = jnp.maximum(inter, jnp.max(dmat, axis=-1, keepdims=True))
        dec = jnp.exp(inter - m_t)
        qb = q32.astype(BF16)
        k_t = k32.T
        sqk = _dot(qb, k_t.astype(BF16)) * jnp.exp(dmat - m_t)
        num = dec * _dot(qb, cmat.astype(BF16)) + _dot(sqk.astype(BF16), vh)
        den = dec * jnp.sum(q32 * nrow, axis=-1, keepdims=True) + jnp.sum(sqk, axis=-1, keepdims=True)
        hc = num * (1.0 / jnp.maximum(jnp.abs(den), jnp.exp(-m_t)))

        gs_r = b_last - brow + irow
        gs_c = b_last - bcol + icol
        m_new = jnp.maximum(b_last + m, jnp.max(gs_r, axis=-1, keepdims=True))
        carry_dec = jnp.exp(b_last + m - m_new)
        c_ref[hd] = carry_dec * cmat + _dot((k_t * jnp.exp(gs_r - m_new)).astype(BF16), vh)
        n_ref[hd] = carry_dec * nrow + jnp.sum(k32 * jnp.exp(gs_c - m_new), axis=0, keepdims=True)
        m_ref[hd:hd + 1, :] = jnp.broadcast_to(m_new, (1, LANES))

        o_ref[rows, sl] = (_rms(hc, ng_ref[:, sl], NORM_EPS) * _sigmoid(og_ref[rows, sl])).astype(o_ref.dtype)


def _mlstm_mix(qk, v, og, gates, conv_w, gate_b, norm_g):
    B, S, _ = qk.shape
    L = MLSTM_CHUNK
    H = MLSTM_HEADS
    gates_r = jnp.transpose(gates[..., :2 * H], (0, 2, 1))
    gb = gate_b.reshape(2 * H)
    gb_c = jnp.pad(gb, (0, LANES - 2 * H)).reshape(1, LANES)
    gb_r = gb.reshape(2 * H, 1)
    step = L * MLSTM_CHUNKS_PER_STEP
    tok = lambda n: pl.BlockSpec((None, step, n), lambda b, c: (b, c, 0))
    return pl.pallas_call(
        functools.partial(_mlstm_kernel, L=L),
        grid=(B, S // step),
        in_specs=[tok(2 * D_MODEL), tok(D_MODEL), tok(D_MODEL), tok(LANES),
                  pl.BlockSpec((None, 2 * H, step), lambda b, c: (b, 0, c)),
                  _resident(conv_w.shape), _resident((1, LANES)), _resident((2 * H, 1)), _resident((1, D_MODEL))],
        out_specs=tok(D_MODEL),
        out_shape=jax.ShapeDtypeStruct((B, S, D_MODEL), BF16),
        scratch_shapes=[pltpu.VMEM((L + CONV_HALO, 2 * D_MODEL), F32),
                        pltpu.VMEM((H, MLSTM_HEAD_DIM, MLSTM_HEAD_DIM), F32),
                        pltpu.VMEM((H, 1, MLSTM_HEAD_DIM), F32),
                        pltpu.VMEM((8, LANES), F32)],
        compiler_params=_params("parallel", "arbitrary"),
        name="mlstm",
    )(qk, v, og, gates, gates_r, conv_w, gb_c, gb_r, norm_g.reshape(1, D_MODEL))


def _gla_kernel(q_ref, k_ref, v_ref, g_ref, z_ref, w2_ref, gb_ref, ng_ref, o_ref, st_ref, *, L):
    @pl.when(pl.program_id(1) == 0)
    def _():
        st_ref[...] = jnp.zeros_like(st_ref)

    for ci in range(GLA_CHUNKS_PER_STEP):
        _gla_chunk(q_ref, k_ref, v_ref, g_ref, z_ref, w2_ref, gb_ref, ng_ref, o_ref, st_ref,
                   slice(ci * L, (ci + 1) * L), L)


def _gla_chunk(q_ref, k_ref, v_ref, g_ref, z_ref, w2_ref, gb_ref, ng_ref, o_ref, st_ref, rows, L):
    log_a = _log_sigmoid(_dot(z_ref[rows, :], w2_ref[...]) + gb_ref[...]) * (1.0 / GLA_TAU)
    b_all = _cumsum_rows(log_a) * LOG2E
    sub = GLA_SUBCHUNK
    row = lax.broadcasted_iota(jnp.int32, (sub, L), 0)
    col = lax.broadcasted_iota(jnp.int32, (sub, L), 1)
    ones = jnp.ones((GLA_KEY_DIM, LANES), BF16)

    for hd in range(GLA_HEADS):
        ks = slice(hd * GLA_KEY_DIM, (hd + 1) * GLA_KEY_DIM)
        vs = slice(hd * GLA_VALUE_DIM, (hd + 1) * GLA_VALUE_DIM)
        b = b_all[:, ks]
        vh = v_ref[rows, vs]
        st = st_ref[hd]
        q = q_ref[rows, ks] * (GLA_KEY_DIM ** -0.5)
        k = k_ref[rows, ks]
        b_t = b.T
        k_t = k.T
        b_last = b_t[:, L - 1:L]
        blocks = []
        for i in range(L // sub):
            r0 = i * sub
            q_i, k_i, b_i = q[r0:r0 + sub], k[r0:r0 + sub], b[r0:r0 + sub]
            terms, offs = [], []
            for s in range(sub):
                t0 = s // SUBLANES * SUBLANES
                offs.append(sum(x.shape[0] for x in terms))
                terms.append(q_i[t0:] * k_i[s:s + 1] * jnp.exp2(b_i[t0:] - b_i[s:s + 1]))
            sums = _dot(jnp.concatenate(terms, axis=0).astype(BF16), ones)
            groups = []
            for t0 in range(0, sub, SUBLANES):
                a_g = jnp.zeros((SUBLANES, L), F32)
                for s in range(min(sub, t0 + SUBLANES)):
                    lo = offs[s] + t0 - s // SUBLANES * SUBLANES
                    a_g = jnp.where(col[:SUBLANES] == r0 + s, sums[lo:lo + SUBLANES], a_g)
                groups.append(a_g)
            a_i = jnp.where(col <= r0 + row, jnp.concatenate(groups, axis=0), 0.0)
            if i > 0:
                ref_row, ref_col = b[r0 - 1:r0], b_t[:, r0 - 1:r0]
                q_s = (q_i * jnp.exp2(b_i - ref_row)).astype(BF16)
                k_s = (k_t * jnp.exp2(ref_col - b_t)).astype(BF16)
                a_i = jnp.where(col < r0, _dot(q_s, k_s), a_i)
            blocks.append(a_i)
        amat = jnp.concatenate(blocks, axis=0)
        o = _dot((q * jnp.exp2(b)).astype(BF16), st.astype(BF16)) + _dot(amat.astype(BF16), vh)
        st_ref[hd] = jnp.exp2(b_last) * st + _dot((k_t * jnp.exp2(b_last - b_t)).astype(BF16), vh)
        gate = g_ref[rows, vs]
        o_ref[rows, vs] = (_rms(o, ng_ref[:, vs], NORM_EPS) * (gate * _sigmoid(gate))).astype(o_ref.dtype)


def _gla_mix(q, k, v, g, z, gate_w2, gate_b, norm_g):
    B, S, _ = q.shape
    L = GLA_CHUNK
    kw =GLA_HEADS * GLA_KEY_DIM
    w2 = jnp.pad(gate_w2, ((0, LANES - GLA_GATE_RANK), (0, 0))).astype(BF16)
    step = L * GLA_CHUNKS_PER_STEP
    tok = lambda n: pl.BlockSpec((None, step, n), lambda b, c: (b, c, 0))
    return pl.pallas_call(
        functools.partial(_gla_kernel, L=L),
        grid=(B, S // step),
        in_specs=[tok(kw), tok(kw), tok(D_MODEL), tok(D_MODEL), tok(LANES),
                  _resident(w2.shape), _resident((1, kw)), _resident((1, D_MODEL))],
        out_specs=tok(D_MODEL),
        out_shape=jax.ShapeDtypeStruct((B, S, D_MODEL), BF16),
        scratch_shapes=[pltpu.VMEM((GLA_HEADS, GLA_KEY_DIM, GLA_VALUE_DIM), F32)],
        compiler_params=_params("parallel", "arbitrary"),
        name="gla",
    )(q, k, v, g, z, w2, gate_b.reshape(1, kw), norm_g.reshape(1, D_MODEL))


def kernel(x, mem, norm_g, ffn_w_up, ffn_w_down, mem_norm_g, mem_w_kv, pool_w_in, pool_w_group, pool_scale, pool_w_out, diff_w_in, diff_lambda, diff_norm_g, diff_w_out, mlstm_w_in, mlstm_conv_w, mlstm_gate_b, mlstm_norm_g, mlstm_w_out, gla_w_in, gla_gate_w2, gla_gate_b, gla_norm_g, gla_w_out, final_norm_g):
    B, S, D = x.shape
    M = mem.shape[1]
    T = B * S
    depth = norm_g.shape[0]
    n_mixers = 4
    mem2 = mem.reshape(B * M, D)
    x2 = x.reshape(T, D)
    norm_g4 = norm_g.reshape(depth, 3, 1, D)
    w_up, w_down = ffn_w_up[0, 0].astype(BF16), ffn_w_down[0, 0].astype(BF16)

    def seq(a):
        return a.reshape(B, S, a.shape[-1])

    for i in range(depth):
        kind, j = i % n_mixers, i // n_mixers
        k_scale = X_HEAD_DIM ** -0.5 * LOG2E
        mem_k, mem_v = _norm_proj(mem2, mem_norm_g, [(mem_w_kv[i][:, :X_WIDTH] * k_scale, BF16),
                                                     (mem_w_kv[i][:, X_WIDTH:], BF16)])
        mem_k = mem_k.reshape(B, M, X_WIDTH)
        mem_v = mem_v.reshape(B, M, X_WIDTH)
        x2, w_up, w_down = _ffn(x2, _ffn_weights(norm_g4, w_up, w_down, i, 0), (ffn_w_up, ffn_w_down, i, 1))
        if kind == 0:
            mix, xq = _pool_mix(seq(x2), norm_g[i, 1], pool_w_in[j], pool_w_group[j], pool_scale[j])
            w_out = pool_w_out[j]
        elif kind == 1:
            w = diff_w_in[j]
            q_scale = DIFF_HEAD_DIM ** -0.5 * math.log2(math.e)
            q, k, v_t, xq = _norm_proj(x2, norm_g[i, 1], [
                (w[:, :D] * q_scale, BF16), (w[:, D:2 * D], BF16), (w[:, 2 * D:3 * D], BF16, "values_t"),
                (w[:, 3 * D:], BF16)], seq_len=S)
            mix = _diff_mix(seq(q), seq(k), v_t, diff_lambda[j], diff_norm_g[j], i)
            w_out = diff_w_out[j]
        elif kind == 2:
            w = mlstm_w_in[j]
            ng = 2 * MLSTM_HEADS
            qk, v, og, gates, xq = _norm_proj(x2, norm_g[i, 1], [
                (w[:, :2 * D], F32), (w[:, 2 * D:3 * D], BF16), (w[:, 3 * D:4 * D], F32),
                (w[:, 4 * D:4 * D + ng], F32), (w[:, 4 * D + ng:], BF16)])
            mix = _mlstm_mix(seq(qk), seq(v), seq(og), seq(gates), mlstm_conv_w[j], mlstm_gate_b[j], mlstm_norm_g[j])
            w_out = mlstm_w_out[j]
        else:
            w = gla_w_in[j]
            kw = GLA_HEADS * GLA_KEY_DIM
            o0 = 2 * kw + 2 * D
            q, k, v, g, z, xq = _norm_proj(x2, norm_g[i, 1], [
                (w[:, :kw], F32), (w[:, kw:2 * kw], F32), (w[:, 2 * kw:2 * kw + D], BF16),
                (w[:, 2 * kw + D:o0], F32), (w[:, o0:o0 + GLA_GATE_RANK], BF16), (w[:, o0 + GLA_GATE_RANK:], BF16)])
            mix = _gla_mix(seq(q), seq(k), seq(v), seq(g), seq(z), gla_gate_w2[j], gla_gate_b[j], gla_norm_g[j])
            w_out = gla_w_out[j]
        last = i == depth - 1
        x3, w_up, w_down = _attn_out_ffn(seq(x2), mix, seq(xq), mem_k, mem_v, w_out,
                                         _ffn_weights(norm_g4, w_up, w_down, i, 2),
                                         final_g=final_norm_g if last else None,
                                         next_cast=None if last else (ffn_w_up, ffn_w_down, i + 1, 0))
        x2 = x3.reshape(T, D)
    return x2.reshape(B, S, D)
```

```python
import functools
import math

import jax
import jax.numpy as jnp
from jax import lax
from jax.experimental import pallas as pl
from jax.experimental.pallas import tpu as pltpu

F32 = jnp.float32
BF16 = jnp.bfloat16

D_MODEL = 1024
D_FF = 2816
X_HEADS = 4
X_HEAD_DIM = 128
X_WIDTH = X_HEADS * X_HEAD_DIM
POOL_WINDOWS = (2, 4, 8, 16)
POOL_GROUP_DIM = D_MODEL // len(POOL_WINDOWS)
POOL_HALO = 16
DIFF_HEAD_DIM = 64
DIFF_HEADS = 8
DIFF_HEAD_BLOCK = 8
DIFF_SLOPE_TERMS = 3
DIFF_AUX_ROWS = 16
MLSTM_HEADS = 4
MLSTM_HEAD_DIM = 256
MLSTM_CONV = 4
CONV_HALO = 8
GLA_HEADS = 4
GLA_KEY_DIM = 128
GLA_VALUE_DIM = 256
GLA_GATE_RANK = 16
GLA_TAU = 16.0
GLA_SUBCHUNK = 16
LOG2E = math.log2(math.e)
NORM_EPS = 1e-6
SUBLN_EPS = 1e-5
LANES = 128
SUBLANES = 8

VMEM_LIMIT = 56 * 1024 * 1024
TOKEN_TILE = 512
FF_CHUNKS = (0, 1536, D_FF)
FFN_TOKEN_TILE = 1024
FFN_WIDE_CHUNKS = (0, 768, 1536, 2304, D_FF)
ATTN_TILE = 256
MLSTM_CHUNK = 256
MLSTM_CHUNKS_PER_STEP = 2
GLA_CHUNK = 128
GLA_CHUNKS_PER_STEP = 2


def _params(*sem):
    return pltpu.CompilerParams(dimension_semantics=sem, vmem_limit_bytes=VMEM_LIMIT)


def _resident(shape):
    nd = len(shape)
    return pl.BlockSpec(shape, lambda *_: (0,) * nd, pipeline_mode=pl.Buffered(1))


def _rms(x, g, eps):
    return x * lax.rsqrt(jnp.mean(x * x, axis=-1, keepdims=True) + eps) * g


def _sigmoid(x):
    return 1.0 / (1.0 + jnp.exp(-x))


def _log_sigmoid(x):
    return jnp.minimum(x, 0.0) - jnp.log(1.0 + jnp.exp(-jnp.abs(x)))


def _dot(a, b):
    return jnp.dot(a, b, preferred_element_type=F32)


def _dot_nt(a, b):
    return lax.dot_general(a, b, (((1,), (1,)), ((), ())), preferred_element_type=F32)


def _split3(x):
    hi = x.astype(BF16)
    r1 = x - hi.astype(F32)
    mid = r1.astype(BF16)
    lo = (r1 - mid.astype(F32)).astype(BF16)
    return hi, mid, lo


def _tri(n, upper):
    r = lax.broadcasted_iota(jnp.int32, (n, n), 0)
    c = lax.broadcasted_iota(jnp.int32, (n, n), 1)
    keep = (r <= c) if upper else (c <= r)
    return jnp.where(keep, 1.0, 0.0).astype(BF16)


def _cumsum_rows(x):
    tri = _tri(x.shape[0], upper=False)
    hi, mid, lo = _split3(x)
    return _dot(tri, hi) + _dot(tri, mid) + _dot(tri, lo)


def _cumsum_lanes(x):
    tri = _tri(x.shape[1], upper=True)
    hi, mid, lo = _split3(x)
    return _dot(hi, tri) + _dot(mid, tri) + _dot(lo, tri)


def _ffn_half_step(x, g_ref, wg_ref, wu_ref, wd_ref, chunks=FF_CHUNKS):
    h = _rms(x, g_ref[...], NORM_EPS).astype(BF16)
    acc = jnp.zeros_like(x)
    for lo, hi in zip(chunks[:-1], chunks[1:]):
        sl = slice(lo, hi)
        gate = _dot(h, wg_ref[:, sl])
        up = _dot(h, wu_ref[:, sl])
        act = (gate * _sigmoid(gate) * up).astype(BF16)
        acc = acc + _dot(act, wd_ref[sl, :])
    return x + 0.5 * acc


def _stacked(shape, lead, tail=None):
    tail = tail or (0,) * len(shape)
    return pl.BlockSpec((None,) * len(lead) + shape, lambda *_: lead + tail, pipeline_mode=pl.Buffered(1))


def _ffn_weights(norm_g, w_up, w_down, layer, norm_slot):
    args = [norm_g, w_up, w_up, w_down]
    specs = [_stacked((1, D_MODEL), (layer, norm_slot)), _stacked((D_MODEL, D_FF), (), (0, 0)),
             _stacked((D_MODEL, D_FF), (), (0, 1)), _stacked((D_FF, D_MODEL), ())]
    return args, specs


def _next_weights_cast(w_up_all, w_down_all, layer, slot, step_of, n_steps):
    args, in_specs, out_specs, out_shapes = [w_up_all, w_down_all], [], [], []
    for w in args:
        n_rows, n_cols = w.shape[2:]
        n_slabs = max(s for s in range(1, n_steps + 1) if n_rows % (16 * s) == 0)
        slab = lambda *idx, n_slabs=n_slabs: jnp.minimum(step_of(*idx), n_slabs - 1)
        in_specs.append(pl.BlockSpec((None, None, n_rows // n_slabs, n_cols),
                                     lambda *idx, slab=slab: (layer, slot, slab(*idx), 0)))
        out_specs.append(pl.BlockSpec((n_rows // n_slabs, n_cols), lambda *idx, slab=slab: (slab(*idx), 0)))
        out_shapes.append(jax.ShapeDtypeStruct((n_rows, n_cols), BF16))
    return args, in_specs, out_specs, out_shapes


def _ffn_kernel(x_ref, g_ref, wg_ref, wu_ref, wd_ref, nu_ref, nd_ref, o_ref, nu_out_ref, nd_out_ref):
    o_ref[...] = _ffn_half_step(x_ref[...], g_ref, wg_ref, wu_ref, wd_ref, FFN_WIDE_CHUNKS)
    nu_out_ref[...] = nu_ref[...].astype(BF16)
    nd_out_ref[...] = nd_ref[...].astype(BF16)


def _ffn(x, ffn_weights, next_cast):
    T = x.shape[0]
    w_args, w_specs = ffn_weights
    n_steps = T // FFN_TOKEN_TILE
    c_args, c_in, c_out, c_shapes = _next_weights_cast(*next_cast, step_of=lambda i: i, n_steps=n_steps)
    tok = pl.BlockSpec((FFN_TOKEN_TILE, D_MODEL), lambda i: (i, 0))
    y, w_up, w_down = pl.pallas_call(
        _ffn_kernel,
        grid=(n_steps,),
        in_specs=[tok] + w_specs + c_in,
        out_specs=[tok] + c_out,
        out_shape=[jax.ShapeDtypeStruct((T, D_MODEL), F32)] + c_shapes,
        compiler_params=_params("parallel"),
        name="ffn",
    )(x, *w_args, *c_args)
    return y, w_up, w_down


def _norm_proj_kernel(x_ref, g_ref, *refs, n_out):
    h = _rms(x_ref[...], g_ref[...], NORM_EPS).astype(BF16)
    for w_ref, o_ref in zip(refs[:n_out], refs[n_out:]):
        y = _dot(h, w_ref[...])
        if len(o_ref.shape) == 2:
            o_ref[...] = y.astype(o_ref.dtype)
            continue
        n_heads, n_sub, rows, t = o_ref.shape
        hd = rows - DIFF_AUX_ROWS
        aux = jnp.where(lax.broadcasted_iota(jnp.int32, (DIFF_AUX_ROWS, t), 0) == 0, 1.0, 0.0).astype(o_ref.dtype)
        for head in range(n_heads):
            for s in range(n_sub):
                o_ref[head, s, 0:hd, :] = y[s * t:(s + 1) * t, head * hd:(head + 1) * hd].T.astype(o_ref.dtype)
                o_ref[head, s, hd:rows, :] = aux


def _norm_proj(x, g, pieces, seq_len=None):
    T = x.shape[0]
    ws, out_specs, out_shapes = [], [], []
    for w, dt, *layout in pieces:
        n = w.shape[1]
        n_pad = -(-n // LANES) * LANES
        if n_pad != n:
            w = jnp.pad(w, ((0, 0), (0, n_pad - n)))
        ws.append(w.astype(BF16))
        if layout:
            hd, t, per_seq = 2 * DIFF_HEAD_DIM, ATTN_TILE, seq_len // TOKEN_TILE
            blk = (None, n // hd, TOKEN_TILE // t, hd + DIFF_AUX_ROWS, t)
            out_specs.append(pl.BlockSpec(blk, lambda i: (i // per_seq, 0, i % per_seq, 0, 0)))
            out_shapes.append(jax.ShapeDtypeStruct((T // seq_len, n // hd, seq_len // t) + blk[3:], dt))
            continue
        out_specs.append(pl.BlockSpec((TOKEN_TILE, n_pad), lambda i: (i, 0)))
        out_shapes.append(jax.ShapeDtypeStruct((T, n_pad), dt))
    in_specs = [pl.BlockSpec((TOKEN_TILE, D_MODEL), lambda i: (i, 0)), _resident((1, D_MODEL))]
    in_specs += [_resident(w.shape) for w in ws]
    return pl.pallas_call(
        functools.partial(_norm_proj_kernel, n_out=len(ws)),
        grid=(T // TOKEN_TILE,),
        in_specs=in_specs,
        out_specs=out_specs,
        out_shape=out_shapes,
        compiler_params=_params("parallel"),
        name="norm_proj",
    )(x, g.reshape(1, D_MODEL), *ws)


def _attn_out_kernel(x_ref, mix_ref, xq_ref, mk_ref, mv_ref, w1_ref, w2_ref, g_ref, wg_ref, wu_ref, wd_ref,
                     *rest, final):
    xq = xq_ref[...]
    mk = mk_ref[...]
    mv = mv_ref[...]
    sls = [slice(h * X_HEAD_DIM, (h + 1) * X_HEAD_DIM) for h in range(X_HEADS)]
    scores = [_dot_nt(xq[:, sl], mk[:, sl]) for sl in sls]
    probs = [jnp.exp2(s - jnp.max(s, axis=-1, keepdims=True)) for s in scores]
    outs = [_dot(e.astype(BF16), mv[:, sl]) * (1.0 / jnp.sum(e, axis=-1, keepdims=True))
            for e, sl in zip(probs, sls)]
    xo = jnp.concatenate([o.astype(BF16) for o in outs], axis=-1)
    x = x_ref[...] + _dot(mix_ref[...], w1_ref[...]) + _dot(xo, w2_ref[...])
    y = _ffn_half_step(x, g_ref, wg_ref, wu_ref, wd_ref)
    if final:
        fg_ref, o_ref = rest
        o_ref[...] = _rms(y, fg_ref[...], NORM_EPS)
    else:
        nu_ref, nd_ref, o_ref, nu_out_ref, nd_out_ref = rest
        o_ref[...] = y
        nu_out_ref[...] = nu_ref[...].astype(BF16)
        nd_out_ref[...] = nd_ref[...].astype(BF16)


def _attn_out_ffn(x, mix, xq, mem_k, mem_v, w_out, ffn_weights, final_g=None, next_cast=None):
    B, S, _ = x.shape
    M = mem_k.shape[1]
    w1 = w_out[:D_MODEL].astype(BF16)
    w2 = w_out[D_MODEL:].astype(BF16)
    w_args, w_specs = ffn_weights
    tok = lambda n: pl.BlockSpec((None, TOKEN_TILE, n), lambda b, i: (b, i, 0))
    mem = pl.BlockSpec((None, M, X_WIDTH), lambda b, i: (b, 0, 0))
    out_specs, out_shapes = [tok(D_MODEL)], [jax.ShapeDtypeStruct((B, S, D_MODEL), F32)]
    per_seq = S // TOKEN_TILE
    if final_g is not None:
        w_args = w_args + [final_g.reshape(1, D_MODEL)]
        w_specs = w_specs + [_resident((1, D_MODEL))]
    else:
        c_args, c_in, c_out, c_shapes = _next_weights_cast(*next_cast, step_of=lambda b, i: b * per_seq + i,
                                                           n_steps=B * per_seq)
        w_args, w_specs = w_args + c_args, w_specs + c_in
        out_specs, out_shapes = out_specs + c_out, out_shapes + c_shapes
    outs = pl.pallas_call(
        functools.partial(_attn_out_kernel, final=final_g is not None),
        grid=(B, per_seq),
        in_specs=[tok(D_MODEL), tok(D_MODEL), tok(X_WIDTH), mem, mem, _resident(w1.shape), _resident(w2.shape)]
        + w_specs,
        out_specs=out_specs,
        out_shape=out_shapes,
        compiler_params=_params("parallel", "parallel"),
        name="attn_out_ffn",
    )(x, mix, xq, mem_k, mem_v, w1, w2, *w_args)
    if final_g is not None:
        return outs[0], None, None
    return outs[0], outs[1], outs[2]


def _pool_kernel(x_ref, g_ref, wu_ref, wq_ref, wg_ref, sc_ref, o_ref, xq_ref, ext_ref, a_ref, b_ref, *, ts):
    j = pl.program_id(1)
    H = POOL_HALO

    @pl.when(j == 0)
    def _():
        ext_ref[0:2 * H, :] = jnp.zeros((2 * H, D_MODEL), F32)
        a_ref[0:H, :] = jnp.zeros((H, POOL_GROUP_DIM), F32)
        b_ref[0:H, :] = jnp.zeros((H, POOL_GROUP_DIM), F32)

    h = _rms(x_ref[...], g_ref[...], NORM_EPS).astype(BF16)
    xq_ref[...] = _dot(h, wq_ref[...]).astype(xq_ref.dtype)
    ext_ref[2 * H:2 * H + ts, :] = _dot(h, wu_ref[...])
    pos = j * ts + lax.broadcasted_iota(jnp.int32, (ts, 1), 0)
    n = ts + H
    for g, w in enumerate(POOL_WINDOWS):
        sl = slice(g * POOL_GROUP_DIM, (g + 1) * POOL_GROUP_DIM)
        src, cols, d, bufs = ext_ref, sl, 1, [a_ref, b_ref]
        while d < w:
            dst = bufs.pop(0)
            dst[H:H + n, :] = src[H:H + n, cols] + src[H - d:H - d + n, cols]
            bufs.append(dst)
            src, cols, d = dst, slice(None), 2 * d
        inv = 1.0 / jnp.minimum(pos + 1, w).astype(F32)
        pooled = (src[2 * H:2 * H + ts, cols] * inv - ext_ref[2 * H:2 * H + ts, sl]).astype(BF16)
        o_ref[:, sl] = (_dot(pooled, wg_ref[g]) * sc_ref[:, sl]).astype(o_ref.dtype)
    ext_ref[H:2 * H, :] = ext_ref[ts + H:ts + 2 * H, :]


def _pool_mix(x, g, w_in, w_group, scale):
    B, S, _ = x.shape
    ts = TOKEN_TILE
    tok = lambda n: pl.BlockSpec((None, ts, n), lambda b, j: (b, j, 0))
    return pl.pallas_call(
        functools.partial(_pool_kernel, ts=ts),
        grid=(B, S // ts),
        in_specs=[tok(D_MODEL), _resident((1, D_MODEL)), _resident((D_MODEL, D_MODEL)), _resident((D_MODEL, X_WIDTH)),
                  _resident(w_group.shape), _resident((1, D_MODEL))],
        out_specs=[tok(D_MODEL), tok(X_WIDTH)],
        out_shape=[jax.ShapeDtypeStruct((B, S, D_MODEL), BF16), jax.ShapeDtypeStruct((B, S, X_WIDTH), BF16)],
        scratch_shapes=[pltpu.VMEM((ts + 2 * POOL_HALO, D_MODEL), F32),
                        pltpu.VMEM((ts + 2 * POOL_HALO, POOL_GROUP_DIM), F32),
                        pltpu.VMEM((ts + 2 * POOL_HALO, POOL_GROUP_DIM), F32)],
        compiler_params=_params("parallel", "arbitrary"),
        name="pool_mix",
    )(x, g.reshape(1, D_MODEL), w_in[:, :D_MODEL].astype(BF16), w_in[:, D_MODEL:].astype(BF16),
      w_group.astype(BF16), scale.reshape(1, D_MODEL))


def _lane_features(lane, groups):
    out = jnp.zeros(lane.shape, F32)
    for g, v in enumerate(groups):
        for r in range(DIFF_SLOPE_TERMS):
            out = jnp.where(lane == DIFF_SLOPE_TERMS * g + r, v[r] if isinstance(v, list) else v, out)
    return out


def _diff_kernel(slopes_ref, lam_ref, q_ref, k_ref, vt_ref, ng_ref, o_ref,
                 ka_ref, qf_ref, acc_ref, m_ref, *, t, n_tiles, lam_init):
    hg = pl.program_id(1)
    qi = pl.program_id(2)
    hd = 2 * DIFF_HEAD_DIM
    lane = lax.broadcasted_iota(jnp.int32, (t, hd), 1)
    rowf = lax.broadcasted_iota(jnp.int32, (t, hd), 0).astype(F32)
    own = (lane < DIFF_HEAD_DIM, lane >= DIFF_HEAD_DIM)
    heads = range(DIFF_HEAD_BLOCK)
    c_terms = [[slopes_ref[hg * DIFF_HEAD_BLOCK + hb, r] for r in range(DIFF_SLOPE_TERMS)] for hb in heads]
    chains = [(hb, c) for hb in heads for c in range(2)]

    @pl.when(qi == 0)
    def _():
        key_feat = []
        for hb in heads:
            neg = [-cr for cr in c_terms[hb]]
            key_feat.append(_lane_features(lane, [rowf, 0.0, neg, [cr * t for cr in neg]]))
        key_tile_lanes = (lane >= DIFF_SLOPE_TERMS) & (lane < 2 * DIFF_SLOPE_TERMS)

        def build(j, carry):
            start = pl.multiple_of(j * t, t)
            for hb in heads:
                feat = jnp.where(key_tile_lanes, jnp.asarray(j, F32), key_feat[hb])
                ka_ref[hb, pl.ds(start, t), 0:hd] = k_ref[pl.ds(start, t), hb * hd:(hb + 1) * hd]
                ka_ref[hb, pl.ds(start, t), hd:2 * hd] = feat.astype(BF16)
            return carry

        lax.fori_loop(0, n_tiles, build, 0)
        for hb in heads:
            qf_ref[hb] = _lane_features(lane, [c_terms[hb], [cr * t for cr in c_terms[hb]], rowf, 0.0])

    qa = {}
    tile_lanes = (lane >= 3 * DIFF_SLOPE_TERMS) & (lane < 4 * DIFF_SLOPE_TERMS)
    for hb in heads:
        q = q_ref[:, hb * hd:(hb + 1) * hd]
        feat = jnp.where(tile_lanes, jnp.asarray(qi, F32), qf_ref[hb])
        for c in range(2):
            qa[hb, c] = jnp.concatenate([jnp.where(own[c], q, jnp.zeros_like(q)), feat.astype(BF16)], axis=1)
    acc_ref[...] = jnp.zeros_like(acc_ref)
    m_ref[...] = jnp.full(m_ref.shape, -jnp.inf, F32)

    def tiles(kjs, masked):
        starts = [pl.multiple_of(kj * t, t) for kj in kjs]
        scores = {}
        for hb, c in chains:
            parts = []
            for start in starts:
                s = _dot_nt(ka_ref[hb, pl.ds(start, t), :], qa[hb, c])
                if masked:
                    key = lax.broadcasted_iota(jnp.int32, (t, t), 0)
                    qry = lax.broadcasted_iota(jnp.int32, (t, t), 1)
                    s = jnp.where(key <= qry, s, -jnp.inf)
                parts.append(s)
            scores[hb, c] = parts
        probs = {}
        for hb, c in chains:
            parts = scores[hb, c]
            m_prev = m_ref[hb, c]
            m_new = m_prev
            for s in parts:
                m_new = jnp.maximum(m_new, jnp.max(s, axis=0, keepdims=True))
            alpha = jnp.exp2(m_prev - m_new)
            m_ref[hb, c] = m_new
            probs[hb, c] = (alpha, jnp.concatenate([jnp.exp2(s - m_new).astype(BF16) for s in parts], axis=0))
        for hb, c in chains:
            alpha, p = probs[hb, c]
            vt = jnp.concatenate([vt_ref[hb, kj] for kj in kjs], axis=1)
            acc_ref[hb, c] = alpha * acc_ref[hb, c] + _dot(vt, p)

    def body(pj, carry):
        tiles([2 * pj, 2 * pj + 1], masked=False)
        return carry

    lax.fori_loop(0, qi // 2, body, 0)

    @pl.when(qi % 2 == 1)
    def _():
        tiles([qi - 1], masked=False)

    tiles([qi], masked=True)

    lp = lam_ref[...]
    lam = (jnp.exp(jnp.sum(lp[0:1] * lp[1:2], axis=-1, keepdims=True))
           - jnp.exp(jnp.sum(lp[2:3] * lp[3:4], axis=-1, keepdims=True)) + lam_init)
    for hb in range(DIFF_HEAD_BLOCK):
        a0, a1 = acc_ref[hb, 0], acc_ref[hb, 1]
        o_t = (a0[:hd] * (1.0 / a0[hd:hd + 1]) - lam * (a1[:hd] * (1.0 / a1[hd:hd + 1])))
        o_ref[:, hb * hd:(hb + 1) * hd] = (_rms(o_t.T, ng_ref[...], SUBLN_EPS) * (1.0 - lam_init)).astype(o_ref.dtype)


def _diff_mix(q, k, v_t, lam_p, norm_g, layer_idx):
    B, S, _ = q.shape
    t = ATTN_TILE
    n_tiles = S // t
    hd = 2 * DIFF_HEAD_DIM
    hb = DIFF_HEAD_BLOCK
    lam_init = 0.8 - 0.6 * math.exp(-0.3 * layer_idx)
    slopes = jnp.asarray([2.0 ** (-8.0 * (h + 1) / DIFF_HEADS) for h in range(DIFF_HEADS)], dtype=F32)
    rest = slopes * math.log2(math.e)
    terms = []
    for _ in range(DIFF_SLOPE_TERMS):
        terms.append(rest.astype(BF16).astype(F32))
        rest = rest - terms[-1]
    slopes = jnp.stack(terms, axis=1)
    qspec = pl.BlockSpec((None, t, hb * hd), lambda b, h, i: (b, i, h))
    return pl.pallas_call(
        functools.partial(_diff_kernel, t=t, n_tiles=n_tiles, lam_init=lam_init),
        grid=(B, DIFF_HEADS // hb, n_tiles),
        in_specs=[pl.BlockSpec(memory_space=pltpu.SMEM), _resident(lam_p.shape), qspec,
                  pl.BlockSpec((None, S, hb * hd), lambda b, h, i: (b, 0, h), pipeline_mode=pl.Buffered(1)),
                  pl.BlockSpec((None, hb, n_tiles, hd + DIFF_AUX_ROWS, t), lambda b, h, i: (b, h, 0, 0, 0),
                               pipeline_mode=pl.Buffered(1)),
                  _resident((1, hd))],
        out_specs=qspec,
        out_shape=jax.ShapeDtypeStruct((B, S, D_MODEL), BF16),
        scratch_shapes=[pltpu.VMEM((hb, S, 2 * hd), BF16), pltpu.VMEM((hb, t, hd), F32),
                        pltpu.VMEM((hb, 2, hd + DIFF_AUX_ROWS, t), F32),
                        pltpu.VMEM((hb, 2, 1, t), F32)],
        compiler_params=_params("parallel", "parallel", "arbitrary"),
        name="diff_attn",
    )(slopes, lam_p, q, k, v_t, norm_g.reshape(1, hd))


def _mlstm_kernel(qk_ref, v_ref, og_ref, gc_ref, gr_ref, cw_ref, gbc_ref, gbr_ref, ng_ref, o_ref,
                  ext_ref, c_ref, n_ref, m_ref, *, L):
    @pl.when(pl.program_id(1) == 0)
    def _():
        ext_ref[0:CONV_HALO, :] = jnp.zeros((CONV_HALO, 2 * D_MODEL), F32)
        c_ref[...] = jnp.zeros_like(c_ref)
        n_ref[...] = jnp.zeros_like(n_ref)
        m_ref[...] = jnp.zeros_like(m_ref)

    for ci in range(MLSTM_CHUNKS_PER_STEP):
        _mlstm_chunk(qk_ref, v_ref, og_ref, gc_ref, gr_ref, cw_ref, gbc_ref, gbr_ref, ng_ref, o_ref,
                     ext_ref, c_ref, n_ref, m_ref, slice(ci * L, (ci + 1) * L), L)


def _mlstm_chunk(qk_ref, v_ref, og_ref, gc_ref, gr_ref, cw_ref, gbc_ref, gbr_ref, ng_ref, o_ref,
                 ext_ref, c_ref, n_ref, m_ref, rows, L):
    ext_ref[CONV_HALO:CONV_HALO + L, :] = qk_ref[rows, :]
    conv = cw_ref[MLSTM_CONV - 1:MLSTM_CONV, :] * qk_ref[rows, :]
    for j in range(1, MLSTM_CONV):
        conv = conv + (cw_ref[MLSTM_CONV - 1 - j:MLSTM_CONV - j, :]
                       * ext_ref[CONV_HALO - j:CONV_HALO - j + L, :])
    ext_ref[0:CONV_HALO, :] = ext_ref[L:L + CONV_HALO, :]
    qk = conv * _sigmoid(conv)

    gc = gc_ref[rows, :] + gbc_ref[...]
    gr = gr_ref[:, rows] + gbr_ref[...]
    b_c = _cumsum_rows(_log_sigmoid(gc))
    b_r = _cumsum_lanes(_log_sigmoid(gr))
    row = lax.broadcasted_iota(jnp.int32, (L, L), 0)
    col = lax.broadcasted_iota(jnp.int32, (L, L), 1)
    causal = col <= row

    for hd in range(MLSTM_HEADS):
        sl = slice(hd * MLSTM_HEAD_DIM, (hd + 1) * MLSTM_HEAD_DIM)
        q32 = qk[:, sl] * (MLSTM_HEAD_DIM ** -0.5)
        k32 = qk[:, D_MODEL + hd * MLSTM_HEAD_DIM:D_MODEL + (hd + 1) * MLSTM_HEAD_DIM]
        vh = v_ref[rows, sl]
        fi = MLSTM_HEADS + hd
        bcol, icol = b_c[:, fi:fi + 1], gc[:, hd:hd + 1]
        brow, irow = b_r[fi:fi + 1, :], gr[hd:hd + 1, :]
        b_last = brow[:, L - 1:L]
        m = m_ref[hd:hd + 1, 0:1]
        cmat = c_ref[hd]
        nrow = n_ref[hd]

        dmat = jnp.where(causal, bcol - brow + irow, -jnp.inf)
        inter = bcol + m
        m_t = jnp.maximum(inter, jnp.max(dmat, axis=-1, keepdims=True))
        dec = jnp.exp(inter - m_t)
        qb = q32.astype(BF16)
        k_t = k32.T
        sqk = _dot(qb, k_t.astype(BF16)) * jnp.exp(dmat - m_t)
        num = dec * _dot(qb, cmat.astype(BF16)) + _dot(sqk.astype(BF16), vh)
        den = dec * jnp.sum(q32 * nrow, axis=-1, keepdims=True) + jnp.sum(sqk, axis=-1, keepdims=True)
        hc = num * (1.0 / jnp.maximum(jnp.abs(den), jnp.exp(-m_t)))

        gs_r = b_last - brow + irow
        gs_c = b_last - bcol + icol
        m_new = jnp.maximum(b_last + m, jnp.max(gs_r, axis=-1, keepdims=True))
        carry_dec = jnp.exp(b_last + m - m_new)
        c_ref[hd] = carry_dec * cmat + _dot((k_t * jnp.exp(gs_r - m_new)).astype(BF16), vh)
        n_ref[hd] = carry_dec * nrow + jnp.sum(k32 * jnp.exp(gs_c - m_new), axis=0, keepdims=True)
        m_ref[hd:hd + 1, :] = jnp.broadcast_to(m_new, (1, LANES))

        o_ref[rows, sl] = (_rms(hc, ng_ref[:, sl], NORM_EPS) * _sigmoid(og_ref[rows, sl])).astype(o_ref.dtype)


def _mlstm_mix(qk, v, og, gates, conv_w, gate_b, norm_g):
    B, S, _ = qk.shape
    L = MLSTM_CHUNK
    H = MLSTM_HEADS
    gates_r = jnp.transpose(gates[..., :2 * H], (0, 2, 1))
    gb = gate_b.reshape(2 * H)
    gb_c = jnp.pad(gb, (0, LANES - 2 * H)).reshape(1, LANES)
    gb_r = gb.reshape(2 * H, 1)
    step = L * MLSTM_CHUNKS_PER_STEP
    tok = lambda n: pl.BlockSpec((None, step, n), lambda b, c: (b, c, 0))
    return pl.pallas_call(
        functools.partial(_mlstm_kernel, L=L),
        grid=(B, S // step),
        in_specs=[tok(2 * D_MODEL), tok(D_MODEL), tok(D_MODEL), tok(LANES),
                  pl.BlockSpec((None, 2 * H, step), lambda b, c: (b, 0, c)),
                  _resident(conv_w.shape), _resident((1, LANES)), _resident((2 * H, 1)), _resident((1, D_MODEL))],
        out_specs=tok(D_MODEL),
        out_shape=jax.ShapeDtypeStruct((B, S, D_MODEL), BF16),
        scratch_shapes=[pltpu.VMEM((L + CONV_HALO, 2 * D_MODEL), F32),
                        pltpu.VMEM((H, MLSTM_HEAD_DIM, MLSTM_HEAD_DIM), F32),
                        pltpu.VMEM((H, 1, MLSTM_HEAD_DIM), F32),
                        pltpu.VMEM((8, LANES), F32)],
        compiler_params=_params("parallel", "arbitrary"),
        name="mlstm",
    )(qk, v, og, gates, gates_r, conv_w, gb_c, gb_r, norm_g.reshape(1, D_MODEL))


def _gla_kernel(q_ref, k_ref, v_ref, g_ref, z_ref, w2_ref, gb_ref, ng_ref, o_ref, st_ref, *, L):
    @pl.when(pl.program_id(1) == 0)
    def _():
        st_ref[...] = jnp.zeros_like(st_ref)

    for ci in range(GLA_CHUNKS_PER_STEP):
        _gla_chunk(q_ref, k_ref, v_ref, g_ref, z_ref, w2_ref, gb_ref, ng_ref, o_ref, st_ref,
                   slice(ci * L, (ci + 1) * L), L)


def _gla_chunk(q_ref, k_ref, v_ref, g_ref, z_ref, w2_ref, gb_ref, ng_ref, o_ref, st_ref, rows, L):
    log_a = _log_sigmoid(_dot(z_ref[rows, :], w2_ref[...]) + gb_ref[...]) * (1.0 / GLA_TAU)
    b_all = _cumsum_rows(log_a) * LOG2E
    sub = GLA_SUBCHUNK
    row = lax.broadcasted_iota(jnp.int32, (sub, L), 0)
    col = lax.broadcasted_iota(jnp.int32, (sub, L), 1)
    ones = jnp.ones((GLA_KEY_DIM, LANES), BF16)

    for hd in range(GLA_HEADS):
        ks = slice(hd * GLA_KEY_DIM, (hd + 1) * GLA_KEY_DIM)
        vs = slice(hd * GLA_VALUE_DIM, (hd + 1) * GLA_VALUE_DIM)
        b = b_all[:, ks]
        vh = v_ref[rows, vs]
        st = st_ref[hd]
        q = q_ref[rows, ks] * (GLA_KEY_DIM ** -0.5)
        k = k_ref[rows, ks]
        b_t = b.T
        k_t = k.T
        b_last = b_t[:, L - 1:L]
        blocks = []
        for i in range(L // sub):
            r0 = i * sub
            q_i, k_i, b_i = q[r0:r0 + sub], k[r0:r0 + sub], b[r0:r0 + sub]
            terms, offs = [], []
            for s in range(sub):
                t0 = s // SUBLANES * SUBLANES
                offs.append(sum(x.shape[0] for x in terms))
                terms.append(q_i[t0:] * k_i[s:s + 1] * jnp.exp2(b_i[t0:] - b_i[s:s + 1]))
            sums = _dot(jnp.concatenate(terms, axis=0).astype(BF16), ones)
            groups = []
            for t0 in range(0, sub, SUBLANES):
                a_g = jnp.zeros((SUBLANES, L), F32)
                for s in range(min(sub, t0 + SUBLANES)):
                    lo = offs[s] + t0 - s // SUBLANES * SUBLANES
                    a_g = jnp.where(col[:SUBLANES] == r0 + s, sums[lo:lo + SUBLANES], a_g)
                groups.append(a_g)
            a_i = jnp.where(col <= r0 + row, jnp.concatenate(groups, axis=0), 0.0)
            if i > 0:
                ref_row, ref_col = b[r0 - 1:r0], b_t[:, r0 - 1:r0]
                q_s = (q_i * jnp.exp2(b_i - ref_row)).astype(BF16)
                k_s = (k_t * jnp.exp2(ref_col - b_t)).astype(BF16)
                a_i = jnp.where(col < r0, _dot(q_s, k_s), a_i)
            blocks.append(a_i)
        amat = jnp.concatenate(blocks, axis=0)
        o = _dot((q * jnp.exp2(b)).astype(BF16), st.astype(BF16)) + _dot(amat.astype(BF16), vh)
        st_ref[hd] = jnp.exp2(b_last) * st + _dot((k_t * jnp.exp2(b_last - b_t)).astype(BF16), vh)
        gate = g_ref[rows, vs]
        o_ref[rows, vs] = (_rms(o, ng_ref[:, vs], NORM_EPS) * (gate * _sigmoid(gate))).astype(o_ref.dtype)


def _gla_mix(q, k, v, g, z, gate_w2, gate_b, norm_g):
    B, S, _ = q.shape
    L = GLA_CHUNK
    kw =GLA_HEADS * GLA_KEY_DIM
    w2 = jnp.pad(gate_w2, ((0, LANES - GLA_GATE_RANK), (0, 0))).astype(BF16)
    step = L * GLA_CHUNKS_PER_STEP
    tok = lambda n: pl.BlockSpec((None, step, n), lambda b, c: (b, c, 0))
    return pl.pallas_call(
        functools.partial(_gla_kernel, L=L),
        grid=(B, S // step),
        in_specs=[tok(kw), tok(kw), tok(D_MODEL), tok(D_MODEL), tok(LANES),
                  _resident(w2.shape), _resident((1, kw)), _resident((1, D_MODEL))],
        out_specs=tok(D_MODEL),
        out_shape=jax.ShapeDtypeStruct((B, S, D_MODEL), BF16),
        scratch_shapes=[pltpu.VMEM((GLA_HEADS, GLA_KEY_DIM, GLA_VALUE_DIM), F32)],
        compiler_params=_params("parallel", "arbitrary"),
        name="gla",
    )(q, k, v, g, z, w2, gate_b.reshape(1, kw), norm_g.reshape(1, D_MODEL))


def kernel(x, mem, norm_g, ffn_w_up, ffn_w_down, mem_norm_g, mem_w_kv, pool_w_in, pool_w_group, pool_scale, pool_w_out, diff_w_in, diff_lambda, diff_norm_g, diff_w_out, mlstm_w_in, mlstm_conv_w, mlstm_gate_b, mlstm_norm_g, mlstm_w_out, gla_w_in, gla_gate_w2, gla_gate_b, gla_norm_g, gla_w_out, final_norm_g):
    B, S, D = x.shape
    M = mem.shape[1]
    T = B * S
    depth = norm_g.shape[0]
    n_mixers = 4
    mem2 = mem.reshape(B * M, D)
    x2 = x.reshape(T, D)
    norm_g4 = norm_g.reshape(depth, 3, 1, D)
    w_up, w_down = ffn_w_up[0, 0].astype(BF16), ffn_w_down[0, 0].astype(BF16)

    def seq(a):
        return a.reshape(B, S, a.shape[-1])

    for i in range(depth):
        kind, j = i % n_mixers, i // n_mixers
        k_scale = X_HEAD_DIM ** -0.5 * LOG2E
        mem_k, mem_v = _norm_proj(mem2, mem_norm_g, [(mem_w_kv[i][:, :X_WIDTH] * k_scale, BF16),
                                                     (mem_w_kv[i][:, X_WIDTH:], BF16)])
        mem_k = mem_k.reshape(B, M, X_WIDTH)
        mem_v = mem_v.reshape(B, M, X_WIDTH)
        x2, w_up, w_down = _ffn(x2, _ffn_weights(norm_g4, w_up, w_down, i, 0), (ffn_w_up, ffn_w_down, i, 1))
        if kind == 0:
            mix, xq = _pool_mix(seq(x2), norm_g[i, 1], pool_w_in[j], pool_w_group[j], pool_scale[j])
            w_out = pool_w_out[j]
        elif kind == 1:
            w = diff_w_in[j]
            q_scale = DIFF_HEAD_DIM ** -0.5 * math.log2(math.e)
            q, k, v_t, xq = _norm_proj(x2, norm_g[i, 1], [
                (w[:, :D] * q_scale, BF16), (w[:, D:2 * D], BF16), (w[:, 2 * D:3 * D], BF16, "values_t"),
                (w[:, 3 * D:], BF16)], seq_len=S)
            mix = _diff_mix(seq(q), seq(k), v_t, diff_lambda[j], diff_norm_g[j], i)
            w_out = diff_w_out[j]
        elif kind == 2:
            w = mlstm_w_in[j]
            ng = 2 * MLSTM_HEADS
            qk, v, og, gates, xq = _norm_proj(x2, norm_g[i, 1], [
                (w[:, :2 * D], F32), (w[:, 2 * D:3 * D], BF16), (w[:, 3 * D:4 * D], F32),
                (w[:, 4 * D:4 * D + ng], F32), (w[:, 4 * D + ng:], BF16)])
            mix = _mlstm_mix(seq(qk), seq(v), seq(og), seq(gates), mlstm_conv_w[j], mlstm_gate_b[j], mlstm_norm_g[j])
            w_out = mlstm_w_out[j]
        else:
            w = gla_w_in[j]
            kw = GLA_HEADS * GLA_KEY_DIM
            o0 = 2 * kw + 2 * D
            q, k, v, g, z, xq = _norm_proj(x2, norm_g[i, 1], [
                (w[:, :kw], F32), (w[:, kw:2 * kw], F32), (w[:, 2 * kw:2 * kw + D], BF16),
                (w[:, 2 * kw + D:o0], F32), (w[:, o0:o0 + GLA_GATE_RANK], BF16), (w[:, o0 + GLA_GATE_RANK:], BF16)])
            mix = _gla_mix(seq(q), seq(k), seq(v), seq(g), seq(z), gla_gate_w2[j], gla_gate_b[j], gla_norm_g[j])
            w_out = gla_w_out[j]
        last = i == depth - 1
        x3, w_up, w_down = _attn_out_ffn(seq(x2), mix, seq(xq), mem_k, mem_v, w_out,
                                         _ffn_weights(norm_g4, w_up, w_down, i, 2),
                                         final_g=final_norm_g if last else None,
                                         next_cast=None if last else (ffn_w_up, ffn_w_down, i + 1, 0))
        x2 = x3.reshape(T, D)
    return x2.reshape(B, S, D)
```
